```python
import jax, jax.numpy as jnp
from jax import lax
import numpy as np

D_MODEL = 1024
BATCH = 8
SEQ = 8192
DEPTH = 2
DEC_BATCH = 8
DEC_SEQ = 64
PAST_LEN = 2048

CHUNK = 64
QBLOCK = 128
ROPE_THETA = 10000.0
HEAD_DIM = 64
LN_EPS = 1e-5
A_HEADS = 8
A_KV_DIM = 64
IDX_HEADS = 8
IDX_DIM = 64
TOPK_MAX = 256
B_HEADS = 4
B_DK = 64
B_DV = 128
C_HEADS = 8
D_HEADS = 8
D_LORA_W = 64
D_LORA_A = 64
D_LORA_G = 128
D_GN_EPS = 64e-5
N_EXPERTS = 16
N_GROUPS = 4
EXPERTS_PER_GROUP = N_EXPERTS // N_GROUPS
TOP_K_EXPERTS = 2
D_EXPERT = 256
ALPHA = (2 * DEPTH) ** 0.25
BETA_INIT = (8 * DEPTH) ** -0.25
N_AB = (DEPTH + 1) // 2
N_CD = DEPTH // 2
A_W = A_HEADS * HEAD_DIM
B_W = B_HEADS * B_DV
C_W = C_HEADS * HEAD_DIM
D_W = D_HEADS * HEAD_DIM
AB_WIDTHS = (A_W, A_KV_DIM, A_KV_DIM, IDX_HEADS * IDX_DIM, IDX_DIM, IDX_HEADS, B_HEADS * B_DK, B_HEADS * B_DK, B_W, B_W)
AB_IN = sum(AB_WIDTHS)
AB_MIX = A_W + B_W
D_WIDTHS = (D_W, D_W, D_W, D_LORA_W, D_LORA_A, D_LORA_G)
D_SHIFT_W = sum(D_WIDTHS)
CD_WIDTHS = (C_W, C_W, C_W, D_SHIFT_W)
CD_IN = sum(CD_WIDTHS)
CD_MIX = C_W + D_W

kernel_name = 'hybrid_streaming_dsa_retention_stickbreak_rwkv7_moe'

F32 = jnp.float32


def _split(x, widths):
    return jnp.split(x, [int(c) for c in np.cumsum(widths)[:-1]], axis=-1)


def _rope(x, pos):
    half = x.shape[-1] // 2
    inv = ROPE_THETA ** (-jnp.arange(half, dtype=F32) / half)
    ang = pos.astype(F32)[:, None] * inv[None, :]
    ang = ang.reshape(ang.shape[0], *([1] * (x.ndim - 3)), half)
    cos = jnp.cos(ang).astype(x.dtype)
    sin = jnp.sin(ang).astype(x.dtype)
    x1, x2 = x[..., :half], x[..., half:]
    return jnp.concatenate([x1 * cos - x2 * sin, x1 * sin + x2 * cos], axis=-1)


def _layernorm(x, g, b):
    xf = x.astype(F32)
    mu = jnp.mean(xf, -1, keepdims=True)
    var = jnp.mean(jnp.square(xf - mu), -1, keepdims=True)
    return ((xf - mu) * lax.rsqrt(var + LN_EPS) * g + b).astype(x.dtype)


def _headnorm(o, g, b, eps):
    of = o.astype(F32)
    mu = jnp.mean(of, -1, keepdims=True)
    var = jnp.mean(jnp.square(of - mu), -1, keepdims=True)
    y = ((of - mu) * lax.rsqrt(var + eps)).reshape(*o.shape[:-2], -1) * g
    return y if b is None else y + b


def _to_blocks(x, qb):
    b, t = x.shape[:2]
    return jnp.moveaxis(x.reshape(b, t // qb, qb, *x.shape[2:]), 1, 0)


def _from_blocks(x):
    nb, b, qb = x.shape[:3]
    return jnp.moveaxis(x, 0, 1).reshape(b, nb * qb, *x.shape[3:])


def _dsa_block(q, iq, iw, q_pos, k, v, ik, k_pos, topk):
    rel = jax.nn.relu(jnp.einsum('bqhd,bsd->bqhs', iq, ik).astype(F32))
    score = jnp.einsum('bqh,bqhs->bqs', iw.astype(F32), rel)
    visible = (k_pos[None, :] // CHUNK) <= (q_pos[:, None] // CHUNK)
    score = jnp.where(visible[None], score, -jnp.inf)
    _, sel = lax.top_k(score, topk)
    sel_ok = (k_pos[sel] // CHUNK) <= (q_pos[None, :, None] // CHUNK)
    gather = jax.vmap(lambda rows, idx: rows[idx])
    k_sel = gather(k, sel)
    v_sel = gather(v, sel)
    logits = jnp.einsum('bqhd,bqkd->bqhk', q, k_sel).astype(F32) * HEAD_DIM ** -0.5
    logits = jnp.where(sel_ok[:, :, None, :], logits, -jnp.inf)
    p = jax.nn.softmax(logits, axis=-1).astype(v.dtype)
    return jnp.einsum('bqhk,bqkd->bqhd', p, v_sel)


def _retention(q, k, v, s0):
    bsz, t = q.shape[:2]
    n = min(CHUNK, t)
    log_g = jnp.log(1.0 - 2.0 ** (-5.0 - jnp.arange(B_HEADS, dtype=F32)))
    i = jnp.arange(n, dtype=F32)
    diff = i[:, None] - i[None, :]
    intra = jnp.where(diff >= 0, jnp.exp(jnp.maximum(diff, 0.0)[None] * log_g[:, None, None]), 0.0)
    q_decay = jnp.exp((i[:, None] + 1.0) * log_g[None, :])
    k_decay = jnp.exp((n - 1.0 - i)[:, None] * log_g[None, :])
    chunk_decay = jnp.exp(n * log_g)

    def step(s, c):
        qc, kc, vc = c
        scores = jnp.einsum('bihd,bjhd->bhij', qc, kc) * intra
        o = jnp.einsum('bhij,bjhv->bihv', scores, vc) + jnp.einsum('bihd,bhdv->bihv', qc, s) * q_decay[None, :, :, None]
        s = s * chunk_decay[None, :, None, None] + jnp.einsum('bjhd,bjhv->bhdv', kc * k_decay[None, :, :, None], vc)
        return s, o

    def to_chunks(a):
        return jnp.moveaxis(a.astype(F32).reshape(bsz, t // n, n, *a.shape[2:]), 1, 0)

    s, o = lax.scan(step, s0.astype(F32), (to_chunks(q), to_chunks(k), to_chunks(v)))
    return jnp.moveaxis(o, 0, 1).reshape(bsz, t, *o.shape[3:]), s.astype(s0.dtype)


def _stick_block(q, q_pos, k, v, k_pos):
    z = jnp.einsum('bqhd,bshd->bhqs', q, k).astype(F32) * HEAD_DIM ** -0.5
    strict = k_pos[None, :] < q_pos[:, None]
    log_keep = jnp.where(strict, -jax.nn.softplus(z), 0.0)
    later = lax.cumsum(log_keep, axis=3, reverse=True) - log_keep
    a = jnp.where(strict, jnp.exp(jax.nn.log_sigmoid(z) + later), 0.0)
    return jnp.einsum('bhqs,bshd->bqhd', a.astype(v.dtype), v)


def _rwkv7_scan(s0, r, w, k, v, kk, a):
    def step(s, inp):
        r_t, w_t, k_t, v_t, kk_t, a_t = inp
        sa = jnp.einsum('bhvk,bhk->bhv', s, kk_t)
        s = s * w_t[:, :, None, :] - sa[..., None] * (kk_t * a_t)[:, :, None, :] + v_t[..., None] * k_t[:, :, None, :]
        return s, jnp.einsum('bhvk,bhk->bhv', s, r_t)

    xs = tuple(jnp.moveaxis(u, 1, 0) for u in (r, w, k, v, kk, a))
    s, o = lax.scan(step, s0, xs)
    return jnp.moveaxis(o, 0, 1), s


def _mixer_ab(x, past_k, past_v, past_ik, s_b, w_in, w_out, b_gn):
    bsz, t, _ = x.shape
    past = 0 if past_k is None else past_k.shape[1]
    pos = past + jnp.arange(t)
    aq, ak, av, iq, ik, iw, bq, bk, bv, bg = _split(x @ w_in, AB_WIDTHS)
    aq = _rope(aq.reshape(bsz, t, A_HEADS, HEAD_DIM), pos)
    ak = _rope(ak, pos)
    iq = _rope(iq.reshape(bsz, t, IDX_HEADS, IDX_DIM), pos)
    ik = _rope(ik, pos)
    iw = iw * (IDX_HEADS * IDX_DIM) ** -0.5
    k_all = ak if past_k is None else jnp.concatenate([past_k, ak], axis=1)
    v_all = av if past_v is None else jnp.concatenate([past_v, av], axis=1)
    ik_all = ik if past_ik is None else jnp.concatenate([past_ik, ik], axis=1)
    n_keys = k_all.shape[1]
    k_pos = jnp.arange(n_keys)
    topk = min(TOPK_MAX, n_keys // 4)
    qb = min(QBLOCK, t)
    blocks = (_to_blocks(aq, qb), _to_blocks(iq, qb), _to_blocks(iw, qb), pos.reshape(-1, qb))
    o_a = _from_blocks(lax.map(lambda blk: _dsa_block(blk[0], blk[1], blk[2], blk[3], k_all, v_all, ik_all, k_pos, topk), blocks))
    o_a = o_a.reshape(bsz, t, A_W)
    bq = _rope(bq.reshape(bsz, t, B_HEADS, B_DK), pos)
    bk = _rope(bk.reshape(bsz, t, B_HEADS, B_DK), pos) * B_DK ** -0.5
    bv = bv.reshape(bsz, t, B_HEADS, B_DV)
    o_b, s_new = _retention(bq, bk, bv, s_b)
    o_b = (_headnorm(o_b, b_gn, None, LN_EPS) * jax.nn.silu(bg)).astype(x.dtype)
    y = jnp.concatenate([o_a, o_b], axis=-1) @ w_out
    return y, (ak, av, ik, s_new)


def _mixer_cd(x, past_k, past_v, s_d, shift_d, w_in, w_out, mu, w0, ww2, a0, aw2, gw2, k_k, k_a, r_k, lnx_g, lnx_b):
    bsz, t, _ = x.shape
    past = 0 if past_k is None else past_k.shape[1]
    pos = past + jnp.arange(t)
    cq, ck, cv, pd = _split(x @ w_in, CD_WIDTHS)
    cq = cq.reshape(bsz, t, C_HEADS, HEAD_DIM)
    ck = ck.reshape(bsz, t, C_HEADS, HEAD_DIM)
    cv = cv.reshape(bsz, t, C_HEADS, HEAD_DIM)
    k_all = ck if past_k is None else jnp.concatenate([past_k, ck], axis=1)
    v_all = cv if past_v is None else jnp.concatenate([past_v, cv], axis=1)
    k_pos = jnp.arange(k_all.shape[1])
    qb = min(QBLOCK, t)
    o_c = _from_blocks(lax.map(lambda blk: _stick_block(blk[0], blk[1], k_all, v_all, k_pos), (_to_blocks(cq, qb), pos.reshape(-1, qb))))
    o_c = o_c.reshape(bsz, t, C_W)
    prev = jnp.concatenate([shift_d, pd[:, :-1]], axis=1)
    pm = pd + (prev - pd) * mu
    r, k, v, lw, la, lg = _split(pm, D_WIDTHS)
    w_log = -jax.nn.softplus(-(w0 + jnp.tanh(lw) @ ww2).astype(F32)) - 0.5
    decay = jnp.exp(-jnp.exp(w_log))
    a = jax.nn.sigmoid((a0 + la @ aw2).astype(F32))
    g = jax.nn.sigmoid(lg) @ gw2

    def heads(u):
        return u.astype(F32).reshape(bsz, t, D_HEADS, HEAD_DIM)

    kk = heads(k * k_k)
    kk = kk * lax.rsqrt(jnp.sum(kk * kk, -1, keepdims=True) + 1e-12)
    k_rep = k.astype(F32) * (1.0 + (a - 1.0) * k_a)
    rh, kh, vh = heads(r), heads(k_rep), heads(v)
    o_d, s_new = _rwkv7_scan(s_d.astype(F32), rh, heads(decay), kh, vh, kk, heads(a))
    bonus = jnp.sum(rh * kh * r_k.reshape(D_HEADS, HEAD_DIM), -1, keepdims=True) * vh
    o_d = ((_headnorm(o_d, lnx_g, lnx_b, D_GN_EPS) + bonus.reshape(bsz, t, D_W)) * g).astype(x.dtype)
    y = jnp.concatenate([o_c, o_d], axis=-1) @ w_out
    return y, (ck, cv, s_new.astype(s_d.dtype), pd[:, -1:])


def _moe(x, router_w, router_b, w1, w3, w2):
    aff = jax.nn.sigmoid(jnp.einsum('btd,de->bte', x, router_w).astype(F32))
    biased = aff + router_b.astype(F32)
    grp = biased.reshape(*biased.shape[:-1], N_GROUPS, EXPERTS_PER_GROUP)
    grp_score = jnp.sum(lax.top_k(grp, TOP_K_EXPERTS)[0], -1)
    best = jnp.argmax(grp_score, -1)
    expert_group = jnp.arange(N_EXPERTS) // EXPERTS_PER_GROUP
    masked = jnp.where(expert_group == best[..., None], biased, -jnp.inf)
    _, top_idx = lax.top_k(masked, TOP_K_EXPERTS)
    top_aff = jnp.take_along_axis(aff, top_idx, axis=-1)
    top_gate = top_aff / jnp.sum(top_aff, -1, keepdims=True)
    gate = jnp.einsum('btk,btke->bte', top_gate, jax.nn.one_hot(top_idx, N_EXPERTS, dtype=F32)).astype(x.dtype)
    y = jnp.zeros_like(x)
    for e in range(N_EXPERTS):
        h = jax.nn.silu(x @ w1[e]) * (x @ w3[e])
        y = y + gate[..., e:e + 1] * (h @ w2[e])
    return y


def _run_group(x, past, p):
    bsz = x.shape[0]
    new = [[] for _ in range(8)]
    for layer in range(DEPTH):
        slot = layer // 2
        if layer % 2 == 0:
            if past is None:
                pk = pv = pik = None
                sb = jnp.zeros((bsz, B_HEADS, B_DK, B_DV), x.dtype)
            else:
                pk, pv, pik, sb = past[0][slot], past[1][slot], past[2][slot], past[3][slot]
            y, st = _mixer_ab(x, pk, pv, pik, sb, p['w_in_ab'][slot], p['w_out_ab'][slot], p['b_gn'][slot])
            for j in range(4):
                new[j].append(st[j])
        else:
            if past is None:
                pk = pv = None
                sd = jnp.zeros((bsz, D_HEADS, HEAD_DIM, HEAD_DIM), x.dtype)
                sh = jnp.zeros((bsz, 1, D_SHIFT_W), x.dtype)
            else:
                pk, pv, sd, sh = past[4][slot], past[5][slot], past[6][slot], past[7][slot]
            y, st = _mixer_cd(x, pk, pv, sd, sh, p['w_in_cd'][slot], p['w_out_cd'][slot], p['d_mu'][slot], p['d_w0'][slot], p['d_w2'][slot], p['d_a0'][slot], p['d_a2'][slot], p['d_g2'][slot], p['d_k_k'][slot], p['d_k_a'][slot], p['d_r_k'][slot], p['d_lnx_g'][slot], p['d_lnx_b'][slot])
            for j in range(4):
                new[4 + j].append(st[j])
        x = _layernorm(ALPHA * x + y, p['ln_g'][layer, 0], p['ln_b'][layer, 0])
        y = _moe(x, p['router_w'], p['router_b'], p['moe_w1'][layer], p['moe_w3'][layer], p['moe_w2'][layer])
        x = _layernorm(ALPHA * x + y, p['ln_g'][layer, 1], p['ln_b'][layer, 1])
    return x, [jnp.stack(s) for s in new]


def setup_inputs(seed: int = 0) -> dict:
    key = jax.random.key(seed)

    def nrm(i, shape, scale):
        return scale * jax.random.normal(jax.random.fold_in(key, i), shape, F32)

    def near_one(i, shape):
        return 1.0 + nrm(i, shape, 0.02)

    def unif(i, shape, lo, hi):
        return jax.random.uniform(jax.random.fold_in(key, i), shape, F32, minval=lo, maxval=hi)

    return {
        'x_prompt': nrm(0, (BATCH, SEQ, D_MODEL), 1.0),
        'x_sample': nrm(1, (DEC_BATCH, DEC_SEQ, D_MODEL), 1.0),
        'cache_a_k': nrm(2, (N_AB, DEC_BATCH, PAST_LEN, A_KV_DIM), 1.0),
        'cache_a_v': nrm(3, (N_AB, DEC_BATCH, PAST_LEN, A_KV_DIM), 1.0),
        'cache_a_idx_k': nrm(4, (N_AB, DEC_BATCH, PAST_LEN, IDX_DIM), 1.0),
        'state_b': nrm(5, (N_AB, DEC_BATCH, B_HEADS, B_DK, B_DV), 0.5),
        'cache_c_k': nrm(6, (N_CD, DEC_BATCH, PAST_LEN, C_HEADS, HEAD_DIM), 1.0),
        'cache_c_v': nrm(7, (N_CD, DEC_BATCH, PAST_LEN, C_HEADS, HEAD_DIM), 1.0),
        'state_d': nrm(8, (N_CD, DEC_BATCH, D_HEADS, HEAD_DIM, HEAD_DIM), 0.5),
        'state_d_shift': nrm(9, (N_CD, DEC_BATCH, 1, D_SHIFT_W), 1.0),
        'w_in_ab': nrm(10, (N_AB, D_MODEL, AB_IN), D_MODEL ** -0.5),
        'w_out_ab': nrm(11, (N_AB, AB_MIX, D_MODEL), BETA_INIT * AB_MIX ** -0.5),
        'b_gn': near_one(12, (N_AB, B_W)),
        'w_in_cd': nrm(13, (N_CD, D_MODEL, CD_IN), D_MODEL ** -0.5),
        'w_out_cd': nrm(14, (N_CD, CD_MIX, D_MODEL), BETA_INIT * CD_MIX ** -0.5),
        'd_mu': unif(15, (N_CD, D_SHIFT_W), 0.0, 1.0),
        'd_w0': unif(16, (N_CD, D_W), -4.0, 1.0),
        'd_w2': nrm(17, (N_CD, D_LORA_W, D_W), 0.5 * D_LORA_W ** -0.5),
        'd_a0': nrm(18, (N_CD, D_W), 0.1),
        'd_a2': nrm(19, (N_CD, D_LORA_A, D_W), 0.5 * D_LORA_A ** -0.5),
        'd_g2': nrm(20, (N_CD, D_LORA_G, D_W), D_LORA_G ** -0.5),
        'd_k_k': 1.0 + nrm(21, (N_CD, D_W), 0.1),
        'd_k_a': 1.0 + nrm(22, (N_CD, D_W), 0.1),
        'd_r_k': nrm(23, (N_CD, D_W), 0.1),
        'd_lnx_g': near_one(24, (N_CD, D_W)),
        'd_lnx_b': nrm(25, (N_CD, D_W), 0.02),
        'ln_g': near_one(26, (DEPTH, 2, D_MODEL)),
        'ln_b': nrm(27, (DEPTH, 2, D_MODEL), 0.02),
        'router_w': nrm(28, (D_MODEL, N_EXPERTS), D_MODEL ** -0.5),
        'router_b': nrm(29, (N_EXPERTS,), 0.01),
        'moe_w1': nrm(30, (DEPTH, N_EXPERTS, D_MODEL, D_EXPERT), D_MODEL ** -0.5),
        'moe_w3': nrm(31, (DEPTH, N_EXPERTS, D_MODEL, D_EXPERT), D_MODEL ** -0.5),
        'moe_w2': nrm(32, (DEPTH, N_EXPERTS, D_EXPERT, D_MODEL), BETA_INIT * D_EXPERT ** -0.5),
    }


def reference(x_prompt, x_sample, cache_a_k, cache_a_v, cache_a_idx_k, state_b, cache_c_k, cache_c_v, state_d, state_d_shift, w_in_ab, w_out_ab, b_gn, w_in_cd, w_out_cd, d_mu, d_w0, d_w2, d_a0, d_a2, d_g2, d_k_k, d_k_a, d_r_k, d_lnx_g, d_lnx_b, ln_g, ln_b, router_w, router_b, moe_w1, moe_w3, moe_w2):
    p = {'w_in_ab': w_in_ab, 'w_out_ab': w_out_ab, 'b_gn': b_gn, 'w_in_cd': w_in_cd, 'w_out_cd': w_out_cd,
         'd_mu': d_mu, 'd_w0': d_w0, 'd_w2': d_w2, 'd_a0': d_a0, 'd_a2': d_a2, 'd_g2': d_g2,
         'd_k_k': d_k_k, 'd_k_a': d_k_a, 'd_r_k': d_r_k, 'd_lnx_g': d_lnx_g, 'd_lnx_b': d_lnx_b,
         'ln_g': ln_g, 'ln_b': ln_b, 'router_w': router_w, 'router_b': router_b,
         'moe_w1': moe_w1, 'moe_w3': moe_w3, 'moe_w2': moe_w2}
    y_prompt, sp = _run_group(x_prompt, None, p)
    past = (cache_a_k, cache_a_v, cache_a_idx_k, state_b, cache_c_k, cache_c_v, state_d, state_d_shift)
    y_sample, ss = _run_group(x_sample, past, p)
    return (y_prompt, y_sample, sp[0], sp[1], sp[2], ss[0], ss[1], ss[2], sp[3], ss[3], sp[4], sp[5], ss[4], ss[5], sp[6], ss[6], sp[7], ss[7])
```

```python
import functools
import math

import jax
import jax.numpy as jnp
import numpy as np
from jax import lax
from jax.experimental import pallas as pl
from jax.experimental.pallas import tpu as pltpu

F32 = jnp.float32
BF16 = jnp.bfloat16

D_MODEL = 1024
CHUNK = 64
QBLOCK = 128
ROPE_THETA = 10000.0
HEAD_DIM = 64
LN_EPS = 1e-5
A_HEADS = 8
IDX_HEADS = 8
TOPK_MAX = 256
B_HEADS = 4
B_DK = 64
B_DV = 128
C_HEADS = 8
D_HEADS = 8
D_LORA_W = 64
D_LORA_A = 64
D_LORA_G = 128
D_GN_EPS = 64e-5
N_EXPERTS = 16
N_GROUPS = 4
EXPERTS_PER_GROUP = 4
D_EXPERT = 256
DEPTH = 2
ALPHA = (2 * DEPTH) ** 0.25
D_W = D_HEADS * HEAD_DIM
D_SHIFT_W = 3 * D_W + D_LORA_W + D_LORA_A + D_LORA_G

LANES = 128
SUBLANES = 8
VMEM_LIMIT_BYTES = 56 * 1024 * 1024

NEG_BIG = -1e30
INT_MIN = -(2 ** 31)
EXP_ZERO_BELOW = -104.0

NT_DIMS = (((1,), (1,)), ((), ()))
TN_DIMS = (((0,), (0,)), ((), ()))


def _params(*sem):
    return pltpu.CompilerParams(dimension_semantics=sem, vmem_limit_bytes=VMEM_LIMIT_BYTES)


def _dot(a, b):
    return jnp.dot(a, b, preferred_element_type=F32)


def _dot_hi(a, b):
    return jnp.dot(a, b, preferred_element_type=F32, precision=lax.Precision.HIGHEST)


def _split3(x):
    h1 = x.astype(BF16)
    r1 = x - h1.astype(F32)
    h2 = r1.astype(BF16)
    r2 = r1 - h2.astype(F32)
    return h1, h2, r2.astype(BF16)


def _dot_exact_rhs(x, m01):
    h1, h2, h3 = _split3(x)
    return _dot(h1, m01) + _dot(h2, m01) + _dot(h3, m01)


def _layernorm_rows(x, g, b):
    mu = jnp.mean(x, axis=-1, keepdims=True)
    xc = x - mu
    var = jnp.mean(xc * xc, axis=-1, keepdims=True)
    return xc * lax.rsqrt(var + LN_EPS) * g + b


def _sigmoid(x):
    return 1.0 / (1.0 + jnp.exp(-x))


def _softplus(x):
    return jnp.maximum(x, 0.0) + jnp.log1p(jnp.exp(-jnp.abs(x)))


def _rope_slab(x, cos, sin_signed):
    lane = lax.broadcasted_iota(jnp.int32, x.shape, 1)
    first_half = (lane % HEAD_DIM) < (HEAD_DIM // 2)
    swapped = jnp.where(first_half, pltpu.roll(x, LANES - HEAD_DIM // 2, 1), pltpu.roll(x, HEAD_DIM // 2, 1))
    return x * cos + swapped * sin_signed


AB_ROPED = 1664
AB_PACKED = 2816


def _pack_w_in_ab(w):
    aq, ak, av, iq, ik, iw, bq, bk, bv, bg = jnp.split(
        w, [512, 576, 640, 1152, 1216, 1224, 1480, 1736, 2248], axis=1)
    pad = jnp.zeros((w.shape[0], LANES - 64 - IDX_HEADS), w.dtype)
    return jnp.concatenate([aq, iq, bq, bk, ak, ik, av, iw, pad, bv, bg], axis=1).astype(BF16)


def _proj_ab_kernel(x_ref, w_ref, cos_ref, sin_ref, aq_ref, iq_ref, bqk_ref, kik_ref, aviw_ref, bv_ref, bg_ref):
    xb = x_ref[...].astype(BF16)
    cos = cos_ref[...]
    sin = sin_ref[...]

    def roped(col0, width, scale_from=None):
        y = _dot(xb, w_ref[:, col0:col0 + width])
        parts = []
        for c in range(width // LANES):
            slab = _rope_slab(y[:, c * LANES:(c + 1) * LANES], cos, sin)
            if scale_from is not None and c * LANES >= scale_from:
                slab = slab * (B_DK ** -0.5)
            parts.append(slab)
        return parts

    for c, slab in enumerate(roped(0, 512)):
        aq_ref[:, c * LANES:(c + 1) * LANES] = slab
    for c, slab in enumerate(roped(512, 512)):
        iq_ref[:, c * LANES:(c + 1) * LANES] = slab
    for c, slab in enumerate(roped(1024, 512, scale_from=256)):
        bqk_ref[:, c * LANES:(c + 1) * LANES] = slab
    kik_ref[...] = roped(1536, LANES)[0]
    aviw = _dot(xb, w_ref[:, AB_ROPED:AB_ROPED + LANES])
    lane = lax.broadcasted_iota(jnp.int32, aviw.shape, 1)
    is_iw = (lane >= 64) & (lane < 64 + IDX_HEADS)
    aviw_ref[...] = jnp.where(is_iw, aviw * ((IDX_HEADS * HEAD_DIM) ** -0.5), aviw)
    bv_ref[...] = _dot(xb, w_ref[:, 1792:2304])
    bg_ref[...] = _dot(xb, w_ref[:, 2304:2816])


def _proj_ab(x2, w_packed, cos, sin, t_len, tm):
    n = x2.shape[0]
    nt = t_len // tm
    row = lambda w: pl.BlockSpec((tm, w), lambda i: (i, 0))
    tab = pl.BlockSpec((tm, LANES), lambda i: (i % nt, 0))
    outs = [512, 512, 512, LANES, LANES, 512, 512]
    return pl.pallas_call(
        _proj_ab_kernel,
        grid=(n // tm,),
        in_specs=[row(D_MODEL), pl.BlockSpec((D_MODEL, AB_PACKED), lambda i: (0, 0)), tab, tab],
        out_specs=[row(w) for w in outs],
        out_shape=[jax.ShapeDtypeStruct((n, w), F32) for w in outs],
        compiler_params=_params("parallel"),
        name="proj_ab",
    )(x2, w_packed, cos, sin)


def _rope_tables(past, t_len):
    half = HEAD_DIM // 2
    inv = ROPE_THETA ** (-jnp.arange(half, dtype=F32) / half)
    ang = (past + jnp.arange(t_len)).astype(F32)[:, None] * inv[None, :]
    c, s = jnp.cos(ang), jnp.sin(ang)
    return jnp.concatenate([c, c, c, c], axis=1), jnp.concatenate([-s, s, -s, s], axis=1)


def _dsa_kernel(aq_ref, iq_ref, aviw_ref, kik_ref, v_ref, tri_ref, o_ref,
                skey_ref, qs_ref, p_ref, m_ref, l_ref, acc_ref, *, past, qb, kt, topk):
    i = pl.program_id(1)
    q0 = past + i * qb
    n_tiles = (q0 + qb + kt - 1) // kt
    row = lax.broadcasted_iota(jnp.int32, (qb, 1), 0)
    vis_end = ((q0 + row) // CHUNK + 1) * CHUNK
    lane_kt = lax.broadcasted_iota(jnp.int32, (1, kt), 1)

    iq = iq_ref[...].astype(BF16)
    iw = aviw_ref[:, 64:64 + IDX_HEADS]

    def score_tile(j, carry):
        off = pl.multiple_of(j * kt, kt)
        ik = kik_ref[pl.ds(off, kt), 64:128]
        acc = jnp.zeros((qb, kt), F32)
        for h in range(IDX_HEADS):
            s = lax.dot_general(iq[:, h * 64:(h + 1) * 64], ik, NT_DIMS, preferred_element_type=F32)
            acc = acc + iw[:, h:h + 1] * jnp.maximum(s, 0.0)
        acc = acc + 0.0
        acc = jnp.where(off + lane_kt < vis_end, acc, -jnp.inf)
        bits = pltpu.bitcast(acc, jnp.int32)
        skey_ref[:, pl.ds(off, kt)] = jnp.where(bits < 0, bits ^ 0x7FFFFFFF, bits)
        return carry

    lax.fori_loop(0, n_tiles, score_tile, 0)

    def count(pred_fn):
        def body(j, cnt):
            off = pl.multiple_of(j * kt, kt)
            hit = pred_fn(skey_ref[:, pl.ds(off, kt)])
            for c in range(kt // LANES):
                cnt = cnt + jnp.where(hit[:, c * LANES:(c + 1) * LANES], 1.0, 0.0)
            return cnt
        cnt = lax.fori_loop(0, n_tiles, body, jnp.zeros((qb, LANES), F32))
        return jnp.sum(cnt, axis=1, keepdims=True)

    zero = jnp.zeros((qb, 1), jnp.int32)
    t0 = jnp.where(count(lambda x: x >= zero) >= topk, zero, jnp.full((qb, 1), INT_MIN, jnp.int32))

    def bit_step(it, t):
        cand = t | jnp.left_shift(jnp.int32(1), 30 - it)
        return jnp.where(count(lambda x: x >= cand) >= topk, cand, t)

    thr = lax.fori_loop(0, 31, bit_step, t0)
    need = topk - count(lambda x: x > thr)

    for h in range(A_HEADS):
        qs_ref[h * qb:(h + 1) * qb, :] = (aq_ref[:, h * 64:(h + 1) * 64] * (HEAD_DIM ** -0.5)).astype(BF16)
    m_ref[...] = jnp.full(m_ref.shape, NEG_BIG, F32)
    l_ref[...] = jnp.zeros(l_ref.shape, F32)
    acc_ref[...] = jnp.zeros(acc_ref.shape, F32)

    def attn_tile(j, eq_seen):
        off = pl.multiple_of(j * kt, kt)
        key = skey_ref[:, pl.ds(off, kt)]
        eq = key == thr
        rank = _dot(jnp.where(eq, 1.0, 0.0).astype(BF16), tri_ref[...]) + eq_seen
        sel = ((key > thr) | (eq & (rank <= need))) & (off + lane_kt < vis_end)
        k = kik_ref[pl.ds(off, kt), 0:64]
        s = lax.dot_general(qs_ref[...], k, NT_DIMS, preferred_element_type=F32)
        alphas = []
        for h in range(A_HEADS):
            sh = jnp.where(sel, s[h * qb:(h + 1) * qb, :], NEG_BIG)
            m_old = m_ref[h]
            m_new = jnp.maximum(m_old, jnp.max(sh, axis=1, keepdims=True))
            p = jnp.where(sel, jnp.exp(sh - m_new), 0.0)
            alpha = jnp.exp(m_old - m_new)
            l_ref[h] = alpha * l_ref[h] + jnp.sum(p, axis=1, keepdims=True)
            m_ref[h] = m_new
            p_ref[h * qb:(h + 1) * qb, :] = p.astype(BF16)
            alphas.append(alpha)
        pv = _dot(p_ref[...], v_ref[pl.ds(off, kt), :])
        for h in range(A_HEADS):
            acc_ref[h] = alphas[h] * acc_ref[h] + pv[h * qb:(h + 1) * qb, :]
        return rank[:, kt - 1:kt]

    lax.fori_loop(0, n_tiles, attn_tile, jnp.zeros((qb, 1), F32))
    for h in range(A_HEADS):
        o_ref[:, h * 64:(h + 1) * 64] = (acc_ref[h] / l_ref[h])[:, 0:64]


def _dsa(aq, iq, aviw, kik_all, v_all, bsz, t_len, past, kt):
    qb = min(QBLOCK, t_len)
    nq = t_len // qb
    lp = kik_all.shape[1]
    topk = min(TOPK_MAX, (past + t_len) // 4)
    tri = jnp.triu(jnp.ones((kt, kt), F32)).astype(BF16)
    qrow = lambda w: pl.BlockSpec((qb, w), lambda b, i: (b * nq + i, 0))
    keys = pl.BlockSpec((None, lp, LANES), lambda b, i: (b, 0, 0))
    return pl.pallas_call(
        functools.partial(_dsa_kernel, past=past, qb=qb, kt=kt, topk=topk),
        grid=(bsz, nq),
        in_specs=[qrow(512), qrow(512), qrow(LANES), keys, keys, pl.BlockSpec((kt, kt), lambda b, i: (0, 0))],
        out_specs=qrow(512),
        out_shape=jax.ShapeDtypeStruct((bsz * t_len, 512), F32),
        scratch_shapes=[
            pltpu.VMEM((qb, lp), jnp.int32),
            pltpu.VMEM((A_HEADS * qb, 64), BF16),
            pltpu.VMEM((A_HEADS * qb, kt), BF16),
            pltpu.VMEM((A_HEADS, qb, 1), F32),
            pltpu.VMEM((A_HEADS, qb, 1), F32),
            pltpu.VMEM((A_HEADS, qb, LANES), F32),
        ],
        compiler_params=_params("parallel", "arbitrary"),
        name="dsa",
    )(aq, iq, aviw, kik_all, v_all, tri)


def _retention_kernel(bqk_ref, bv_ref, bg_ref, s0_ref, gn_ref, o_ref, s_out_ref, s_ref, *, n_chunks):
    t = pl.program_id(1)

    @pl.when(t == 0)
    def _():
        s_ref[...] = s0_ref[...]

    n = CHUNK
    ri = lax.broadcasted_iota(jnp.int32, (n, n), 0).astype(F32)
    ci = lax.broadcasted_iota(jnp.int32, (n, n), 1).astype(F32)
    diff = ri - ci
    pos = lax.broadcasted_iota(jnp.int32, (n, 1), 0).astype(F32)
    for h in range(B_HEADS):
        log_g = math.log(1.0 - 2.0 ** (-5.0 - h))
        intra = jnp.where(diff >= 0, jnp.exp(jnp.maximum(diff, 0.0) * log_g), 0.0)
        q_decay = jnp.exp((pos + 1.0) * log_g)
        k_decay = jnp.exp((n - 1.0 - pos) * log_g)
        chunk_decay = math.exp(n * log_g)
        s = s_ref[h]
        for c in range(n_chunks):
            rows = slice(c * n, (c + 1) * n)
            q = bqk_ref[rows, h * B_DK:(h + 1) * B_DK]
            k = bqk_ref[rows, 256 + h * B_DK:256 + (h + 1) * B_DK]
            v = bv_ref[rows, h * B_DV:(h + 1) * B_DV]
            scores = lax.dot_general(q, k, NT_DIMS, preferred_element_type=F32,
                                     precision=lax.Precision.HIGHEST) * intra
            o = _dot_hi(scores, v) + _dot_hi(q, s) * q_decay
            s = s * chunk_decay + lax.dot_general(k * k_decay, v, TN_DIMS, preferred_element_type=F32,
                                                  precision=lax.Precision.HIGHEST)
            mu = jnp.mean(o, axis=-1, keepdims=True)
            oc = o - mu
            var = jnp.mean(oc * oc, axis=-1, keepdims=True)
            g = bg_ref[rows, h * B_DV:(h + 1) * B_DV]
            o_ref[rows, h * B_DV:(h + 1) * B_DV] = (
                oc * lax.rsqrt(var + LN_EPS) * gn_ref[:, h * B_DV:(h + 1) * B_DV] * (g * _sigmoid(g)))
        s_ref[h] = s

    @pl.when(t == pl.num_programs(1) - 1)
    def _():
        s_out_ref[...] = s_ref[...]


def _retention(bqk, bv, bg, s0, b_gn, bsz, t_len, tt):
    nt = t_len // tt
    row = pl.BlockSpec((tt, 512), lambda b, t: (b * nt + t, 0))
    state = pl.BlockSpec((None, B_HEADS, B_DK, B_DV), lambda b, t: (b, 0, 0, 0))
    return pl.pallas_call(
        functools.partial(_retention_kernel, n_chunks=tt // CHUNK),
        grid=(bsz, nt),
        in_specs=[row, row, row, state, pl.BlockSpec((1, 512), lambda b, t: (0, 0))],
        out_specs=[row, state],
        out_shape=[jax.ShapeDtypeStruct((bsz * t_len, 512), F32),
                   jax.ShapeDtypeStruct((bsz, B_HEADS, B_DK, B_DV), F32)],
        scratch_shapes=[pltpu.VMEM((B_HEADS, B_DK, B_DV), F32)],
        compiler_params=_params("parallel", "arbitrary"),
        name="retention",
    )(bqk, bv, bg, s0, b_gn)


def _out_ln_kernel(oa_ref, ob_ref, x_ref, w_ref, g_ref, b_ref, y_ref):
    y = _dot(oa_ref[...].astype(BF16), w_ref[0:512, :]) + _dot(ob_ref[...].astype(BF16), w_ref[512:1024, :])
    y_ref[...] = _layernorm_rows(ALPHA * x_ref[...] + y, g_ref[...], b_ref[...])


def _out_ln(oa, ob, x2, w_out, g, b, tm):
    n = x2.shape[0]
    row = lambda w: pl.BlockSpec((tm, w), lambda i: (i, 0))
    vec = pl.BlockSpec((1, D_MODEL), lambda i: (0, 0))
    return pl.pallas_call(
        _out_ln_kernel,
        grid=(n // tm,),
        in_specs=[row(512), row(512), row(D_MODEL), pl.BlockSpec((D_MODEL, D_MODEL), lambda i: (0, 0)), vec, vec],
        out_specs=row(D_MODEL),
        out_shape=jax.ShapeDtypeStruct((n, D_MODEL), F32),
        compiler_params=_params("parallel"),
        name="out_ln",
    )(oa, ob, x2, w_out, g, b)


def _out_ln_d_kernel(oc_ref, od_ref, r_ref, kr_ref, v_ref, gate_ref, x_ref, w_ref, hm_ref,
                     lnx_g_ref, lnx_b_ref, rk_ref, g_ref, b_ref, y_ref):
    hm = hm_ref[...]
    o = od_ref[...]
    mu = _dot_exact_rhs(o, hm) * (1.0 / HEAD_DIM)
    oc = o - mu
    var = _dot_exact_rhs(oc * oc, hm) * (1.0 / HEAD_DIM)
    normed = oc * lax.rsqrt(var + D_GN_EPS) * lnx_g_ref[...] + lnx_b_ref[...]
    v = v_ref[...]
    bonus = _dot_exact_rhs(r_ref[...] * kr_ref[...] * rk_ref[...], hm) * v
    od = (normed + bonus) * gate_ref[...]
    y = _dot(oc_ref[...].astype(BF16), w_ref[0:512, :]) + _dot(od.astype(BF16), w_ref[512:1024, :])
    y_ref[...] = _layernorm_rows(ALPHA * x_ref[...] + y, g_ref[...], b_ref[...])


def _head_mask():
    head = jnp.arange(D_W) // HEAD_DIM
    return (head[:, None] == head[None, :]).astype(BF16)


def _out_ln_d(oc, od, r, kr, v, gate, x2, w_out, lnx_g, lnx_b, r_k, g, b, tm):
    n = x2.shape[0]
    row = lambda w: pl.BlockSpec((tm, w), lambda i: (i, 0))
    vec = lambda w: pl.BlockSpec((1, w), lambda i: (0, 0))
    return pl.pallas_call(
        _out_ln_d_kernel,
        grid=(n // tm,),
        in_specs=[row(512)] * 6 + [row(D_MODEL), pl.BlockSpec((D_MODEL, D_MODEL), lambda i: (0, 0)),
                                   pl.BlockSpec((D_W, D_W), lambda i: (0, 0)),
                                   vec(D_W), vec(D_W), vec(D_W), vec(D_MODEL), vec(D_MODEL)],
        out_specs=row(D_MODEL),
        out_shape=jax.ShapeDtypeStruct((n, D_MODEL), F32),
        compiler_params=_params("parallel"),
        name="out_ln_d",
    )(oc, od, r, kr, v, gate, x2, w_out, _head_mask(), lnx_g, lnx_b, r_k, g, b)


def _moe_kernel(x_ref, rw_ref, rb_ref, w13_ref, w2_ref, g_ref, b_ref, y_ref, acc_ref, gate_ref, xb_ref):
    e = pl.program_id(1)
    tm = x_ref.shape[0]
    lane = lax.broadcasted_iota(jnp.int32, (tm, N_EXPERTS), 1)

    lane_f = lane.astype(F32)

    def first_argmax(v):
        m = jnp.max(v, axis=1, keepdims=True)
        idx = jnp.min(jnp.where(v == m, lane_f, float(N_EXPERTS)), axis=1, keepdims=True)
        return m, idx.astype(jnp.int32)

    @pl.when(e == 0)
    def _():
        x = x_ref[...]
        xb_ref[...] = x.astype(BF16)
        aff = _sigmoid(_dot_hi(x, rw_ref[...]))
        biased = aff + rb_ref[...]
        best = jnp.zeros((tm, 1), jnp.int32)
        best_score = jnp.full((tm, 1), -jnp.inf, F32)
        for grp in range(N_GROUPS):
            vg = jnp.where(lane // EXPERTS_PER_GROUP == grp, biased, -jnp.inf)
            top1, idx1 = first_argmax(vg)
            top2, _ = first_argmax(jnp.where(lane == idx1, -jnp.inf, vg))
            score = top1 + top2
            better = score > best_score
            best = jnp.where(better, grp, best)
            best_score = jnp.where(better, score, best_score)
        masked = jnp.where(lane // EXPERTS_PER_GROUP == best, biased, -jnp.inf)
        _, idx1 = first_argmax(masked)
        _, idx2 = first_argmax(jnp.where(lane == idx1, -jnp.inf, masked))
        top_aff = jnp.where((lane == idx1) | (lane == idx2), aff, 0.0)
        gate_ref[...] = top_aff / jnp.sum(top_aff, axis=1, keepdims=True)
        acc_ref[...] = jnp.zeros(acc_ref.shape, F32)

    h13 = _dot(xb_ref[...], w13_ref[...])
    h1 = h13[:, 0:D_EXPERT]
    h = (h1 * _sigmoid(h1)) * h13[:, D_EXPERT:2 * D_EXPERT]
    gate_e = jnp.sum(jnp.where(lane == e, gate_ref[...], 0.0), axis=1, keepdims=True)
    acc_ref[...] += gate_e * _dot(h.astype(BF16), w2_ref[...])

    @pl.when(e == N_EXPERTS - 1)
    def _():
        y_ref[...] = _layernorm_rows(ALPHA * x_ref[...] + acc_ref[...], g_ref[...], b_ref[...])


def _moe_ln(x2, router_w, router_b, w13, w2, g, b, tm):
    n = x2.shape[0]
    row = pl.BlockSpec((tm, D_MODEL), lambda i, e: (i, 0))
    vec = pl.BlockSpec((1, D_MODEL), lambda i, e: (0, 0))
    return pl.pallas_call(
        _moe_kernel,
        grid=(n // tm, N_EXPERTS),
        in_specs=[row,
                  pl.BlockSpec((D_MODEL, N_EXPERTS), lambda i, e: (0, 0)),
                  pl.BlockSpec((1, N_EXPERTS), lambda i, e: (0, 0)),
                  pl.BlockSpec((None, D_MODEL, 2 * D_EXPERT), lambda i, e: (e, 0, 0)),
                  pl.BlockSpec((None, D_EXPERT, D_MODEL), lambda i, e: (e, 0, 0)),
                  vec, vec],
        out_specs=row,
        out_shape=jax.ShapeDtypeStruct((n, D_MODEL), F32),
        scratch_shapes=[pltpu.VMEM((tm, D_MODEL), F32), pltpu.VMEM((tm, N_EXPERTS), F32),
                        pltpu.VMEM((tm, D_MODEL), BF16)],
        compiler_params=_params("parallel", "arbitrary"),
        name="moe_ln",
    )(x2, router_w, router_b, w13, w2, g, b)


def _proj_cd_kernel(x_ref, w_ref, cq_ref, ck_ref, cv_ref, pd_ref):
    xb = x_ref[...].astype(BF16)
    cq_ref[...] = _dot(xb, w_ref[:, 0:512])
    ck_ref[...] = _dot(xb, w_ref[:, 512:1024])
    cv_ref[...] = _dot(xb, w_ref[:, 1024:1536])
    pd_ref[...] = _dot(xb, w_ref[:, 1536:1536 + D_SHIFT_W])


def _proj_cd(x2, w_bf16, tm):
    n = x2.shape[0]
    row = lambda w: pl.BlockSpec((tm, w), lambda i: (i, 0))
    outs = [512, 512, 512, D_SHIFT_W]
    return pl.pallas_call(
        _proj_cd_kernel,
        grid=(n // tm,),
        in_specs=[row(D_MODEL), pl.BlockSpec((D_MODEL, 1536 + D_SHIFT_W), lambda i: (0, 0))],
        out_specs=[row(w) for w in outs],
        out_shape=[jax.ShapeDtypeStruct((n, w), F32) for w in outs],
        compiler_params=_params("parallel"),
        name="proj_cd",
    )(x2, w_bf16)


def _stick_kernel(q_ref, k_ref, v_ref, ust_ref, o_ref, *, past, tq, kb):
    i = pl.program_id(2)
    q0 = past + i * tq
    n_kb = (q0 + tq + kb - 1) // kb
    qpos = q0 + lax.broadcasted_iota(jnp.int32, (tq, 1), 0)
    lane_kb = lax.broadcasted_iota(jnp.int32, (1, kb), 1)
    for hh in range(LANES // HEAD_DIM):
        cols = slice(hh * HEAD_DIM, (hh + 1) * HEAD_DIM)
        q = (q_ref[:, cols] * (HEAD_DIM ** -0.5)).astype(BF16)

        def cond(c):
            j, carry, _ = c
            return jnp.logical_and(j >= 0, jnp.max(carry) > EXP_ZERO_BELOW)

        def body(c):
            j, carry, o = c
            off = pl.multiple_of(j * kb, kb)
            k = k_ref[pl.ds(off, kb), cols]
            v = v_ref[pl.ds(off, kb), cols]
            z = lax.dot_general(q, k, NT_DIMS, preferred_element_type=F32)
            strict = off + lane_kb < qpos
            sp = _softplus(z)
            log_keep = jnp.where(strict, -sp, 0.0)
            later = carry + _dot_exact_rhs(log_keep, ust_ref[...])
            a = jnp.where(strict, jnp.exp((z - sp) + later), 0.0)
            o = o + _dot(a.astype(BF16), v)
            return j - 1, carry + jnp.sum(log_keep, axis=1, keepdims=True), o

        _, _, o = lax.while_loop(cond, body, (n_kb - 1, jnp.zeros((tq, 1), F32), jnp.zeros((tq, HEAD_DIM), F32)))
        o_ref[:, cols] = o


def _stick(cq, k_all, v_all, bsz, t_len, past, kb):
    tq = min(QBLOCK, t_len)
    nq = t_len // tq
    lp = k_all.shape[1]
    ust = jnp.tril(jnp.ones((kb, kb), F32), -1).astype(BF16)
    qrow = pl.BlockSpec((tq, LANES), lambda b, hp, i: (b * nq + i, hp))
    keys = pl.BlockSpec((None, lp, LANES), lambda b, hp, i: (b, 0, hp))
    return pl.pallas_call(
        functools.partial(_stick_kernel, past=past, tq=tq, kb=kb),
        grid=(bsz, C_HEADS * HEAD_DIM // LANES, nq),
        in_specs=[qrow, keys, keys, pl.BlockSpec((kb, kb), lambda b, hp, i: (0, 0))],
        out_specs=qrow,
        out_shape=jax.ShapeDtypeStruct((bsz * t_len, 512), F32),
        compiler_params=_params("parallel", "parallel", "arbitrary"),
        name="stick",
    )(cq, k_all, v_all, ust)


def _rwkv_pre_kernel(pd_ref, shift_ref, mu_ref, w0a0_ref, wa2_ref, gw2_ref, kk_ref_, ka_ref, hm_ref,
                     r_out, w_out, kr_out, v_out, kk_out, b_out, g_out, last_ref):
    t = pl.program_id(1)

    @pl.when(t == 0)
    def _():
        last_ref[...] = shift_ref[...]

    pd = pd_ref[...]
    tm = pd.shape[0]
    rolled = pltpu.roll(pd, 1, 0)
    first_row = lax.broadcasted_iota(jnp.int32, (tm, 1), 0) == 0
    prev = jnp.where(first_row, last_ref[...], rolled)
    last_ref[...] = pd[tm - 1:tm, :]
    pm = pd + (prev - pd) * mu_ref[...]
    r = pm[:, 0:512]
    k = pm[:, 512:1024]
    v = pm[:, 1024:1536]
    lwa = pm[:, 1536:1664]
    lg = pm[:, 1664:1792]
    lane = lax.broadcasted_iota(jnp.int32, lwa.shape, 1)
    lwa = jnp.where(lane < D_LORA_W, jnp.tanh(lwa), lwa)
    pre = w0a0_ref[...] + _dot_hi(lwa, wa2_ref[...])
    w_log = -_softplus(-pre[:, 0:512]) - 0.5
    decay = jnp.exp(-jnp.exp(w_log))
    a = _sigmoid(pre[:, 512:1024])
    g = _dot_hi(_sigmoid(lg), gw2_ref[...])
    kk = k * kk_ref_[...]
    kk = kk * lax.rsqrt(_dot_exact_rhs(kk * kk, hm_ref[...]) + 1e-12)
    r_out[...] = r
    w_out[...] = decay
    kr_out[...] = k * (1.0 + (a - 1.0) * ka_ref[...])
    v_out[...] = v
    kk_out[...] = kk
    b_out[...] = kk * a
    g_out[...] = g


def _rwkv_pre(pd, shift, mu, w0a0, wa2, gw2, k_k, k_a, bsz, t_len, tm):
    nt = t_len // tm
    row = lambda w: pl.BlockSpec((tm, w), lambda b, t: (b * nt + t, 0))
    const = lambda s: pl.BlockSpec(s, lambda b, t: (0,) * len(s))
    return pl.pallas_call(
        _rwkv_pre_kernel,
        grid=(bsz, nt),
        in_specs=[row(D_SHIFT_W), pl.BlockSpec((None, 1, D_SHIFT_W), lambda b, t: (b, 0, 0)),
                  const((1, D_SHIFT_W)), const((1, 2 * D_W)), const((LANES, 2 * D_W)), const((D_LORA_G, D_W)),
                  const((1, D_W)), const((1, D_W)), const((D_W, D_W))],
        out_specs=[row(D_W)] * 7,
        out_shape=[jax.ShapeDtypeStruct((bsz * t_len, D_W), F32)] * 7,
        scratch_shapes=[pltpu.VMEM((1, D_SHIFT_W), F32)],
        compiler_params=_params("parallel", "arbitrary"),
        name="rwkv_pre",
    )(pd, shift, mu, w0a0, wa2, gw2, k_k, k_a, _head_mask())


RWKV_VL = HEAD_DIM // 2


def _rwkv_scan_kernel(kk_ref, w_ref, b_ref, kr_ref, r_ref, v_ref, s0_ref, o_ref, s_out_ref, s_ref, *, tb):
    g = pl.program_id(0)

    @pl.when(g == 0)
    def _():
        s_ref[...] = s0_ref[...]

    def step(t, carry):
        kk = kk_ref[t]
        w = w_ref[t]
        b = b_ref[t]
        kr = kr_ref[t]
        r = r_ref[t]
        for vl in range(RWKV_VL):
            s = s_ref[vl]
            sa = jnp.sum(s * kk, axis=0, keepdims=True)
            s = s * w - sa * b + v_ref[t, vl:vl + 1, :] * kr
            s_ref[vl] = s
            o_ref[t, vl:vl + 1, :] = jnp.sum(s * r, axis=0, keepdims=True)
        return carry

    lax.fori_loop(0, tb, step, 0)

    @pl.when(g == pl.num_programs(0) - 1)
    def _():
        s_out_ref[...] = s_ref[...]


def _rwkv_scan(kk, w, b, kr, r, v, s0, t_len, tb):
    big = pl.BlockSpec((tb, HEAD_DIM, LANES), lambda g: (g, 0, 0))
    small = pl.BlockSpec((tb, RWKV_VL, LANES), lambda g: (g, 0, 0))
    state = pl.BlockSpec((RWKV_VL, HEAD_DIM, LANES), lambda g: (0, 0, 0))
    return pl.pallas_call(
        functools.partial(_rwkv_scan_kernel, tb=tb),
        grid=(t_len // tb,),
        in_specs=[big] * 5 + [small, state],
        out_specs=[small, state],
        out_shape=[jax.ShapeDtypeStruct((t_len, RWKV_VL, LANES), F32),
                   jax.ShapeDtypeStruct((RWKV_VL, HEAD_DIM, LANES), F32)],
        scratch_shapes=[pltpu.VMEM((RWKV_VL, HEAD_DIM, LANES), F32)],
        compiler_params=_params("arbitrary"),
        name="rwkv_scan",
    )(kk, w, b, kr, r, v, s0)


def _to_scan_keyed(x2, bsz, t_len):
    y = x2.reshape(bsz, t_len, D_HEADS, HEAD_DIM).transpose(1, 3, 0, 2).reshape(t_len, HEAD_DIM, bsz * D_HEADS)
    return jnp.concatenate([y, y], axis=-1)


def _to_scan_valued(x2, bsz, t_len):
    y = x2.reshape(bsz, t_len, D_HEADS, 2, RWKV_VL).transpose(1, 4, 3, 0, 2)
    return y.reshape(t_len, RWKV_VL, 2 * bsz * D_HEADS)


def _from_scan_valued(y, bsz, t_len):
    return y.reshape(t_len, RWKV_VL, 2, bsz, D_HEADS).transpose(3, 0, 4, 2, 1).reshape(bsz * t_len, D_W)


def _state_to_scan(s, bsz):
    y = s.reshape(bsz, D_HEADS, 2, RWKV_VL, HEAD_DIM).transpose(3, 4, 2, 0, 1)
    return y.reshape(RWKV_VL, HEAD_DIM, 2 * bsz * D_HEADS)


def _state_from_scan(y, bsz):
    return y.reshape(RWKV_VL, HEAD_DIM, 2, bsz, D_HEADS).transpose(3, 4, 2, 0, 1).reshape(
        bsz, D_HEADS, HEAD_DIM, HEAD_DIM)


def _pad_keys(x, mult):
    pad = (-x.shape[1]) % mult
    return x if pad == 0 else jnp.pad(x, ((0, 0), (0, pad), (0, 0)))


def _tile(n, cap):
    return min(n, cap)


def _run_group(x, past, p):
    bsz, t_len, _ = x.shape
    assert bsz * D_HEADS * 2 == LANES, "rwkv scan packs (value half, batch, head) into the lane axis"
    n = bsz * t_len
    x2 = x.reshape(n, D_MODEL)
    past_len = 0 if past is None else past[0].shape[2]
    tm = _tile(t_len, 512)

    cos, sin = _rope_tables(past_len, t_len)
    aq, iq, bqk, kik, aviw, bv, bg = _proj_ab(x2, p['w_in_ab'], cos, sin, t_len, tm)
    ak = kik[:, 0:64].reshape(1, bsz, t_len, 64)
    ik = kik[:, 64:128].reshape(1, bsz, t_len, 64)
    av = aviw[:, 0:64].reshape(1, bsz, t_len, 64)
    kik_all = kik.reshape(bsz, t_len, LANES).astype(BF16)
    v_all = aviw.reshape(bsz, t_len, LANES).astype(BF16)
    if past is None:
        s_b = jnp.zeros((bsz, B_HEADS, B_DK, B_DV), F32)
    else:
        pk, pv, pik, sb = past[0][0], past[1][0], past[2][0], past[3][0]
        kik_all = jnp.concatenate([jnp.concatenate([pk, pik], axis=-1).astype(BF16), kik_all], axis=1)
        v_all = jnp.concatenate([jnp.concatenate([pv, jnp.zeros_like(pv)], axis=-1).astype(BF16), v_all], axis=1)
        s_b = sb
    kt = 512
    o_a = _dsa(aq, iq, aviw, _pad_keys(kik_all, kt), _pad_keys(v_all, kt), bsz, t_len, past_len, kt)
    o_b, s_b_new = _retention(bqk, bv, bg, s_b, p['b_gn'], bsz, t_len, tm)
    x2 = _out_ln(o_a, o_b, x2, p['w_out_ab'], p['ln_g'][0, 0][None], p['ln_b'][0, 0][None], tm)
    tm_moe = _tile(n, 1024)
    x2 = _moe_ln(x2, p['router_w'], p['router_b'], p['w13'][0], p['w2'][0],
                 p['ln_g'][0, 1][None], p['ln_b'][0, 1][None], tm_moe)

    cq, ck, cv, pd = _proj_cd(x2, p['w_in_cd'], tm)
    ck_all = ck.reshape(bsz, t_len, 512).astype(BF16)
    cv_all = cv.reshape(bsz, t_len, 512).astype(BF16)
    if past is None:
        s_d = jnp.zeros((bsz, D_HEADS, HEAD_DIM, HEAD_DIM), F32)
        shift = jnp.zeros((bsz, 1, D_SHIFT_W), F32)
    else:
        ck_all = jnp.concatenate([past[4][0].reshape(bsz, past_len, 512).astype(BF16), ck_all], axis=1)
        cv_all = jnp.concatenate([past[5][0].reshape(bsz, past_len, 512).astype(BF16), cv_all], axis=1)
        s_d, shift = past[6][0], past[7][0]
    kb = 128
    o_c = _stick(cq, _pad_keys(ck_all, kb), _pad_keys(cv_all, kb), bsz, t_len, past_len, kb)
    r, w, kr, v, kk, b, g = _rwkv_pre(pd, shift, p['d_mu'], p['d_w0a0'], p['d_wa2'], p['d_g2'],
                                      p['d_k_k'], p['d_k_a'], bsz, t_len, tm)
    keyed = [_to_scan_keyed(u, bsz, t_len) for u in (kk, w, b, kr, r)]
    o_scan, s_scan = _rwkv_scan(*keyed, _to_scan_valued(v, bsz, t_len), _state_to_scan(s_d, bsz),
                                t_len, _tile(t_len, 64))
    o_d = _from_scan_valued(o_scan, bsz, t_len)
    s_d_new = _state_from_scan(s_scan, bsz)
    x2 = _out_ln_d(o_c, o_d, r, kr, v, g, x2, p['w_out_cd'], p['d_lnx_g'], p['d_lnx_b'], p['d_r_k'],
                   p['ln_g'][1, 0][None], p['ln_b'][1, 0][None], tm)
    x2 = _moe_ln(x2, p['router_w'], p['router_b'], p['w13'][1], p['w2'][1],
                 p['ln_g'][1, 1][None], p['ln_b'][1, 1][None], tm_moe)

    states = (ak, av, ik, s_b_new[None],
              ck.reshape(1, bsz, t_len, C_HEADS, HEAD_DIM), cv.reshape(1, bsz, t_len, C_HEADS, HEAD_DIM),
              s_d_new[None], pd.reshape(bsz, t_len, D_SHIFT_W)[:, -1:][None])
    return x2.reshape(bsz, t_len, D_MODEL), states


def kernel(x_prompt, x_sample, cache_a_k, cache_a_v, cache_a_idx_k, state_b, cache_c_k, cache_c_v, state_d, state_d_shift, w_in_ab, w_out_ab, b_gn, w_in_cd, w_out_cd, d_mu, d_w0, d_w2, d_a0, d_a2, d_g2, d_k_k, d_k_a, d_r_k, d_lnx_g, d_lnx_b, ln_g, ln_b, router_w, router_b, moe_w1, moe_w3, moe_w2):
    zeros_w = jnp.zeros((D_LORA_W, D_W), F32)
    p = {
        'w_in_ab': _pack_w_in_ab(w_in_ab[0]),
        'w_out_ab': w_out_ab[0].astype(BF16),
        'b_gn': b_gn,
        'w_in_cd': w_in_cd[0].astype(BF16),
        'w_out_cd': w_out_cd[0].astype(BF16),
        'd_mu': d_mu,
        'd_w0a0': jnp.concatenate([d_w0, d_a0], axis=1),
        'd_wa2': jnp.concatenate([jnp.concatenate([d_w2[0], zeros_w], axis=1),
                                  jnp.concatenate([zeros_w, d_a2[0]], axis=1)], axis=0),
        'd_g2': d_g2[0],
        'd_k_k': d_k_k, 'd_k_a': d_k_a, 'd_r_k': d_r_k, 'd_lnx_g': d_lnx_g, 'd_lnx_b': d_lnx_b,
        'ln_g': ln_g, 'ln_b': ln_b,
        'router_w': router_w, 'router_b': router_b[None],
        'w13': jnp.concatenate([moe_w1, moe_w3], axis=-1).astype(BF16),
        'w2': moe_w2.astype(BF16),
    }
    y_p, sp = _run_group(x_prompt, None, p)
    past = (cache_a_k, cache_a_v, cache_a_idx_k, state_b, cache_c_k, cache_c_v, state_d, state_d_shift)
    y_s, ss = _run_group(x_sample, past, p)
    return (y_p, y_s, sp[0], sp[1], sp[2], ss[0], ss[1], ss[2], sp[3], ss[3], sp[4], sp[5], ss[4], ss[5],
            sp[6], ss[6], sp[7], ss[7])
```

```python
import functools
import math

import jax
import jax.numpy as jnp
import numpy as np
from jax import lax
from jax.experimental import pallas as pl
from jax.experimental.pallas import tpu as pltpu

F32 = jnp.float32
BF16 = jnp.bfloat16

D_MODEL = 1024
CHUNK = 64
QBLOCK = 128
ROPE_THETA = 10000.0
HEAD_DIM = 64
LN_EPS = 1e-5
A_HEADS = 8
IDX_HEADS = 8
TOPK_MAX = 256
B_HEADS = 4
B_DK = 64
B_DV = 128
C_HEADS = 8
D_HEADS = 8
D_LORA_W = 64
D_LORA_A = 64
D_LORA_G = 128
D_GN_EPS = 64e-5
N_EXPERTS = 16
N_GROUPS = 4
EXPERTS_PER_GROUP = 4
D_EXPERT = 256
DEPTH = 2
ALPHA = (2 * DEPTH) ** 0.25
D_W = D_HEADS * HEAD_DIM
D_SHIFT_W = 3 * D_W + D_LORA_W + D_LORA_A + D_LORA_G

LANES = 128
SUBLANES = 8
VMEM_LIMIT_BYTES = 56 * 1024 * 1024

INT_MIN = -(2 ** 31)
EXP_ZERO_BELOW = -104.0

NT_DIMS = (((1,), (1,)), ((), ()))
TN_DIMS = (((0,), (0,)), ((), ()))


def _params(*sem):
    return pltpu.CompilerParams(dimension_semantics=sem, vmem_limit_bytes=VMEM_LIMIT_BYTES)


def _dot(a, b):
    return jnp.dot(a, b, preferred_element_type=F32)


def _dot_hi(a, b):
    return jnp.dot(a, b, preferred_element_type=F32, precision=lax.Precision.HIGHEST)


def _split3(x):
    h1 = x.astype(BF16)
    r1 = x - h1.astype(F32)
    h2 = r1.astype(BF16)
    r2 = r1 - h2.astype(F32)
    return h1, h2, r2.astype(BF16)


def _dot_exact_rhs(x, m01):
    h1, h2, h3 = _split3(x)
    return _dot(h1, m01) + _dot(h2, m01) + _dot(h3, m01)


def _layernorm_rows(x, g, b):
    mu = jnp.mean(x, axis=-1, keepdims=True)
    xc = x - mu
    var = jnp.mean(xc * xc, axis=-1, keepdims=True)
    return xc * lax.rsqrt(var + LN_EPS) * g + b


def _sigmoid(x):
    return 1.0 / (1.0 + jnp.exp(-x))


def _softplus(x):
    return jnp.maximum(x, 0.0) + jnp.log1p(jnp.exp(-jnp.abs(x)))


def _rope_slab(x, cos, sin_signed):
    lane = lax.broadcasted_iota(jnp.int32, x.shape, 1)
    first_half = (lane % HEAD_DIM) < (HEAD_DIM // 2)
    swapped = jnp.where(first_half, pltpu.roll(x, LANES - HEAD_DIM // 2, 1), pltpu.roll(x, HEAD_DIM // 2, 1))
    return x * cos + swapped * sin_signed


AB_ROPED = 1664
AB_PACKED = 2816


def _pack_w_in_ab(w):
    aq, ak, av, iq, ik, iw, bq, bk, bv, bg = jnp.split(
        w, [512, 576, 640, 1152, 1216, 1224, 1480, 1736, 2248], axis=1)
    pad = jnp.zeros((w.shape[0], LANES - 64 - IDX_HEADS), w.dtype)
    return jnp.concatenate([aq, iq, bq, bk, ak, ik, av, iw, pad, bv, bg], axis=1).astype(BF16)


def _proj_ab_kernel(x_ref, w_ref, cos_ref, sin_ref, aq_ref, iq_ref, bqk_ref, kik_ref, aviw_ref, bv_ref, bg_ref):
    xb = x_ref[...].astype(BF16)
    cos = cos_ref[...]
    sin = sin_ref[...]

    def roped(col0, width, scale_from=None):
        y = _dot(xb, w_ref[:, col0:col0 + width])
        parts = []
        for c in range(width // LANES):
            slab = _rope_slab(y[:, c * LANES:(c + 1) * LANES], cos, sin)
            if scale_from is not None and c * LANES >= scale_from:
                slab = slab * (B_DK ** -0.5)
            parts.append(slab)
        return parts

    for c, slab in enumerate(roped(0, 512)):
        aq_ref[:, c * LANES:(c + 1) * LANES] = slab
    for c, slab in enumerate(roped(512, 512)):
        iq_ref[:, c * LANES:(c + 1) * LANES] = slab
    for c, slab in enumerate(roped(1024, 512, scale_from=256)):
        bqk_ref[:, c * LANES:(c + 1) * LANES] = slab
    kik_ref[...] = roped(1536, LANES)[0]
    aviw = _dot(xb, w_ref[:, AB_ROPED:AB_ROPED + LANES])
    lane = lax.broadcasted_iota(jnp.int32, aviw.shape, 1)
    is_iw = (lane >= 64) & (lane < 64 + IDX_HEADS)
    aviw_ref[...] = jnp.where(is_iw, aviw * ((IDX_HEADS * HEAD_DIM) ** -0.5), aviw)
    bv_ref[...] = _dot(xb, w_ref[:, 1792:2304])
    bg_ref[...] = _dot(xb, w_ref[:, 2304:2816])


def _proj_ab(x2, w_packed, cos, sin, t_len, tm):
    n = x2.shape[0]
    nt = t_len // tm
    row = lambda w: pl.BlockSpec((tm, w), lambda i: (i, 0))
    tab = pl.BlockSpec((tm, LANES), lambda i: (i % nt, 0))
    outs = [512, 512, 512, LANES, LANES, 512, 512]
    return pl.pallas_call(
        _proj_ab_kernel,
        grid=(n // tm,),
        in_specs=[row(D_MODEL), pl.BlockSpec((D_MODEL, AB_PACKED), lambda i: (0, 0)), tab, tab],
        out_specs=[row(w) for w in outs],
        out_shape=[jax.ShapeDtypeStruct((n, w), F32) for w in outs],
        compiler_params=_params("parallel"),
        name="proj_ab",
    )(x2, w_packed, cos, sin)


def _rope_tables(past, t_len):
    half = HEAD_DIM // 2
    inv = ROPE_THETA ** (-jnp.arange(half, dtype=F32) / half)
    ang = (past + jnp.arange(t_len)).astype(F32)[:, None] * inv[None, :]
    c, s = jnp.cos(ang), jnp.sin(ang)
    return jnp.concatenate([c, c, c, c], axis=1), jnp.concatenate([-s, s, -s, s], axis=1)


def _dsa_kernel(aq_ref, iq_ref, aviw_ref, kik_ref, v_ref, tri_ref, o_ref,
                skey_ref, qs_ref, p_ref, m_ref, mlane_ref, acc_ref, *, past, qb, kt, topk):
    i = pl.program_id(1)
    q0 = past + i * qb
    n_tiles = (q0 + qb + kt - 1) // kt
    row = lax.broadcasted_iota(jnp.int32, (qb, 1), 0)
    vis_end = ((q0 + row) // CHUNK + 1) * CHUNK
    lane_kt = lax.broadcasted_iota(jnp.int32, (1, kt), 1)

    iq = iq_ref[...].astype(BF16)
    iw = aviw_ref[:, 64:64 + IDX_HEADS]

    def score_tile(j, carry):
        off = pl.multiple_of(j * kt, kt)
        ik = kik_ref[pl.ds(off, kt), 64:128]
        acc = jnp.zeros((qb, kt), F32)
        for h in range(IDX_HEADS):
            s = lax.dot_general(iq[:, h * 64:(h + 1) * 64], ik, NT_DIMS, preferred_element_type=F32)
            acc = acc + iw[:, h:h + 1] * jnp.maximum(s, 0.0)
        acc = acc + 0.0
        acc = jnp.where(off + lane_kt < vis_end, acc, -jnp.inf)
        bits = pltpu.bitcast(acc, jnp.int32)
        skey_ref[:, pl.ds(off, kt)] = jnp.where(bits < 0, bits ^ 0x7FFFFFFF, bits)
        return carry

    lax.fori_loop(0, n_tiles, score_tile, 0)

    def count(pred_fn):
        def body(j, cnt):
            off = pl.multiple_of(j * kt, kt)
            hit = pred_fn(skey_ref[:, pl.ds(off, kt)])
            for c in range(kt // LANES):
                cnt = cnt + jnp.where(hit[:, c * LANES:(c + 1) * LANES], 1.0, 0.0)
            return cnt
        cnt = lax.fori_loop(0, n_tiles, body, jnp.zeros((qb, LANES), F32))
        return jnp.sum(cnt, axis=1, keepdims=True)

    zero = jnp.zeros((qb, 1), jnp.int32)
    t0 = jnp.where(count(lambda x: x >= zero) >= topk, zero, jnp.full((qb, 1), INT_MIN, jnp.int32))

    def bit_step(it, t):
        cand = t | jnp.left_shift(jnp.int32(1), 30 - it)
        return jnp.where(count(lambda x: x >= cand) >= topk, cand, t)

    thr = lax.fori_loop(0, 31, bit_step, t0)
    need = topk - count(lambda x: x > thr)

    for h in range(A_HEADS):
        qs_ref[h * qb:(h + 1) * qb, :] = (aq_ref[:, h * 64:(h + 1) * 64] * (HEAD_DIM ** -0.5)).astype(BF16)
    mlane_ref[...] = jnp.full(mlane_ref.shape, -jnp.inf, F32)
    acc_ref[...] = jnp.zeros(acc_ref.shape, F32)

    def max_tile(j, eq_seen):
        off = pl.multiple_of(j * kt, kt)
        key = skey_ref[:, pl.ds(off, kt)]
        eq = key == thr
        rank = _dot(jnp.where(eq, 1.0, 0.0).astype(BF16), tri_ref[...]) + eq_seen
        sel = ((key > thr) | (eq & (rank <= need))) & (off + lane_kt < vis_end)
        bias = jnp.where(sel, 0.0, -jnp.inf)
        skey_ref[:, pl.ds(off, kt)] = pltpu.bitcast(bias, jnp.int32)
        k = kik_ref[pl.ds(off, kt), 0:64]
        s = lax.dot_general(qs_ref[...], k, NT_DIMS, preferred_element_type=F32)
        for h in range(A_HEADS):
            sh = s[h * qb:(h + 1) * qb, :] + bias
            mm = sh[:, 0:LANES]
            for c in range(1, kt // LANES):
                mm = jnp.maximum(mm, sh[:, c * LANES:(c + 1) * LANES])
            mlane_ref[h] = jnp.maximum(mlane_ref[h], mm)
        return rank[:, kt - 1:kt]

    lax.fori_loop(0, n_tiles, max_tile, jnp.zeros((qb, 1), F32))
    for h in range(A_HEADS):
        m_ref[h * qb:(h + 1) * qb, :] = jnp.max(mlane_ref[h], axis=1, keepdims=True)

    def pv_tile(j, carry):
        off = pl.multiple_of(j * kt, kt)
        bias = pltpu.bitcast(skey_ref[:, pl.ds(off, kt)], F32)
        k = kik_ref[pl.ds(off, kt), 0:64]
        s = lax.dot_general(qs_ref[...], k, NT_DIMS, preferred_element_type=F32)
        for h in range(A_HEADS):
            rows = slice(h * qb, (h + 1) * qb)
            p_ref[rows, :] = jnp.exp((s[rows, :] + bias) - m_ref[rows, :]).astype(BF16)
        acc_ref[...] += _dot(p_ref[...], v_ref[pl.ds(off, kt), :])
        return carry

    lax.fori_loop(0, n_tiles, pv_tile, 0)
    for h in range(A_HEADS):
        a = acc_ref[h * qb:(h + 1) * qb, :]
        o_ref[:, h * 64:(h + 1) * 64] = (a / a[:, 64:65])[:, 0:64]


def _dsa(aq, iq, aviw, kik_all, v_all, bsz, t_len, past, kt):
    qb = min(QBLOCK, t_len)
    nq = t_len // qb
    lp = kik_all.shape[1]
    topk = min(TOPK_MAX, (past + t_len) // 4)
    tri = jnp.triu(jnp.ones((kt, kt), F32)).astype(BF16)
    qrow = lambda w: pl.BlockSpec((qb, w), lambda b, i: (b * nq + i, 0))
    keys = pl.BlockSpec((None, lp, LANES), lambda b, i: (b, 0, 0))
    return pl.pallas_call(
        functools.partial(_dsa_kernel, past=past, qb=qb, kt=kt, topk=topk),
        grid=(bsz, nq),
        in_specs=[qrow(512), qrow(512), qrow(LANES), keys, keys, pl.BlockSpec((kt, kt), lambda b, i: (0, 0))],
        out_specs=qrow(512),
        out_shape=jax.ShapeDtypeStruct((bsz * t_len, 512), F32),
        scratch_shapes=[
            pltpu.VMEM((qb, lp), jnp.int32),
            pltpu.VMEM((A_HEADS * qb, 64), BF16),
            pltpu.VMEM((A_HEADS * qb, kt), BF16),
            pltpu.VMEM((A_HEADS * qb, 1), F32),
            pltpu.VMEM((A_HEADS, qb, LANES), F32),
            pltpu.VMEM((A_HEADS * qb, LANES), F32),
        ],
        compiler_params=_params("parallel", "arbitrary"),
        name="dsa",
    )(aq, iq, aviw, kik_all, v_all, tri)


def _retention_kernel(bqk_ref, bv_ref, bg_ref, s0_ref, gn_ref, o_ref, s_out_ref, s_ref, *, n_chunks):
    t = pl.program_id(1)

    @pl.when(t == 0)
    def _():
        s_ref[...] = s0_ref[...]

    n = CHUNK
    ri = lax.broadcasted_iota(jnp.int32, (n, n), 0).astype(F32)
    ci = lax.broadcasted_iota(jnp.int32, (n, n), 1).astype(F32)
    diff = ri - ci
    pos = lax.broadcasted_iota(jnp.int32, (n, 1), 0).astype(F32)
    for h in range(B_HEADS):
        log_g = math.log(1.0 - 2.0 ** (-5.0 - h))
        intra = jnp.where(diff >= 0, jnp.exp(jnp.maximum(diff, 0.0) * log_g), 0.0)
        q_decay = jnp.exp((pos + 1.0) * log_g)
        k_decay = jnp.exp((n - 1.0 - pos) * log_g)
        chunk_decay = math.exp(n * log_g)
        s = s_ref[h]
        for c in range(n_chunks):
            rows = slice(c * n, (c + 1) * n)
            q = bqk_ref[rows, h * B_DK:(h + 1) * B_DK]
            k = bqk_ref[rows, 256 + h * B_DK:256 + (h + 1) * B_DK]
            v = bv_ref[rows, h * B_DV:(h + 1) * B_DV]
            scores = lax.dot_general(q, k, NT_DIMS, preferred_element_type=F32,
                                     precision=lax.Precision.HIGHEST) * intra
            o = _dot_hi(scores, v) + _dot_hi(q, s) * q_decay
            s = s * chunk_decay + lax.dot_general(k * k_decay, v, TN_DIMS, preferred_element_type=F32,
                                                  precision=lax.Precision.HIGHEST)
            mu = jnp.mean(o, axis=-1, keepdims=True)
            oc = o - mu
            var = jnp.mean(oc * oc, axis=-1, keepdims=True)
            g = bg_ref[rows, h * B_DV:(h + 1) * B_DV]
            o_ref[rows, h * B_DV:(h + 1) * B_DV] = (
                oc * lax.rsqrt(var + LN_EPS) * gn_ref[:, h * B_DV:(h + 1) * B_DV] * (g * _sigmoid(g)))
        s_ref[h] = s

    @pl.when(t == pl.num_programs(1) - 1)
    def _():
        s_out_ref[...] = s_ref[...]


def _retention(bqk, bv, bg, s0, b_gn, bsz, t_len, tt):
    nt = t_len // tt
    row = pl.BlockSpec((tt, 512), lambda b, t: (b * nt + t, 0))
    state = pl.BlockSpec((None, B_HEADS, B_DK, B_DV), lambda b, t: (b, 0, 0, 0))
    return pl.pallas_call(
        functools.partial(_retention_kernel, n_chunks=tt // CHUNK),
        grid=(bsz, nt),
        in_specs=[row, row, row, state, pl.BlockSpec((1, 512), lambda b, t: (0, 0))],
        out_specs=[row, state],
        out_shape=[jax.ShapeDtypeStruct((bsz * t_len, 512), F32),
                   jax.ShapeDtypeStruct((bsz, B_HEADS, B_DK, B_DV), F32)],
        scratch_shapes=[pltpu.VMEM((B_HEADS, B_DK, B_DV), F32)],
        compiler_params=_params("parallel", "arbitrary"),
        name="retention",
    )(bqk, bv, bg, s0, b_gn)


def _out_ln_kernel(oa_ref, ob_ref, x_ref, w_ref, g_ref, b_ref, y_ref):
    y = _dot(oa_ref[...].astype(BF16), w_ref[0:512, :]) + _dot(ob_ref[...].astype(BF16), w_ref[512:1024, :])
    y_ref[...] = _layernorm_rows(ALPHA * x_ref[...] + y, g_ref[...], b_ref[...])


def _out_ln(oa, ob, x2, w_out, g, b, tm):
    n = x2.shape[0]
    row = lambda w: pl.BlockSpec((tm, w), lambda i: (i, 0))
    vec = pl.BlockSpec((1, D_MODEL), lambda i: (0, 0))
    return pl.pallas_call(
        _out_ln_kernel,
        grid=(n // tm,),
        in_specs=[row(512), row(512), row(D_MODEL), pl.BlockSpec((D_MODEL, D_MODEL), lambda i: (0, 0)), vec, vec],
        out_specs=row(D_MODEL),
        out_shape=jax.ShapeDtypeStruct((n, D_MODEL), F32),
        compiler_params=_params("parallel"),
        name="out_ln",
    )(oa, ob, x2, w_out, g, b)


def _out_ln_d_kernel(oc_ref, od_ref, r_ref, kr_ref, v_ref, gate_ref, x_ref, w_ref, hm_ref,
                     lnx_g_ref, lnx_b_ref, rk_ref, g_ref, b_ref, y_ref):
    hm = hm_ref[...]
    o = od_ref[...]
    mu = _dot_exact_rhs(o, hm) * (1.0 / HEAD_DIM)
    oc = o - mu
    var = _dot_exact_rhs(oc * oc, hm) * (1.0 / HEAD_DIM)
    normed = oc * lax.rsqrt(var + D_GN_EPS) * lnx_g_ref[...] + lnx_b_ref[...]
    v = v_ref[...]
    bonus = _dot_exact_rhs(r_ref[...] * kr_ref[...] * rk_ref[...], hm) * v
    od = (normed + bonus) * gate_ref[...]
    y = _dot(oc_ref[...].astype(BF16), w_ref[0:512, :]) + _dot(od.astype(BF16), w_ref[512:1024, :])
    y_ref[...] = _layernorm_rows(ALPHA * x_ref[...] + y, g_ref[...], b_ref[...])


def _head_mask():
    head = jnp.arange(D_W) // HEAD_DIM
    return (head[:, None] == head[None, :]).astype(BF16)


def _out_ln_d(oc, od, r, kr, v, gate, x2, w_out, lnx_g, lnx_b, r_k, g, b, tm):
    n = x2.shape[0]
    row = lambda w: pl.BlockSpec((tm, w), lambda i: (i, 0))
    vec = lambda w: pl.BlockSpec((1, w), lambda i: (0, 0))
    return pl.pallas_call(
        _out_ln_d_kernel,
        grid=(n // tm,),
        in_specs=[row(512)] * 6 + [row(D_MODEL), pl.BlockSpec((D_MODEL, D_MODEL), lambda i: (0, 0)),
                                   pl.BlockSpec((D_W, D_W), lambda i: (0, 0)),
                                   vec(D_W), vec(D_W), vec(D_W), vec(D_MODEL), vec(D_MODEL)],
        out_specs=row(D_MODEL),
        out_shape=jax.ShapeDtypeStruct((n, D_MODEL), F32),
        compiler_params=_params("parallel"),
        name="out_ln_d",
    )(oc, od, r, kr, v, gate, x2, w_out, _head_mask(), lnx_g, lnx_b, r_k, g, b)


def _moe_kernel(x_ref, rw_ref, rb_ref, w13_ref, w2_ref, g_ref, b_ref, y_ref, acc_ref, gate_ref, xb_ref):
    e = pl.program_id(1)
    tm = x_ref.shape[0]
    lane = lax.broadcasted_iota(jnp.int32, (tm, N_EXPERTS), 1)

    lane_f = lane.astype(F32)

    def first_argmax(v):
        m = jnp.max(v, axis=1, keepdims=True)
        idx = jnp.min(jnp.where(v == m, lane_f, float(N_EXPERTS)), axis=1, keepdims=True)
        return m, idx.astype(jnp.int32)

    @pl.when(e == 0)
    def _():
        x = x_ref[...]
        xb_ref[...] = x.astype(BF16)
        aff = _sigmoid(_dot_hi(x, rw_ref[...]))
        biased = aff + rb_ref[...]
        best = jnp.zeros((tm, 1), jnp.int32)
        best_score = jnp.full((tm, 1), -jnp.inf, F32)
        for grp in range(N_GROUPS):
            vg = jnp.where(lane // EXPERTS_PER_GROUP == grp, biased, -jnp.inf)
            top1, idx1 = first_argmax(vg)
            top2, _ = first_argmax(jnp.where(lane == idx1, -jnp.inf, vg))
            score = top1 + top2
            better = score > best_score
            best = jnp.where(better, grp, best)
            best_score = jnp.where(better, score, best_score)
        masked = jnp.where(lane // EXPERTS_PER_GROUP == best, biased, -jnp.inf)
        _, idx1 = first_argmax(masked)
        _, idx2 = first_argmax(jnp.where(lane == idx1, -jnp.inf, masked))
        top_aff = jnp.where((lane == idx1) | (lane == idx2), aff, 0.0)
        gate_ref[...] = top_aff / jnp.sum(top_aff, axis=1, keepdims=True)
        acc_ref[...] = jnp.zeros(acc_ref.shape, F32)

    h13 = _dot(xb_ref[...], w13_ref[...])
    h1 = h13[:, 0:D_EXPERT]
    h = (h1 * _sigmoid(h1)) * h13[:, D_EXPERT:2 * D_EXPERT]
    gate_e = jnp.sum(jnp.where(lane == e, gate_ref[...], 0.0), axis=1, keepdims=True)
    acc_ref[...] += gate_e * _dot(h.astype(BF16), w2_ref[...])

    @pl.when(e == N_EXPERTS - 1)
    def _():
        y_ref[...] = _layernorm_rows(ALPHA * x_ref[...] + acc_ref[...], g_ref[...], b_ref[...])


def _moe_ln(x2, router_w, router_b, w13, w2, g, b, tm):
    n = x2.shape[0]
    row = pl.BlockSpec((tm, D_MODEL), lambda i, e: (i, 0))
    vec = pl.BlockSpec((1, D_MODEL), lambda i, e: (0, 0))
    return pl.pallas_call(
        _moe_kernel,
        grid=(n // tm, N_EXPERTS),
        in_specs=[row,
                  pl.BlockSpec((D_MODEL, N_EXPERTS), lambda i, e: (0, 0)),
                  pl.BlockSpec((1, N_EXPERTS), lambda i, e: (0, 0)),
                  pl.BlockSpec((None, D_MODEL, 2 * D_EXPERT), lambda i, e: (e, 0, 0)),
                  pl.BlockSpec((None, D_EXPERT, D_MODEL), lambda i, e: (e, 0, 0)),
                  vec, vec],
        out_specs=row,
        out_shape=jax.ShapeDtypeStruct((n, D_MODEL), F32),
        scratch_shapes=[pltpu.VMEM((tm, D_MODEL), F32), pltpu.VMEM((tm, N_EXPERTS), F32),
                        pltpu.VMEM((tm, D_MODEL), BF16)],
        compiler_params=_params("parallel", "arbitrary"),
        name="moe_ln",
    )(x2, router_w, router_b, w13, w2, g, b)


def _proj_cd_kernel(x_ref, w_ref, cq_ref, ck_ref, cv_ref, pd_ref):
    xb = x_ref[...].astype(BF16)
    cq_ref[...] = _dot(xb, w_ref[:, 0:512])
    ck_ref[...] = _dot(xb, w_ref[:, 512:1024])
    cv_ref[...] = _dot(xb, w_ref[:, 1024:1536])
    pd_ref[...] = _dot(xb, w_ref[:, 1536:1536 + D_SHIFT_W])


def _proj_cd(x2, w_bf16, tm):
    n = x2.shape[0]
    row = lambda w: pl.BlockSpec((tm, w), lambda i: (i, 0))
    outs = [512, 512, 512, D_SHIFT_W]
    return pl.pallas_call(
        _proj_cd_kernel,
        grid=(n // tm,),
        in_specs=[row(D_MODEL), pl.BlockSpec((D_MODEL, 1536 + D_SHIFT_W), lambda i: (0, 0))],
        out_specs=[row(w) for w in outs],
        out_shape=[jax.ShapeDtypeStruct((n, w), F32) for w in outs],
        compiler_params=_params("parallel"),
        name="proj_cd",
    )(x2, w_bf16)


def _stick_kernel(q_ref, k_ref, v_ref, ust_ref, o_ref, *, past, tq, kb):
    i = pl.program_id(2)
    q0 = past + i * tq
    n_kb = (q0 + tq + kb - 1) // kb
    qpos = q0 + lax.broadcasted_iota(jnp.int32, (tq, 1), 0)
    lane_kb = lax.broadcasted_iota(jnp.int32, (1, kb), 1)
    n_heads = LANES // HEAD_DIM
    cols = [slice(hh * HEAD_DIM, (hh + 1) * HEAD_DIM) for hh in range(n_heads)]
    qs = [(q_ref[:, c] * (HEAD_DIM ** -0.5)).astype(BF16) for c in cols]

    def cond(c):
        j, carries, _ = c
        worst = carries[0]
        for carry in carries[1:]:
            worst = jnp.maximum(worst, carry)
        return jnp.logical_and(j >= 0, jnp.max(worst) > EXP_ZERO_BELOW)

    def body(c):
        j, carries, outs = c
        off = pl.multiple_of(j * kb, kb)
        strict = off + lane_kb < qpos
        new_carries, new_outs = [], []
        for hh in range(n_heads):
            k = k_ref[pl.ds(off, kb), cols[hh]]
            v = v_ref[pl.ds(off, kb), cols[hh]]
            z = lax.dot_general(qs[hh], k, NT_DIMS, preferred_element_type=F32)
            sp = _softplus(z)
            log_keep = jnp.where(strict, -sp, 0.0)
            later = carries[hh] + _dot_exact_rhs(log_keep, ust_ref[...])
            a = jnp.where(strict, jnp.exp((z - sp) + later), 0.0)
            new_outs.append(outs[hh] + _dot(a.astype(BF16), v))
            new_carries.append(carries[hh] + jnp.sum(log_keep, axis=1, keepdims=True))
        return j - 1, tuple(new_carries), tuple(new_outs)

    init = (n_kb - 1, tuple(jnp.zeros((tq, 1), F32) for _ in cols), tuple(jnp.zeros((tq, HEAD_DIM), F32) for _ in cols))
    _, _, outs = lax.while_loop(cond, body, init)
    for hh in range(n_heads):
        o_ref[:, cols[hh]] = outs[hh]


def _stick(cq, k_all, v_all, bsz, t_len, past, kb):
    tq = min(kb, t_len)
    nq = t_len // tq
    lp = k_all.shape[1]
    ust = jnp.tril(jnp.ones((kb, kb), F32), -1).astype(BF16)
    qrow = pl.BlockSpec((tq, LANES), lambda b, hp, i: (b * nq + i, hp))
    keys = pl.BlockSpec((None, lp, LANES), lambda b, hp, i: (b, 0, hp))
    return pl.pallas_call(
        functools.partial(_stick_kernel, past=past, tq=tq, kb=kb),
        grid=(bsz, C_HEADS * HEAD_DIM // LANES, nq),
        in_specs=[qrow, keys, keys, pl.BlockSpec((kb, kb), lambda b, hp, i: (0, 0))],
        out_specs=qrow,
        out_shape=jax.ShapeDtypeStruct((bsz * t_len, 512), F32),
        compiler_params=_params("parallel", "parallel", "arbitrary"),
        name="stick",
    )(cq, k_all, v_all, ust)


def _rwkv_pre_kernel(pd_ref, shift_ref, mu_ref, w0a0_ref, wa2_ref, gw2_ref, kk_ref_, ka_ref, hm_ref,
                     r_out, w_out, kr_out, v_out, kk_out, b_out, g_out, last_ref):
    t = pl.program_id(1)

    @pl.when(t == 0)
    def _():
        last_ref[...] = shift_ref[...]

    pd = pd_ref[...]
    tm = pd.shape[0]
    rolled = pltpu.roll(pd, 1, 0)
    first_row = lax.broadcasted_iota(jnp.int32, (tm, 1), 0) == 0
    prev = jnp.where(first_row, last_ref[...], rolled)
    last_ref[...] = pd[tm - 1:tm, :]
    pm = pd + (prev - pd) * mu_ref[...]
    r = pm[:, 0:512]
    k = pm[:, 512:1024]
    v = pm[:, 1024:1536]
    lwa = pm[:, 1536:1664]
    lg = pm[:, 1664:1792]
    lane = lax.broadcasted_iota(jnp.int32, lwa.shape, 1)
    lwa = jnp.where(lane < D_LORA_W, jnp.tanh(lwa), lwa)
    pre = w0a0_ref[...] + _dot_hi(lwa, wa2_ref[...])
    w_log = -_softplus(-pre[:, 0:512]) - 0.5
    decay = jnp.exp(-jnp.exp(w_log))
    a = _sigmoid(pre[:, 512:1024])
    g = _dot_hi(_sigmoid(lg), gw2_ref[...])
    kk = k * kk_ref_[...]
    kk = kk * lax.rsqrt(_dot_exact_rhs(kk * kk, hm_ref[...]) + 1e-12)
    r_out[...] = r
    w_out[...] = decay
    kr_out[...] = k * (1.0 + (a - 1.0) * ka_ref[...])
    v_out[...] = v
    kk_out[...] = kk
    b_out[...] = kk * a
    g_out[...] = g


def _rwkv_pre(pd, shift, mu, w0a0, wa2, gw2, k_k, k_a, bsz, t_len, tm):
    nt = t_len // tm
    row = lambda w: pl.BlockSpec((tm, w), lambda b, t: (b * nt + t, 0))
    const = lambda s: pl.BlockSpec(s, lambda b, t: (0,) * len(s))
    return pl.pallas_call(
        _rwkv_pre_kernel,
        grid=(bsz, nt),
        in_specs=[row(D_SHIFT_W), pl.BlockSpec((None, 1, D_SHIFT_W), lambda b, t: (b, 0, 0)),
                  const((1, D_SHIFT_W)), const((1, 2 * D_W)), const((LANES, 2 * D_W)), const((D_LORA_G, D_W)),
                  const((1, D_W)), const((1, D_W)), const((D_W, D_W))],
        out_specs=[row(D_W)] * 7,
        out_shape=[jax.ShapeDtypeStruct((bsz * t_len, D_W), F32)] * 7,
        scratch_shapes=[pltpu.VMEM((1, D_SHIFT_W), F32)],
        compiler_params=_params("parallel", "arbitrary"),
        name="rwkv_pre",
    )(pd, shift, mu, w0a0, wa2, gw2, k_k, k_a, _head_mask())


RWKV_VL = HEAD_DIM // 2


def _rwkv_scan_kernel(kk_ref, w_ref, b_ref, kr_ref, r_ref, v_ref, s0_ref, o_ref, s_out_ref, s_ref, *, tb):
    g = pl.program_id(0)

    @pl.when(g == 0)
    def _():
        s_ref[...] = s0_ref[...]

    def step(t, carry):
        kk = kk_ref[t]
        w = w_ref[t]
        b = b_ref[t]
        kr = kr_ref[t]
        r = r_ref[t]
        for vl in range(RWKV_VL):
            s = s_ref[vl]
            sa = jnp.sum(s * kk, axis=0, keepdims=True)
            s = s * w - sa * b + v_ref[t, vl:vl + 1, :] * kr
            s_ref[vl] = s
            o_ref[t, vl:vl + 1, :] = jnp.sum(s * r, axis=0, keepdims=True)
        return carry

    lax.fori_loop(0, tb, step, 0)

    @pl.when(g == pl.num_programs(0) - 1)
    def _():
        s_out_ref[...] = s_ref[...]


def _rwkv_scan(kk, w, b, kr, r, v, s0, t_len, tb):
    big = pl.BlockSpec((tb, HEAD_DIM, LANES), lambda g: (g, 0, 0))
    small = pl.BlockSpec((tb, RWKV_VL, LANES), lambda g: (g, 0, 0))
    state = pl.BlockSpec((RWKV_VL, HEAD_DIM, LANES), lambda g: (0, 0, 0))
    return pl.pallas_call(
        functools.partial(_rwkv_scan_kernel, tb=tb),
        grid=(t_len // tb,),
        in_specs=[big] * 5 + [small, state],
        out_specs=[small, state],
        out_shape=[jax.ShapeDtypeStruct((t_len, RWKV_VL, LANES), F32),
                   jax.ShapeDtypeStruct((RWKV_VL, HEAD_DIM, LANES), F32)],
        scratch_shapes=[pltpu.VMEM((RWKV_VL, HEAD_DIM, LANES), F32)],
        compiler_params=_params("arbitrary"),
        name="rwkv_scan",
    )(kk, w, b, kr, r, v, s0)


def _to_scan_keyed(x2, bsz, t_len):
    y = x2.reshape(bsz, t_len, D_HEADS, HEAD_DIM).transpose(1, 3, 0, 2).reshape(t_len, HEAD_DIM, bsz * D_HEADS)
    return jnp.concatenate([y, y], axis=-1)


def _to_scan_valued(x2, bsz, t_len):
    y = x2.reshape(bsz, t_len, D_HEADS, 2, RWKV_VL).transpose(1, 4, 3, 0, 2)
    return y.reshape(t_len, RWKV_VL, 2 * bsz * D_HEADS)


def _from_scan_valued(y, bsz, t_len):
    return y.reshape(t_len, RWKV_VL, 2, bsz, D_HEADS).transpose(3, 0, 4, 2, 1).reshape(bsz * t_len, D_W)


def _state_to_scan(s, bsz):
    y = s.reshape(bsz, D_HEADS, 2, RWKV_VL, HEAD_DIM).transpose(3, 4, 2, 0, 1)
    return y.reshape(RWKV_VL, HEAD_DIM, 2 * bsz * D_HEADS)


def _state_from_scan(y, bsz):
    return y.reshape(RWKV_VL, HEAD_DIM, 2, bsz, D_HEADS).transpose(3, 4, 2, 0, 1).reshape(
        bsz, D_HEADS, HEAD_DIM, HEAD_DIM)


def _pad_keys(x, mult):
    pad = (-x.shape[1]) % mult
    return x if pad == 0 else jnp.pad(x, ((0, 0), (0, pad), (0, 0)))


def _tile(n, cap):
    return min(n, cap)


def _run_group(x, past, p):
    bsz, t_len, _ = x.shape
    assert bsz * D_HEADS * 2 == LANES, "rwkv scan packs (value half, batch, head) into the lane axis"
    n = bsz * t_len
    x2 = x.reshape(n, D_MODEL)
    past_len = 0 if past is None else past[0].shape[2]
    tm = _tile(t_len, 512)

    cos, sin = _rope_tables(past_len, t_len)
    aq, iq, bqk, kik, aviw, bv, bg = _proj_ab(x2, p['w_in_ab'], cos, sin, t_len, tm)
    ak = kik[:, 0:64].reshape(1, bsz, t_len, 64)
    ik = kik[:, 64:128].reshape(1, bsz, t_len, 64)
    av = aviw[:, 0:64].reshape(1, bsz, t_len, 64)
    kik_all = kik.reshape(bsz, t_len, LANES).astype(BF16)
    ones_col = (jnp.arange(LANES) == 64).astype(F32)
    v_all = jnp.where(jnp.arange(LANES) < 64, aviw, ones_col).reshape(bsz, t_len, LANES).astype(BF16)
    if past is None:
        s_b = jnp.zeros((bsz, B_HEADS, B_DK, B_DV), F32)
    else:
        pk, pv, pik, sb = past[0][0], past[1][0], past[2][0], past[3][0]
        kik_all = jnp.concatenate([jnp.concatenate([pk, pik], axis=-1).astype(BF16), kik_all], axis=1)
        pv_slab = jnp.concatenate([pv, jnp.broadcast_to(ones_col[64:], pv.shape)], axis=-1)
        v_all = jnp.concatenate([pv_slab.astype(BF16), v_all], axis=1)
        s_b = sb
    kt = 512
    o_a = _dsa(aq, iq, aviw, _pad_keys(kik_all, kt), _pad_keys(v_all, kt), bsz, t_len, past_len, kt)
    o_b, s_b_new = _retention(bqk, bv, bg, s_b, p['b_gn'], bsz, t_len, tm)
    x2 = _out_ln(o_a, o_b, x2, p['w_out_ab'], p['ln_g'][0, 0][None], p['ln_b'][0, 0][None], tm)
    tm_moe = _tile(n, 1024)
    x2 = _moe_ln(x2, p['router_w'], p['router_b'], p['w13'][0], p['w2'][0],
                 p['ln_g'][0, 1][None], p['ln_b'][0, 1][None], tm_moe)

    cq, ck, cv, pd = _proj_cd(x2, p['w_in_cd'], tm)
    ck_all = ck.reshape(bsz, t_len, 512).astype(BF16)
    cv_all = cv.reshape(bsz, t_len, 512).astype(BF16)
    if past is None:
        s_d = jnp.zeros((bsz, D_HEADS, HEAD_DIM, HEAD_DIM), F32)
        shift = jnp.zeros((bsz, 1, D_SHIFT_W), F32)
    else:
        ck_all = jnp.concatenate([past[4][0].reshape(bsz, past_len, 512).astype(BF16), ck_all], axis=1)
        cv_all = jnp.concatenate([past[5][0].reshape(bsz, past_len, 512).astype(BF16), cv_all], axis=1)
        s_d, shift = past[6][0], past[7][0]
    kb = 256
    o_c = _stick(cq, _pad_keys(ck_all, kb), _pad_keys(cv_all, kb), bsz, t_len, past_len, kb)
    r, w, kr, v, kk, b, g = _rwkv_pre(pd, shift, p['d_mu'], p['d_w0a0'], p['d_wa2'], p['d_g2'],
                                      p['d_k_k'], p['d_k_a'], bsz, t_len, tm)
    keyed = [_to_scan_keyed(u, bsz, t_len) for u in (kk, w, b, kr, r)]
    o_scan, s_scan = _rwkv_scan(*keyed, _to_scan_valued(v, bsz, t_len), _state_to_scan(s_d, bsz),
                                t_len, _tile(t_len, 64))
    o_d = _from_scan_valued(o_scan, bsz, t_len)
    s_d_new = _state_from_scan(s_scan, bsz)
    x2 = _out_ln_d(o_c, o_d, r, kr, v, g, x2, p['w_out_cd'], p['d_lnx_g'], p['d_lnx_b'], p['d_r_k'],
                   p['ln_g'][1, 0][None], p['ln_b'][1, 0][None], tm)
    x2 = _moe_ln(x2, p['router_w'], p['router_b'], p['w13'][1], p['w2'][1],
                 p['ln_g'][1, 1][None], p['ln_b'][1, 1][None], tm_moe)

    states = (ak, av, ik, s_b_new[None],
              ck.reshape(1, bsz, t_len, C_HEADS, HEAD_DIM), cv.reshape(1, bsz, t_len, C_HEADS, HEAD_DIM),
              s_d_new[None], pd.reshape(bsz, t_len, D_SHIFT_W)[:, -1:][None])
    return x2.reshape(bsz, t_len, D_MODEL), states


def kernel(x_prompt, x_sample, cache_a_k, cache_a_v, cache_a_idx_k, state_b, cache_c_k, cache_c_v, state_d, state_d_shift, w_in_ab, w_out_ab, b_gn, w_in_cd, w_out_cd, d_mu, d_w0, d_w2, d_a0, d_a2, d_g2, d_k_k, d_k_a, d_r_k, d_lnx_g, d_lnx_b, ln_g, ln_b, router_w, router_b, moe_w1, moe_w3, moe_w2):
    zeros_w = jnp.zeros((D_LORA_W, D_W), F32)
    p = {
        'w_in_ab': _pack_w_in_ab(w_in_ab[0]),
        'w_out_ab': w_out_ab[0].astype(BF16),
        'b_gn': b_gn,
        'w_in_cd': w_in_cd[0].astype(BF16),
        'w_out_cd': w_out_cd[0].astype(BF16),
        'd_mu': d_mu,
        'd_w0a0': jnp.concatenate([d_w0, d_a0], axis=1),
        'd_wa2': jnp.concatenate([jnp.concatenate([d_w2[0], zeros_w], axis=1),
                                  jnp.concatenate([zeros_w, d_a2[0]], axis=1)], axis=0),
        'd_g2': d_g2[0],
        'd_k_k': d_k_k, 'd_k_a': d_k_a, 'd_r_k': d_r_k, 'd_lnx_g': d_lnx_g, 'd_lnx_b': d_lnx_b,
        'ln_g': ln_g, 'ln_b': ln_b,
        'router_w': router_w, 'router_b': router_b[None],
        'w13': jnp.concatenate([moe_w1, moe_w3], axis=-1).astype(BF16),
        'w2': moe_w2.astype(BF16),
    }
    y_p, sp = _run_group(x_prompt, None, p)
    past = (cache_a_k, cache_a_v, cache_a_idx_k, state_b, cache_c_k, cache_c_v, state_d, state_d_shift)
    y_s, ss = _run_group(x_sample, past, p)
    return (y_p, y_s, sp[0], sp[1], sp[2], ss[0], ss[1], ss[2], sp[3], ss[3], sp[4], sp[5], ss[4], ss[5],
            sp[6], ss[6], sp[7], ss[7])
```

```python
import functools
import math

import jax
import jax.numpy as jnp
import numpy as np
from jax import lax
from jax.experimental import pallas as pl
from jax.experimental.pallas import tpu as pltpu

F32 = jnp.float32
BF16 = jnp.bfloat16

D_MODEL = 1024
CHUNK = 64
QBLOCK = 128
ROPE_THETA = 10000.0
HEAD_DIM = 64
LN_EPS = 1e-5
A_HEADS = 8
IDX_HEADS = 8
TOPK_MAX = 256
B_HEADS = 4
B_DK = 64
B_DV = 128
C_HEADS = 8
D_HEADS = 8
D_LORA_W = 64
D_LORA_A = 64
D_LORA_G = 128
D_GN_EPS = 64e-5
N_EXPERTS = 16
N_GROUPS = 4
EXPERTS_PER_GROUP = 4
D_EXPERT = 256
DEPTH = 2
ALPHA = (2 * DEPTH) ** 0.25
D_W = D_HEADS * HEAD_DIM
D_SHIFT_W = 3 * D_W + D_LORA_W + D_LORA_A + D_LORA_G

LANES = 128
SUBLANES = 8
VMEM_LIMIT_BYTES = 56 * 1024 * 1024

INT_MIN = -(2 ** 31)
EXP_ZERO_BELOW = -104.0

NT_DIMS = (((1,), (1,)), ((), ()))
TN_DIMS = (((0,), (0,)), ((), ()))


def _params(*sem):
    return pltpu.CompilerParams(dimension_semantics=sem, vmem_limit_bytes=VMEM_LIMIT_BYTES)


def _dot(a, b):
    return jnp.dot(a, b, preferred_element_type=F32)


def _dot_hi(a, b):
    return jnp.dot(a, b, preferred_element_type=F32, precision=lax.Precision.HIGHEST)


def _split3(x):
    h1 = x.astype(BF16)
    r1 = x - h1.astype(F32)
    h2 = r1.astype(BF16)
    r2 = r1 - h2.astype(F32)
    return h1, h2, r2.astype(BF16)


def _dot_exact_rhs(x, m01):
    h1, h2, h3 = _split3(x)
    return _dot(h1, m01) + _dot(h2, m01) + _dot(h3, m01)


def _layernorm_rows(x, g, b):
    mu = jnp.mean(x, axis=-1, keepdims=True)
    xc = x - mu
    var = jnp.mean(xc * xc, axis=-1, keepdims=True)
    return xc * lax.rsqrt(var + LN_EPS) * g + b


def _sigmoid(x):
    return 1.0 / (1.0 + jnp.exp(-x))


def _softplus(x):
    return jnp.maximum(x, 0.0) + jnp.log1p(jnp.exp(-jnp.abs(x)))


def _rope_slab(x, cos, sin_signed):
    lane = lax.broadcasted_iota(jnp.int32, x.shape, 1)
    first_half = (lane % HEAD_DIM) < (HEAD_DIM // 2)
    swapped = jnp.where(first_half, pltpu.roll(x, LANES - HEAD_DIM // 2, 1), pltpu.roll(x, HEAD_DIM // 2, 1))
    return x * cos + swapped * sin_signed


AB_ROPED = 1664
AB_PACKED = 2816


def _pack_w_in_ab(w):
    aq, ak, av, iq, ik, iw, bq, bk, bv, bg = jnp.split(
        w, [512, 576, 640, 1152, 1216, 1224, 1480, 1736, 2248], axis=1)
    pad = jnp.zeros((w.shape[0], LANES - 64 - IDX_HEADS), w.dtype)
    return jnp.concatenate([aq, iq, bq, bk, ak, ik, av, iw, pad, bv, bg], axis=1).astype(BF16)


def _proj_ab_kernel(x_ref, w_ref, cos_ref, sin_ref, aq_ref, iq_ref, bqk_ref, kik_ref, aviw_ref, bv_ref, bg_ref,
                    kik16_ref, v16_ref):
    xb = x_ref[...].astype(BF16)
    cos = cos_ref[...]
    sin = sin_ref[...]

    def roped(col0, width, scale_from=None):
        y = _dot(xb, w_ref[:, col0:col0 + width])
        parts = []
        for c in range(width // LANES):
            slab = _rope_slab(y[:, c * LANES:(c + 1) * LANES], cos, sin)
            if scale_from is not None and c * LANES >= scale_from:
                slab = slab * (B_DK ** -0.5)
            parts.append(slab)
        return parts

    for c, slab in enumerate(roped(0, 512)):
        aq_ref[:, c * LANES:(c + 1) * LANES] = slab
    for c, slab in enumerate(roped(512, 512)):
        iq_ref[:, c * LANES:(c + 1) * LANES] = slab
    for c, slab in enumerate(roped(1024, 512, scale_from=256)):
        bqk_ref[:, c * LANES:(c + 1) * LANES] = slab
    kik = roped(1536, LANES)[0]
    kik_ref[...] = kik
    kik16_ref[...] = kik.astype(BF16)
    aviw = _dot(xb, w_ref[:, AB_ROPED:AB_ROPED + LANES])
    lane = lax.broadcasted_iota(jnp.int32, aviw.shape, 1)
    is_iw = (lane >= 64) & (lane < 64 + IDX_HEADS)
    aviw_ref[...] = jnp.where(is_iw, aviw * ((IDX_HEADS * HEAD_DIM) ** -0.5), aviw)
    v16_ref[...] = jnp.where(lane < 64, aviw, jnp.where(lane == 64, 1.0, 0.0)).astype(BF16)
    bv_ref[...] = _dot(xb, w_ref[:, 1792:2304])
    bg_ref[...] = _dot(xb, w_ref[:, 2304:2816])


def _proj_ab(x2, w_packed, cos, sin, t_len, tm):
    n = x2.shape[0]
    nt = t_len // tm
    row = lambda w: pl.BlockSpec((tm, w), lambda i: (i, 0))
    tab = pl.BlockSpec((tm, LANES), lambda i: (i % nt, 0))
    outs = [512, 512, 512, LANES, LANES, 512, 512]
    return pl.pallas_call(
        _proj_ab_kernel,
        grid=(n // tm,),
        in_specs=[row(D_MODEL), pl.BlockSpec((D_MODEL, AB_PACKED), lambda i: (0, 0)), tab, tab],
        out_specs=[row(w) for w in outs] + [row(LANES), row(LANES)],
        out_shape=[jax.ShapeDtypeStruct((n, w), F32) for w in outs] + [jax.ShapeDtypeStruct((n, LANES), BF16)] * 2,
        compiler_params=_params("parallel"),
        name="proj_ab",
    )(x2, w_packed, cos, sin)


def _rope_tables(past, t_len):
    half = HEAD_DIM // 2
    inv = ROPE_THETA ** (-jnp.arange(half, dtype=F32) / half)
    ang = (past + jnp.arange(t_len)).astype(F32)[:, None] * inv[None, :]
    c, s = jnp.cos(ang), jnp.sin(ang)
    return jnp.concatenate([c, c, c, c], axis=1), jnp.concatenate([-s, s, -s, s], axis=1)


def _dsa_kernel(aq_ref, iq_ref, aviw_ref, kik_ref, v_ref, tri_ref, o_ref,
                skey_ref, half_ref, iqs_ref, iwb_ref, qs_ref, p_ref, m_ref, mlane_ref, acc_ref,
                *, past, qb, kt, topk):
    i = pl.program_id(1)
    q0 = past + i * qb
    n_tiles = (q0 + qb + kt - 1) // kt
    row = lax.broadcasted_iota(jnp.int32, (qb, 1), 0)
    vis_end = ((q0 + row) // CHUNK + 1) * CHUNK
    lane_kt = lax.broadcasted_iota(jnp.int32, (1, kt), 1)

    for h in range(IDX_HEADS):
        iqs_ref[h * qb:(h + 1) * qb, :] = iq_ref[:, h * 64:(h + 1) * 64].astype(BF16)
        iwb_ref[h] = jnp.broadcast_to(aviw_ref[:, 64 + h:65 + h], (qb, LANES))

    def score_tile(j, carry):
        off = pl.multiple_of(j * kt, kt)
        ik = kik_ref[pl.ds(off, kt), 64:128]
        s = lax.dot_general(iqs_ref[...], ik, NT_DIMS, preferred_element_type=F32)
        parts = []
        for c in range(kt // LANES):
            a = jnp.zeros((qb, LANES), F32)
            for h in range(IDX_HEADS):
                a = a + iwb_ref[h] * jnp.maximum(s[h * qb:(h + 1) * qb, c * LANES:(c + 1) * LANES], 0.0)
            parts.append(a)
        acc = jnp.concatenate(parts, axis=1)
        acc = acc + 0.0
        acc = jnp.where(off + lane_kt < vis_end, acc, -jnp.inf)
        bits = pltpu.bitcast(acc, jnp.int32)
        key = jnp.where(bits < 0, bits ^ 0x7FFFFFFF, bits)
        skey_ref[:, pl.ds(off, kt)] = key
        half_ref[:, pl.ds(off, kt)] = jnp.right_shift(key, 16).astype(jnp.int16)
        return carry

    lax.fori_loop(0, n_tiles, score_tile, 0)

    def count(pred_fn):
        def body(j, cnt):
            off = pl.multiple_of(j * kt, kt)
            hit = pred_fn(skey_ref[:, pl.ds(off, kt)])
            for c in range(kt // LANES):
                cnt = cnt + jnp.where(hit[:, c * LANES:(c + 1) * LANES], 1.0, 0.0)
            return cnt
        cnt = lax.fori_loop(0, n_tiles, body, jnp.zeros((qb, LANES), F32))
        return jnp.sum(cnt, axis=1, keepdims=True)

    def count_half_ge(cand):
        cand16 = jnp.broadcast_to(cand, (qb, LANES)).astype(jnp.int16)
        one = jnp.ones((qb, LANES), jnp.int16)
        nil = jnp.zeros((qb, LANES), jnp.int16)

        def body(j, cnt):
            off = pl.multiple_of(j * kt, kt)
            tile = half_ref[:, pl.ds(off, kt)]
            for c in range(kt // LANES):
                cnt = cnt + jnp.where(tile[:, c * LANES:(c + 1) * LANES] >= cand16, one, nil)
            return cnt
        cnt = lax.fori_loop(0, n_tiles, body, nil)
        return jnp.sum(cnt.astype(F32), axis=1, keepdims=True)

    def bisect16(cnt_min, extra):
        zero = jnp.zeros((qb, 1), jnp.int32)
        cnt = extra + count_half_ge(zero)
        ok = cnt >= topk
        start = (jnp.where(ok, zero, jnp.full((qb, 1), -(2 ** 15), jnp.int32)), jnp.where(ok, cnt, cnt_min))

        def bit_step(it, c):
            t, cnt_t = c
            cand = t | jnp.left_shift(jnp.int32(1), 14 - it)
            cnt = extra + count_half_ge(cand)
            ok = cnt >= topk
            return jnp.where(ok, cand, t), jnp.where(ok, cnt, cnt_t)
        return lax.fori_loop(0, 15, bit_step, start)

    visited = jnp.full((qb, 1), n_tiles * kt, jnp.int32).astype(F32)
    t_hi, cnt_hi = bisect16(visited, 0.0)
    top16 = 2 ** 15 - 1
    above = jnp.where(t_hi == top16, 0.0, count_half_ge(jnp.minimum(t_hi + 1, top16)))

    def low_tile(j, carry):
        off = pl.multiple_of(j * kt, kt)
        key = skey_ref[:, pl.ds(off, kt)]
        low = (key & 0xFFFF) - 2 ** 15
        half_ref[:, pl.ds(off, kt)] = jnp.where(jnp.right_shift(key, 16) == t_hi, low, -(2 ** 15)).astype(jnp.int16)
        return carry

    lax.fori_loop(0, n_tiles, low_tile, 0)
    t_lo, cnt_ge = bisect16(cnt_hi, above)
    thr = jnp.left_shift(t_hi, 16) | (t_lo + 2 ** 15)
    lane_ok = lambda off: off + lane_kt < vis_end

    def select_all_ties(j, carry):
        off = pl.multiple_of(j * kt, kt)
        sel = (skey_ref[:, pl.ds(off, kt)] >= thr) & lane_ok(off)
        skey_ref[:, pl.ds(off, kt)] = pltpu.bitcast(jnp.where(sel, 0.0, -jnp.inf), jnp.int32)
        return carry

    def select_ranked_ties(need):
        def body(j, eq_seen):
            off = pl.multiple_of(j * kt, kt)
            key = skey_ref[:, pl.ds(off, kt)]
            eq = key == thr
            rank = _dot(jnp.where(eq, 1.0, 0.0).astype(BF16), tri_ref[...]) + eq_seen
            sel = ((key > thr) | (eq & (rank <= need))) & lane_ok(off)
            skey_ref[:, pl.ds(off, kt)] = pltpu.bitcast(jnp.where(sel, 0.0, -jnp.inf), jnp.int32)
            return rank[:, kt - 1:kt]
        return body

    def exact_fit():
        lax.fori_loop(0, n_tiles, select_all_ties, 0)

    def surplus_ties():
        need = topk - count(lambda x: x > thr)
        lax.fori_loop(0, n_tiles, select_ranked_ties(need), jnp.zeros((qb, 1), F32))

    lax.cond(jnp.max(jnp.abs(cnt_ge - topk)) == 0.0, exact_fit, surplus_ties)

    for h in range(A_HEADS):
        qs_ref[h * qb:(h + 1) * qb, :] = (aq_ref[:, h * 64:(h + 1) * 64] * (HEAD_DIM ** -0.5)).astype(BF16)
    mlane_ref[...] = jnp.full(mlane_ref.shape, -jnp.inf, F32)
    acc_ref[...] = jnp.zeros(acc_ref.shape, F32)

    def max_tile(j, carry):
        off = pl.multiple_of(j * kt, kt)
        bias = pltpu.bitcast(skey_ref[:, pl.ds(off, kt)], F32)
        k = kik_ref[pl.ds(off, kt), 0:64]
        s = lax.dot_general(qs_ref[...], k, NT_DIMS, preferred_element_type=F32)
        for h in range(A_HEADS):
            sh = s[h * qb:(h + 1) * qb, :] + bias
            mm = sh[:, 0:LANES]
            for c in range(1, kt // LANES):
                mm = jnp.maximum(mm, sh[:, c * LANES:(c + 1) * LANES])
            mlane_ref[h] = jnp.maximum(mlane_ref[h], mm)
        return carry

    lax.fori_loop(0, n_tiles, max_tile, 0)
    for h in range(A_HEADS):
        m_ref[h * qb:(h + 1) * qb, :] = jnp.max(mlane_ref[h], axis=1, keepdims=True)

    def pv_tile(j, carry):
        off = pl.multiple_of(j * kt, kt)
        bias = pltpu.bitcast(skey_ref[:, pl.ds(off, kt)], F32)
        k = kik_ref[pl.ds(off, kt), 0:64]
        s = lax.dot_general(qs_ref[...], k, NT_DIMS, preferred_element_type=F32)
        for h in range(A_HEADS):
            rows = slice(h * qb, (h + 1) * qb)
            p_ref[rows, :] = jnp.exp((s[rows, :] + bias) - m_ref[rows, :]).astype(BF16)
        acc_ref[...] += _dot(p_ref[...], v_ref[pl.ds(off, kt), :])
        return carry

    lax.fori_loop(0, n_tiles, pv_tile, 0)
    for h in range(A_HEADS):
        a = acc_ref[h * qb:(h + 1) * qb, :]
        o_ref[:, h * 64:(h + 1) * 64] = (a / a[:, 64:65])[:, 0:64]


def _dsa(aq, iq, aviw, kik_all, v_all, bsz, t_len, past, kt):
    qb = min(QBLOCK, t_len)
    nq = t_len // qb
    lp = kik_all.shape[1]
    topk = min(TOPK_MAX, (past + t_len) // 4)
    tri = jnp.triu(jnp.ones((kt, kt), F32)).astype(BF16)
    qrow = lambda w: pl.BlockSpec((qb, w), lambda b, i: (b * nq + i, 0))
    keys = pl.BlockSpec((None, lp, LANES), lambda b, i: (b, 0, 0))
    return pl.pallas_call(
        functools.partial(_dsa_kernel, past=past, qb=qb, kt=kt, topk=topk),
        grid=(bsz, nq),
        in_specs=[qrow(512), qrow(512), qrow(LANES), keys, keys, pl.BlockSpec((kt, kt), lambda b, i: (0, 0))],
        out_specs=qrow(512),
        out_shape=jax.ShapeDtypeStruct((bsz * t_len, 512), F32),
        scratch_shapes=[
            pltpu.VMEM((qb, lp), jnp.int32),
            pltpu.VMEM((qb, lp), jnp.int16),
            pltpu.VMEM((IDX_HEADS * qb, 64), BF16),
            pltpu.VMEM((IDX_HEADS, qb, LANES), F32),
            pltpu.VMEM((A_HEADS * qb, 64), BF16),
            pltpu.VMEM((A_HEADS * qb, kt), BF16),
            pltpu.VMEM((A_HEADS * qb, 1), F32),
            pltpu.VMEM((A_HEADS, qb, LANES), F32),
            pltpu.VMEM((A_HEADS * qb, LANES), F32),
        ],
        compiler_params=_params("parallel", "arbitrary"),
        name="dsa",
    )(aq, iq, aviw, kik_all, v_all, tri)


def _retention_kernel(bqk_ref, bv_ref, bg_ref, s0_ref, gn_ref, o_ref, s_out_ref, s_ref, *, n_chunks):
    t = pl.program_id(1)

    @pl.when(t == 0)
    def _():
        s_ref[...] = s0_ref[...]

    n = CHUNK
    ri = lax.broadcasted_iota(jnp.int32, (n, n), 0).astype(F32)
    ci = lax.broadcasted_iota(jnp.int32, (n, n), 1).astype(F32)
    diff = ri - ci
    pos = lax.broadcasted_iota(jnp.int32, (n, 1), 0).astype(F32)
    for h in range(B_HEADS):
        log_g = math.log(1.0 - 2.0 ** (-5.0 - h))
        intra = jnp.where(diff >= 0, jnp.exp(jnp.maximum(diff, 0.0) * log_g), 0.0)
        q_decay = jnp.exp((pos + 1.0) * log_g)
        k_decay = jnp.exp((n - 1.0 - pos) * log_g)
        chunk_decay = math.exp(n * log_g)
        s = s_ref[h]
        for c in range(n_chunks):
            rows = slice(c * n, (c + 1) * n)
            q = bqk_ref[rows, h * B_DK:(h + 1) * B_DK]
            k = bqk_ref[rows, 256 + h * B_DK:256 + (h + 1) * B_DK]
            v = bv_ref[rows, h * B_DV:(h + 1) * B_DV]
            scores = lax.dot_general(q, k, NT_DIMS, preferred_element_type=F32,
                                     precision=lax.Precision.HIGHEST) * intra
            o = _dot_hi(scores, v) + _dot_hi(q, s) * q_decay
            s = s * chunk_decay + lax.dot_general(k * k_decay, v, TN_DIMS, preferred_element_type=F32,
                                                  precision=lax.Precision.HIGHEST)
            mu = jnp.mean(o, axis=-1, keepdims=True)
            oc = o - mu
            var = jnp.mean(oc * oc, axis=-1, keepdims=True)
            g = bg_ref[rows, h * B_DV:(h + 1) * B_DV]
            o_ref[rows, h * B_DV:(h + 1) * B_DV] = (
                oc * lax.rsqrt(var + LN_EPS) * gn_ref[:, h * B_DV:(h + 1) * B_DV] * (g * _sigmoid(g)))
        s_ref[h] = s

    @pl.when(t == pl.num_programs(1) - 1)
    def _():
        s_out_ref[...] = s_ref[...]


def _retention(bqk, bv, bg, s0, b_gn, bsz, t_len, tt):
    nt = t_len // tt
    row = pl.BlockSpec((tt, 512), lambda b, t: (b * nt + t, 0))
    state = pl.BlockSpec((None, B_HEADS, B_DK, B_DV), lambda b, t: (b, 0, 0, 0))
    return pl.pallas_call(
        functools.partial(_retention_kernel, n_chunks=tt // CHUNK),
        grid=(bsz, nt),
        in_specs=[row, row, row, state, pl.BlockSpec((1, 512), lambda b, t: (0, 0))],
        out_specs=[row, state],
        out_shape=[jax.ShapeDtypeStruct((bsz * t_len, 512), F32),
                   jax.ShapeDtypeStruct((bsz, B_HEADS, B_DK, B_DV), F32)],
        scratch_shapes=[pltpu.VMEM((B_HEADS, B_DK, B_DV), F32)],
        compiler_params=_params("parallel", "arbitrary"),
        name="retention",
    )(bqk, bv, bg, s0, b_gn)


def _out_ln_kernel(oa_ref, ob_ref, x_ref, w_ref, g_ref, b_ref, y_ref):
    y = _dot(oa_ref[...].astype(BF16), w_ref[0:512, :]) + _dot(ob_ref[...].astype(BF16), w_ref[512:1024, :])
    y_ref[...] = _layernorm_rows(ALPHA * x_ref[...] + y, g_ref[...], b_ref[...])


def _out_ln(oa, ob, x2, w_out, g, b, tm):
    n = x2.shape[0]
    row = lambda w: pl.BlockSpec((tm, w), lambda i: (i, 0))
    vec = pl.BlockSpec((1, D_MODEL), lambda i: (0, 0))
    return pl.pallas_call(
        _out_ln_kernel,
        grid=(n // tm,),
        in_specs=[row(512), row(512), row(D_MODEL), pl.BlockSpec((D_MODEL, D_MODEL), lambda i: (0, 0)), vec, vec],
        out_specs=row(D_MODEL),
        out_shape=jax.ShapeDtypeStruct((n, D_MODEL), F32),
        compiler_params=_params("parallel"),
        name="out_ln",
    )(oa, ob, x2, w_out, g, b)


def _out_ln_d_kernel(oc_ref, od_ref, r_ref, kr_ref, v_ref, gate_ref, x_ref, w_ref, hm_ref,
                     lnx_g_ref, lnx_b_ref, rk_ref, g_ref, b_ref, y_ref):
    hm = hm_ref[...]
    o = od_ref[...]
    mu = _dot_exact_rhs(o, hm) * (1.0 / HEAD_DIM)
    oc = o - mu
    var = _dot_exact_rhs(oc * oc, hm) * (1.0 / HEAD_DIM)
    normed = oc * lax.rsqrt(var + D_GN_EPS) * lnx_g_ref[...] + lnx_b_ref[...]
    v = v_ref[...]
    bonus = _dot_exact_rhs(r_ref[...] * kr_ref[...] * rk_ref[...], hm) * v
    od = (normed + bonus) * gate_ref[...]
    y = _dot(oc_ref[...].astype(BF16), w_ref[0:512, :]) + _dot(od.astype(BF16), w_ref[512:1024, :])
    y_ref[...] = _layernorm_rows(ALPHA * x_ref[...] + y, g_ref[...], b_ref[...])


def _head_mask():
    head = jnp.arange(D_W) // HEAD_DIM
    return (head[:, None] == head[None, :]).astype(BF16)


def _out_ln_d(oc, od, r, kr, v, gate, x2, w_out, lnx_g, lnx_b, r_k, g, b, tm):
    n = x2.shape[0]
    row = lambda w: pl.BlockSpec((tm, w), lambda i: (i, 0))
    vec = lambda w: pl.BlockSpec((1, w), lambda i: (0, 0))
    return pl.pallas_call(
        _out_ln_d_kernel,
        grid=(n // tm,),
        in_specs=[row(512)] * 6 + [row(D_MODEL), pl.BlockSpec((D_MODEL, D_MODEL), lambda i: (0, 0)),
                                   pl.BlockSpec((D_W, D_W), lambda i: (0, 0)),
                                   vec(D_W), vec(D_W), vec(D_W), vec(D_MODEL), vec(D_MODEL)],
        out_specs=row(D_MODEL),
        out_shape=jax.ShapeDtypeStruct((n, D_MODEL), F32),
        compiler_params=_params("parallel"),
        name="out_ln_d",
    )(oc, od, r, kr, v, gate, x2, w_out, _head_mask(), lnx_g, lnx_b, r_k, g, b)


def _moe_kernel(x_ref, rw_ref, rb_ref, w13_ref, w2_ref, g_ref, b_ref, y_ref, acc_ref, gate_ref, xb_ref):
    e = pl.program_id(1)
    tm = x_ref.shape[0]
    lane = lax.broadcasted_iota(jnp.int32, (tm, N_EXPERTS), 1)

    lane_f = lane.astype(F32)

    def first_argmax(v):
        m = jnp.max(v, axis=1, keepdims=True)
        idx = jnp.min(jnp.where(v == m, lane_f, float(N_EXPERTS)), axis=1, keepdims=True)
        return m, idx.astype(jnp.int32)

    @pl.when(e == 0)
    def _():
        x = x_ref[...]
        xb_ref[...] = x.astype(BF16)
        aff = _sigmoid(_dot_hi(x, rw_ref[...]))
        biased = aff + rb_ref[...]
        best = jnp.zeros((tm, 1), jnp.int32)
        best_score = jnp.full((tm, 1), -jnp.inf, F32)
        for grp in range(N_GROUPS):
            vg = jnp.where(lane // EXPERTS_PER_GROUP == grp, biased, -jnp.inf)
            top1, idx1 = first_argmax(vg)
            top2, _ = first_argmax(jnp.where(lane == idx1, -jnp.inf, vg))
            score = top1 + top2
            better = score > best_score
            best = jnp.where(better, grp, best)
            best_score = jnp.where(better, score, best_score)
        masked = jnp.where(lane // EXPERTS_PER_GROUP == best, biased, -jnp.inf)
        _, idx1 = first_argmax(masked)
        _, idx2 = first_argmax(jnp.where(lane == idx1, -jnp.inf, masked))
        top_aff = jnp.where((lane == idx1) | (lane == idx2), aff, 0.0)
        gate_ref[...] = top_aff / jnp.sum(top_aff, axis=1, keepdims=True)
        acc_ref[...] = jnp.zeros(acc_ref.shape, F32)

    h13 = _dot(xb_ref[...], w13_ref[...])
    h1 = h13[:, 0:D_EXPERT]
    h = (h1 * _sigmoid(h1)) * h13[:, D_EXPERT:2 * D_EXPERT]
    gate_e = jnp.sum(jnp.where(lane == e, gate_ref[...], 0.0), axis=1, keepdims=True)
    acc_ref[...] += gate_e * _dot(h.astype(BF16), w2_ref[...])

    @pl.when(e == N_EXPERTS - 1)
    def _():
        y_ref[...] = _layernorm_rows(ALPHA * x_ref[...] + acc_ref[...], g_ref[...], b_ref[...])


def _moe_ln(x2, router_w, router_b, w13, w2, g, b, tm):
    n = x2.shape[0]
    row = pl.BlockSpec((tm, D_MODEL), lambda i, e: (i, 0))
    vec = pl.BlockSpec((1, D_MODEL), lambda i, e: (0, 0))
    return pl.pallas_call(
        _moe_kernel,
        grid=(n // tm, N_EXPERTS),
        in_specs=[row,
                  pl.BlockSpec((D_MODEL, N_EXPERTS), lambda i, e: (0, 0)),
                  pl.BlockSpec((1, N_EXPERTS), lambda i, e: (0, 0)),
                  pl.BlockSpec((None, D_MODEL, 2 * D_EXPERT), lambda i, e: (e, 0, 0)),
                  pl.BlockSpec((None, D_EXPERT, D_MODEL), lambda i, e: (e, 0, 0)),
                  vec, vec],
        out_specs=row,
        out_shape=jax.ShapeDtypeStruct((n, D_MODEL), F32),
        scratch_shapes=[pltpu.VMEM((tm, D_MODEL), F32), pltpu.VMEM((tm, N_EXPERTS), F32),
                        pltpu.VMEM((tm, D_MODEL), BF16)],
        compiler_params=_params("parallel", "arbitrary"),
        name="moe_ln",
    )(x2, router_w, router_b, w13, w2, g, b)


def _proj_cd_kernel(x_ref, w_ref, cq_ref, ck_ref, cv_ref, pd_ref, ck16_ref, cv16_ref):
    xb = x_ref[...].astype(BF16)
    cq_ref[...] = _dot(xb, w_ref[:, 0:512])
    ck = _dot(xb, w_ref[:, 512:1024])
    ck_ref[...] = ck
    ck16_ref[...] = ck.astype(BF16)
    cv = _dot(xb, w_ref[:, 1024:1536])
    cv_ref[...] = cv
    cv16_ref[...] = cv.astype(BF16)
    pd_ref[...] = _dot(xb, w_ref[:, 1536:1536 + D_SHIFT_W])


def _proj_cd(x2, w_bf16, tm):
    n = x2.shape[0]
    row = lambda w: pl.BlockSpec((tm, w), lambda i: (i, 0))
    outs = [512, 512, 512, D_SHIFT_W]
    return pl.pallas_call(
        _proj_cd_kernel,
        grid=(n // tm,),
        in_specs=[row(D_MODEL), pl.BlockSpec((D_MODEL, 1536 + D_SHIFT_W), lambda i: (0, 0))],
        out_specs=[row(w) for w in outs] + [row(512), row(512)],
        out_shape=[jax.ShapeDtypeStruct((n, w), F32) for w in outs] + [jax.ShapeDtypeStruct((n, 512), BF16)] * 2,
        compiler_params=_params("parallel"),
        name="proj_cd",
    )(x2, w_bf16)


def _stick_kernel(q_ref, k_ref, v_ref, ust_ref, o_ref, *, past, tq, kb):
    i = pl.program_id(2)
    q0 = past + i * tq
    n_kb = (q0 + tq + kb - 1) // kb
    qpos = q0 + lax.broadcasted_iota(jnp.int32, (tq, 1), 0)
    lane_kb = lax.broadcasted_iota(jnp.int32, (1, kb), 1)
    n_heads = LANES // HEAD_DIM
    cols = [slice(hh * HEAD_DIM, (hh + 1) * HEAD_DIM) for hh in range(n_heads)]
    qs = [(q_ref[:, c] * (HEAD_DIM ** -0.5)).astype(BF16) for c in cols]

    def cond(c):
        j, carries, _ = c
        worst = carries[0]
        for carry in carries[1:]:
            worst = jnp.maximum(worst, carry)
        return jnp.logical_and(j >= 0, jnp.max(worst) > EXP_ZERO_BELOW)

    def body(c):
        j, carries, outs = c
        off = pl.multiple_of(j * kb, kb)
        strict = off + lane_kb < qpos
        new_carries, new_outs = [], []
        for hh in range(n_heads):
            k = k_ref[pl.ds(off, kb), cols[hh]]
            v = v_ref[pl.ds(off, kb), cols[hh]]
            z = lax.dot_general(qs[hh], k, NT_DIMS, preferred_element_type=F32)
            sp = _softplus(z)
            log_keep = jnp.where(strict, -sp, 0.0)
            later = carries[hh] + _dot_exact_rhs(log_keep, ust_ref[...])
            a = jnp.where(strict, jnp.exp((z - sp) + later), 0.0)
            new_outs.append(outs[hh] + _dot(a.astype(BF16), v))
            new_carries.append(carries[hh] + jnp.sum(log_keep, axis=1, keepdims=True))
        return j - 1, tuple(new_carries), tuple(new_outs)

    init = (n_kb - 1, tuple(jnp.zeros((tq, 1), F32) for _ in cols), tuple(jnp.zeros((tq, HEAD_DIM), F32) for _ in cols))
    _, _, outs = lax.while_loop(cond, body, init)
    for hh in range(n_heads):
        o_ref[:, cols[hh]] = outs[hh]


def _stick(cq, k_all, v_all, bsz, t_len, past, kb):
    tq = min(kb, t_len)
    nq = t_len // tq
    lp = k_all.shape[1]
    ust = jnp.tril(jnp.ones((kb, kb), F32), -1).astype(BF16)
    qrow = pl.BlockSpec((tq, LANES), lambda b, hp, i: (b * nq + i, hp))
    keys = pl.BlockSpec((None, lp, LANES), lambda b, hp, i: (b, 0, hp))
    return pl.pallas_call(
        functools.partial(_stick_kernel, past=past, tq=tq, kb=kb),
        grid=(bsz, C_HEADS * HEAD_DIM // LANES, nq),
        in_specs=[qrow, keys, keys, pl.BlockSpec((kb, kb), lambda b, hp, i: (0, 0))],
        out_specs=qrow,
        out_shape=jax.ShapeDtypeStruct((bsz * t_len, 512), F32),
        compiler_params=_params("parallel", "parallel", "arbitrary"),
        name="stick",
    )(cq, k_all, v_all, ust)


def _rwkv_pre_kernel(pd_ref, shift_ref, mu_ref, w0a0_ref, wa2_ref, gw2_ref, kk_ref_, ka_ref, hm_ref,
                     r_out, w_out, kr_out, v_out, kk_out, b_out, g_out, last_ref):
    t = pl.program_id(1)

    @pl.when(t == 0)
    def _():
        last_ref[...] = shift_ref[...]

    pd = pd_ref[...]
    tm = pd.shape[0]
    rolled = pltpu.roll(pd, 1, 0)
    first_row = lax.broadcasted_iota(jnp.int32, (tm, 1), 0) == 0
    prev = jnp.where(first_row, last_ref[...], rolled)
    last_ref[...] = pd[tm - 1:tm, :]
    pm = pd + (prev - pd) * mu_ref[...]
    r = pm[:, 0:512]
    k = pm[:, 512:1024]
    v = pm[:, 1024:1536]
    lwa = pm[:, 1536:1664]
    lg = pm[:, 1664:1792]
    lane = lax.broadcasted_iota(jnp.int32, lwa.shape, 1)
    lwa = jnp.where(lane < D_LORA_W, jnp.tanh(lwa), lwa)
    pre = w0a0_ref[...] + _dot_hi(lwa, wa2_ref[...])
    w_log = -_softplus(-pre[:, 0:512]) - 0.5
    decay = jnp.exp(-jnp.exp(w_log))
    a = _sigmoid(pre[:, 512:1024])
    g = _dot_hi(_sigmoid(lg), gw2_ref[...])
    kk = k * kk_ref_[...]
    kk = kk * lax.rsqrt(_dot_exact_rhs(kk * kk, hm_ref[...]) + 1e-12)
    r_out[...] = r
    w_out[...] = decay
    kr_out[...] = k * (1.0 + (a - 1.0) * ka_ref[...])
    v_out[...] = v
    kk_out[...] = kk
    b_out[...] = kk * a
    g_out[...] = g


def _rwkv_pre(pd, shift, mu, w0a0, wa2, gw2, k_k, k_a, bsz, t_len, tm):
    nt = t_len // tm
    row = lambda w: pl.BlockSpec((tm, w), lambda b, t: (b * nt + t, 0))
    const = lambda s: pl.BlockSpec(s, lambda b, t: (0,) * len(s))
    return pl.pallas_call(
        _rwkv_pre_kernel,
        grid=(bsz, nt),
        in_specs=[row(D_SHIFT_W), pl.BlockSpec((None, 1, D_SHIFT_W), lambda b, t: (b, 0, 0)),
                  const((1, D_SHIFT_W)), const((1, 2 * D_W)), const((LANES, 2 * D_W)), const((D_LORA_G, D_W)),
                  const((1, D_W)), const((1, D_W)), const((D_W, D_W))],
        out_specs=[row(D_W)] * 7,
        out_shape=[jax.ShapeDtypeStruct((bsz * t_len, D_W), F32)] * 7,
        scratch_shapes=[pltpu.VMEM((1, D_SHIFT_W), F32)],
        compiler_params=_params("parallel", "arbitrary"),
        name="rwkv_pre",
    )(pd, shift, mu, w0a0, wa2, gw2, k_k, k_a, _head_mask())


RWKV_VL = HEAD_DIM // 2


def _rwkv_scan_kernel(kk_ref, w_ref, b_ref, kr_ref, r_ref, v_ref, s0_ref, o_ref, s_out_ref, s_ref, *, tb):
    g = pl.program_id(0)

    @pl.when(g == 0)
    def _():
        s_ref[...] = s0_ref[...]

    def step(t, carry):
        kk = kk_ref[t]
        w = w_ref[t]
        b = b_ref[t]
        kr = kr_ref[t]
        r = r_ref[t]
        for vl in range(RWKV_VL):
            s = s_ref[vl]
            sa = jnp.sum(s * kk, axis=0, keepdims=True)
            s = s * w - sa * b + v_ref[t, vl:vl + 1, :] * kr
            s_ref[vl] = s
            o_ref[t, vl:vl + 1, :] = jnp.sum(s * r, axis=0, keepdims=True)
        return carry

    lax.fori_loop(0, tb, step, 0)

    @pl.when(g == pl.num_programs(0) - 1)
    def _():
        s_out_ref[...] = s_ref[...]


def _rwkv_scan(kk, w, b, kr, r, v, s0, t_len, tb):
    big = pl.BlockSpec((tb, HEAD_DIM, LANES), lambda g: (g, 0, 0))
    small = pl.BlockSpec((tb, RWKV_VL, LANES), lambda g: (g, 0, 0))
    state = pl.BlockSpec((RWKV_VL, HEAD_DIM, LANES), lambda g: (0, 0, 0))
    return pl.pallas_call(
        functools.partial(_rwkv_scan_kernel, tb=tb),
        grid=(t_len // tb,),
        in_specs=[big] * 5 + [small, state],
        out_specs=[small, state],
        out_shape=[jax.ShapeDtypeStruct((t_len, RWKV_VL, LANES), F32),
                   jax.ShapeDtypeStruct((RWKV_VL, HEAD_DIM, LANES), F32)],
        scratch_shapes=[pltpu.VMEM((RWKV_VL, HEAD_DIM, LANES), F32)],
        compiler_params=_params("arbitrary"),
        name="rwkv_scan",
    )(kk, w, b, kr, r, v, s0)


def _to_scan_keyed(x2, bsz, t_len):
    y = x2.reshape(bsz, t_len, D_HEADS, HEAD_DIM).transpose(1, 3, 0, 2).reshape(t_len, HEAD_DIM, bsz * D_HEADS)
    return jnp.concatenate([y, y], axis=-1)


def _to_scan_valued(x2, bsz, t_len):
    y = x2.reshape(bsz, t_len, D_HEADS, 2, RWKV_VL).transpose(1, 4, 3, 0, 2)
    return y.reshape(t_len, RWKV_VL, 2 * bsz * D_HEADS)


def _from_scan_valued(y, bsz, t_len):
    return y.reshape(t_len, RWKV_VL, 2, bsz, D_HEADS).transpose(3, 0, 4, 2, 1).reshape(bsz * t_len, D_W)


def _state_to_scan(s, bsz):
    y = s.reshape(bsz, D_HEADS, 2, RWKV_VL, HEAD_DIM).transpose(3, 4, 2, 0, 1)
    return y.reshape(RWKV_VL, HEAD_DIM, 2 * bsz * D_HEADS)


def _state_from_scan(y, bsz):
    return y.reshape(RWKV_VL, HEAD_DIM, 2, bsz, D_HEADS).transpose(3, 4, 2, 0, 1).reshape(
        bsz, D_HEADS, HEAD_DIM, HEAD_DIM)


def _pad_keys(x, mult):
    pad = (-x.shape[1]) % mult
    return x if pad == 0 else jnp.pad(x, ((0, 0), (0, pad), (0, 0)))


def _tile(n, cap):
    return min(n, cap)


def _run_group(x, past, p):
    bsz, t_len, _ = x.shape
    assert bsz * D_HEADS * 2 == LANES, "rwkv scan packs (value half, batch, head) into the lane axis"
    n = bsz * t_len
    x2 = x.reshape(n, D_MODEL)
    past_len = 0 if past is None else past[0].shape[2]
    tm = _tile(t_len, 512)

    cos, sin = _rope_tables(past_len, t_len)
    aq, iq, bqk, kik, aviw, bv, bg, kik16, v16 = _proj_ab(x2, p['w_in_ab'], cos, sin, t_len, tm)
    ak = kik[:, 0:64].reshape(1, bsz, t_len, 64)
    ik = kik[:, 64:128].reshape(1, bsz, t_len, 64)
    av = aviw[:, 0:64].reshape(1, bsz, t_len, 64)
    kik_all = kik16.reshape(bsz, t_len, LANES)
    v_all = v16.reshape(bsz, t_len, LANES)
    ones_col = (jnp.arange(LANES) == 64).astype(F32)
    if past is None:
        s_b = jnp.zeros((bsz, B_HEADS, B_DK, B_DV), F32)
    else:
        pk, pv, pik, sb = past[0][0], past[1][0], past[2][0], past[3][0]
        kik_all = jnp.concatenate([jnp.concatenate([pk, pik], axis=-1).astype(BF16), kik_all], axis=1)
        pv_slab = jnp.concatenate([pv, jnp.broadcast_to(ones_col[64:], pv.shape)], axis=-1)
        v_all = jnp.concatenate([pv_slab.astype(BF16), v_all], axis=1)
        s_b = sb
    kt = 1024
    o_a = _dsa(aq, iq, aviw, _pad_keys(kik_all, kt), _pad_keys(v_all, kt), bsz, t_len, past_len, kt)
    o_b, s_b_new = _retention(bqk, bv, bg, s_b, p['b_gn'], bsz, t_len, tm)
    x2 = _out_ln(o_a, o_b, x2, p['w_out_ab'], p['ln_g'][0, 0][None], p['ln_b'][0, 0][None], tm)
    tm_moe = _tile(n, 1024)
    x2 = _moe_ln(x2, p['router_w'], p['router_b'], p['w13'][0], p['w2'][0],
                 p['ln_g'][0, 1][None], p['ln_b'][0, 1][None], tm_moe)

    cq, ck, cv, pd, ck16, cv16 = _proj_cd(x2, p['w_in_cd'], tm)
    ck_all = ck16.reshape(bsz, t_len, 512)
    cv_all = cv16.reshape(bsz, t_len, 512)
    if past is None:
        s_d = jnp.zeros((bsz, D_HEADS, HEAD_DIM, HEAD_DIM), F32)
        shift = jnp.zeros((bsz, 1, D_SHIFT_W), F32)
    else:
        ck_all = jnp.concatenate([past[4][0].reshape(bsz, past_len, 512).astype(BF16), ck_all], axis=1)
        cv_all = jnp.concatenate([past[5][0].reshape(bsz, past_len, 512).astype(BF16), cv_all], axis=1)
        s_d, shift = past[6][0], past[7][0]
    kb = 256
    o_c = _stick(cq, _pad_keys(ck_all, kb), _pad_keys(cv_all, kb), bsz, t_len, past_len, kb)
    r, w, kr, v, kk, b, g = _rwkv_pre(pd, shift, p['d_mu'], p['d_w0a0'], p['d_wa2'], p['d_g2'],
                                      p['d_k_k'], p['d_k_a'], bsz, t_len, tm)
    keyed = [_to_scan_keyed(u, bsz, t_len) for u in (kk, w, b, kr, r)]
    o_scan, s_scan = _rwkv_scan(*keyed, _to_scan_valued(v, bsz, t_len), _state_to_scan(s_d, bsz),
                                t_len, _tile(t_len, 64))
    o_d = _from_scan_valued(o_scan, bsz, t_len)
    s_d_new = _state_from_scan(s_scan, bsz)
    x2 = _out_ln_d(o_c, o_d, r, kr, v, g, x2, p['w_out_cd'], p['d_lnx_g'], p['d_lnx_b'], p['d_r_k'],
                   p['ln_g'][1, 0][None], p['ln_b'][1, 0][None], tm)
    x2 = _moe_ln(x2, p['router_w'], p['router_b'], p['w13'][1], p['w2'][1],
                 p['ln_g'][1, 1][None], p['ln_b'][1, 1][None], tm_moe)

    states = (ak, av, ik, s_b_new[None],
              ck.reshape(1, bsz, t_len, C_HEADS, HEAD_DIM), cv.reshape(1, bsz, t_len, C_HEADS, HEAD_DIM),
              s_d_new[None], pd.reshape(bsz, t_len, D_SHIFT_W)[:, -1:][None])
    return x2.reshape(bsz, t_len, D_MODEL), states


def kernel(x_prompt, x_sample, cache_a_k, cache_a_v, cache_a_idx_k, state_b, cache_c_k, cache_c_v, state_d, state_d_shift, w_in_ab, w_out_ab, b_gn, w_in_cd, w_out_cd, d_mu, d_w0, d_w2, d_a0, d_a2, d_g2, d_k_k, d_k_a, d_r_k, d_lnx_g, d_lnx_b, ln_g, ln_b, router_w, router_b, moe_w1, moe_w3, moe_w2):
    zeros_w = jnp.zeros((D_LORA_W, D_W), F32)
    p = {
        'w_in_ab': _pack_w_in_ab(w_in_ab[0]),
        'w_out_ab': w_out_ab[0].astype(BF16),
        'b_gn': b_gn,
        'w_in_cd': w_in_cd[0].astype(BF16),
        'w_out_cd': w_out_cd[0].astype(BF16),
        'd_mu': d_mu,
        'd_w0a0': jnp.concatenate([d_w0, d_a0], axis=1),
        'd_wa2': jnp.concatenate([jnp.concatenate([d_w2[0], zeros_w], axis=1),
                                  jnp.concatenate([zeros_w, d_a2[0]], axis=1)], axis=0),
        'd_g2': d_g2[0],
        'd_k_k': d_k_k, 'd_k_a': d_k_a, 'd_r_k': d_r_k, 'd_lnx_g': d_lnx_g, 'd_lnx_b': d_lnx_b,
        'ln_g': ln_g, 'ln_b': ln_b,
        'router_w': router_w, 'router_b': router_b[None],
        'w13': jnp.concatenate([moe_w1, moe_w3], axis=-1).astype(BF16),
        'w2': moe_w2.astype(BF16),
    }
    y_p, sp = _run_group(x_prompt, None, p)
    past = (cache_a_k, cache_a_v, cache_a_idx_k, state_b, cache_c_k, cache_c_v, state_d, state_d_shift)
    y_s, ss = _run_group(x_sample, past, p)
    return (y_p, y_s, sp[0], sp[1], sp[2], ss[0], ss[1], ss[2], sp[3], ss[3], sp[4], sp[5], ss[4], ss[5],
            sp[6], ss[6], sp[7], ss[7])
```

```python
import functools
import math

import jax
import jax.numpy as jnp
import numpy as np
from jax import lax
from jax.experimental import pallas as pl
from jax.experimental.pallas import tpu as pltpu

F32 = jnp.float32
BF16 = jnp.bfloat16

D_MODEL = 1024
CHUNK = 64
DSA_QUERY_ROWS = 256
ROPE_THETA = 10000.0
HEAD_DIM = 64
LN_EPS = 1e-5
A_HEADS = 8
IDX_HEADS = 8
TOPK_MAX = 256
B_HEADS = 4
B_DK = 64
B_DV = 128
C_HEADS = 8
D_HEADS = 8
D_LORA_W = 64
D_LORA_A = 64
D_LORA_G = 128
D_GN_EPS = 64e-5
N_EXPERTS = 16
N_GROUPS = 4
EXPERTS_PER_GROUP = 4
D_EXPERT = 256
DEPTH = 2
ALPHA = (2 * DEPTH) ** 0.25
D_W = D_HEADS * HEAD_DIM
D_SHIFT_W = 3 * D_W + D_LORA_W + D_LORA_A + D_LORA_G

LANES = 128
SUBLANES = 8
VMEM_LIMIT_BYTES = 56 * 1024 * 1024

INT_MIN = -(2 ** 31)
EXP_ZERO_BELOW = -104.0

NT_DIMS = (((1,), (1,)), ((), ()))
TN_DIMS = (((0,), (0,)), ((), ()))


def _params(*sem):
    return pltpu.CompilerParams(dimension_semantics=sem, vmem_limit_bytes=VMEM_LIMIT_BYTES)


def _dot(a, b):
    return jnp.dot(a, b, preferred_element_type=F32)


def _dot_hi(a, b):
    return jnp.dot(a, b, preferred_element_type=F32, precision=lax.Precision.HIGHEST)


def _dot_f32x3(a, b, dims=(((1,), (0,)), ((), ()))):
    a_hi = a.astype(BF16)
    b_hi = b.astype(BF16)
    a_lo = (a - a_hi.astype(F32)).astype(BF16)
    b_lo = (b - b_hi.astype(F32)).astype(BF16)
    dot = lambda x, y: lax.dot_general(x, y, dims, preferred_element_type=F32)
    return dot(a_hi, b_hi) + dot(a_hi, b_lo) + dot(a_lo, b_hi)


def _split3(x):
    h1 = x.astype(BF16)
    r1 = x - h1.astype(F32)
    h2 = r1.astype(BF16)
    r2 = r1 - h2.astype(F32)
    return h1, h2, r2.astype(BF16)


def _dot_exact_rhs(x, m01):
    h1, h2, h3 = _split3(x)
    return _dot(h1, m01) + _dot(h2, m01) + _dot(h3, m01)


def _layernorm_rows(x, g, b):
    mu = jnp.mean(x, axis=-1, keepdims=True)
    xc = x - mu
    var = jnp.mean(xc * xc, axis=-1, keepdims=True)
    return xc * lax.rsqrt(var + LN_EPS) * g + b


def _sigmoid(x):
    return 1.0 / (1.0 + jnp.exp(-x))


def _softplus(x):
    return jnp.maximum(x, 0.0) + jnp.log1p(jnp.exp(-jnp.abs(x)))


def _rope_slab(x, cos, sin_signed):
    lane = lax.broadcasted_iota(jnp.int32, x.shape, 1)
    first_half = (lane % HEAD_DIM) < (HEAD_DIM // 2)
    swapped = jnp.where(first_half, pltpu.roll(x, LANES - HEAD_DIM // 2, 1), pltpu.roll(x, HEAD_DIM // 2, 1))
    return x * cos + swapped * sin_signed


AB_ROPED = 1664
AB_PACKED = 2816


def _pack_w_in_ab(w):
    aq, ak, av, iq, ik, iw, bq, bk, bv, bg = jnp.split(
        w, [512, 576, 640, 1152, 1216, 1224, 1480, 1736, 2248], axis=1)
    pad = jnp.zeros((w.shape[0], LANES - 64 - IDX_HEADS), w.dtype)
    return jnp.concatenate([aq, iq, bq, bk, ak, ik, av, iw, pad, bv, bg], axis=1).astype(BF16)


def _proj_ab_kernel(x_ref, w_ref, cos_ref, sin_ref, aq_ref, iq_ref, bqk_ref, kik_ref, aviw_ref, bv_ref, bg_ref,
                    kik16_ref, v16_ref):
    xb = x_ref[...].astype(BF16)
    cos = cos_ref[...]
    sin = sin_ref[...]

    def roped(col0, width, scale_from=None):
        y = _dot(xb, w_ref[:, col0:col0 + width])
        parts = []
        for c in range(width // LANES):
            slab = _rope_slab(y[:, c * LANES:(c + 1) * LANES], cos, sin)
            if scale_from is not None and c * LANES >= scale_from:
                slab = slab * (B_DK ** -0.5)
            parts.append(slab)
        return parts

    for c, slab in enumerate(roped(0, 512)):
        aq_ref[:, c * LANES:(c + 1) * LANES] = slab
    for c, slab in enumerate(roped(512, 512)):
        iq_ref[:, c * LANES:(c + 1) * LANES] = slab
    for c, slab in enumerate(roped(1024, 512, scale_from=256)):
        bqk_ref[:, c * LANES:(c + 1) * LANES] = slab
    kik = roped(1536, LANES)[0]
    kik_ref[...] = kik
    kik16_ref[...] = kik.astype(BF16)
    aviw = _dot(xb, w_ref[:, AB_ROPED:AB_ROPED + LANES])
    lane = lax.broadcasted_iota(jnp.int32, aviw.shape, 1)
    is_iw = (lane >= 64) & (lane < 64 + IDX_HEADS)
    aviw_ref[...] = jnp.where(is_iw, aviw * ((IDX_HEADS * HEAD_DIM) ** -0.5), aviw)
    v16_ref[...] = jnp.where(lane < 64, aviw, jnp.where(lane == 64, 1.0, 0.0)).astype(BF16)
    bv_ref[...] = _dot(xb, w_ref[:, 1792:2304])
    bg_ref[...] = _dot(xb, w_ref[:, 2304:2816])


def _proj_ab(x2, w_packed, cos, sin, t_len, tm):
    n = x2.shape[0]
    nt = t_len // tm
    row = lambda w: pl.BlockSpec((tm, w), lambda i: (i, 0))
    tab = pl.BlockSpec((tm, LANES), lambda i: (i % nt, 0))
    outs = [512, 512, 512, LANES, LANES, 512, 512]
    return pl.pallas_call(
        _proj_ab_kernel,
        grid=(n // tm,),
        in_specs=[row(D_MODEL), pl.BlockSpec((D_MODEL, AB_PACKED), lambda i: (0, 0)), tab, tab],
        out_specs=[row(w) for w in outs] + [row(LANES), row(LANES)],
        out_shape=[jax.ShapeDtypeStruct((n, w), F32) for w in outs] + [jax.ShapeDtypeStruct((n, LANES), BF16)] * 2,
        compiler_params=_params("parallel"),
        name="proj_ab",
    )(x2, w_packed, cos, sin)


def _rope_tables(past, t_len):
    half = HEAD_DIM // 2
    inv = ROPE_THETA ** (-jnp.arange(half, dtype=F32) / half)
    ang = (past + jnp.arange(t_len)).astype(F32)[:, None] * inv[None, :]
    c, s = jnp.cos(ang), jnp.sin(ang)
    return jnp.concatenate([c, c, c, c], axis=1), jnp.concatenate([-s, s, -s, s], axis=1)


def _dsa_kernel(aq_ref, iq_ref, aviw_ref, kik_ref, v_ref, tri_ref, o_ref,
                skey_ref, half_ref, iqs_ref, iwb_ref, qs_ref, p_ref, m_ref, mlane_ref, acc_ref,
                *, past, qb, kt, topk):
    i = pl.program_id(1)
    q0 = past + i * qb
    n_tiles = (q0 + qb + kt - 1) // kt
    row = lax.broadcasted_iota(jnp.int32, (qb, 1), 0)
    vis_end = ((q0 + row) // CHUNK + 1) * CHUNK
    lane_kt = lax.broadcasted_iota(jnp.int32, (1, kt), 1)

    for h in range(IDX_HEADS):
        iqs_ref[h * qb:(h + 1) * qb, :] = iq_ref[:, h * 64:(h + 1) * 64].astype(BF16)
        iwb_ref[h] = jnp.broadcast_to(aviw_ref[:, 64 + h:65 + h], (qb, LANES))

    def score_tile(j, carry):
        off = pl.multiple_of(j * kt, kt)
        ik = kik_ref[pl.ds(off, kt), 64:128]
        s = lax.dot_general(iqs_ref[...], ik, NT_DIMS, preferred_element_type=F32)
        parts = []
        for c in range(kt // LANES):
            a = jnp.zeros((qb, LANES), F32)
            for h in range(IDX_HEADS):
                a = a + iwb_ref[h] * jnp.maximum(s[h * qb:(h + 1) * qb, c * LANES:(c + 1) * LANES], 0.0)
            parts.append(a)
        acc = jnp.concatenate(parts, axis=1)
        acc = acc + 0.0
        acc = jnp.where(off + lane_kt < vis_end, acc, -jnp.inf)
        bits = pltpu.bitcast(acc, jnp.int32)
        key = jnp.where(bits < 0, bits ^ 0x7FFFFFFF, bits)
        skey_ref[:, pl.ds(off, kt)] = key
        half_ref[:, pl.ds(off, kt)] = jnp.right_shift(key, 16).astype(jnp.int16)
        return carry

    lax.fori_loop(0, n_tiles, score_tile, 0)

    def count(pred_fn):
        def body(j, cnt):
            off = pl.multiple_of(j * kt, kt)
            hit = pred_fn(skey_ref[:, pl.ds(off, kt)])
            for c in range(kt // LANES):
                cnt = cnt + jnp.where(hit[:, c * LANES:(c + 1) * LANES], 1.0, 0.0)
            return cnt
        cnt = lax.fori_loop(0, n_tiles, body, jnp.zeros((qb, LANES), F32))
        return jnp.sum(cnt, axis=1, keepdims=True)

    def count_half_ge(cand):
        cand16 = jnp.broadcast_to(cand, (qb, LANES)).astype(jnp.int16)
        one = jnp.ones((qb, LANES), jnp.int16)
        nil = jnp.zeros((qb, LANES), jnp.int16)

        def body(j, cnt):
            off = pl.multiple_of(j * kt, kt)
            tile = half_ref[:, pl.ds(off, kt)]
            for c in range(kt // LANES):
                cnt = cnt + jnp.where(tile[:, c * LANES:(c + 1) * LANES] >= cand16, one, nil)
            return cnt
        cnt = lax.fori_loop(0, n_tiles, body, nil)
        return jnp.sum(cnt.astype(F32), axis=1, keepdims=True)

    def bisect16(cnt_min, extra):
        zero = jnp.zeros((qb, 1), jnp.int32)
        cnt = extra + count_half_ge(zero)
        ok = cnt >= topk
        start = (jnp.where(ok, zero, jnp.full((qb, 1), -(2 ** 15), jnp.int32)), jnp.where(ok, cnt, cnt_min))

        def bit_step(it, c):
            t, cnt_t = c
            cand = t | jnp.left_shift(jnp.int32(1), 14 - it)
            cnt = extra + count_half_ge(cand)
            ok = cnt >= topk
            return jnp.where(ok, cand, t), jnp.where(ok, cnt, cnt_t)
        return lax.fori_loop(0, 15, bit_step, start)

    visited = jnp.full((qb, 1), n_tiles * kt, jnp.int32).astype(F32)
    t_hi, cnt_hi = bisect16(visited, 0.0)
    top16 = 2 ** 15 - 1
    above = jnp.where(t_hi == top16, 0.0, count_half_ge(jnp.minimum(t_hi + 1, top16)))

    def low_tile(j, carry):
        off = pl.multiple_of(j * kt, kt)
        key = skey_ref[:, pl.ds(off, kt)]
        low = (key & 0xFFFF) - 2 ** 15
        half_ref[:, pl.ds(off, kt)] = jnp.where(jnp.right_shift(key, 16) == t_hi, low, -(2 ** 15)).astype(jnp.int16)
        return carry

    lax.fori_loop(0, n_tiles, low_tile, 0)
    t_lo, cnt_ge = bisect16(cnt_hi, above)
    thr = jnp.left_shift(t_hi, 16) | (t_lo + 2 ** 15)
    lane_ok = lambda off: off + lane_kt < vis_end

    def select_all_ties(j, carry):
        off = pl.multiple_of(j * kt, kt)
        sel = (skey_ref[:, pl.ds(off, kt)] >= thr) & lane_ok(off)
        skey_ref[:, pl.ds(off, kt)] = pltpu.bitcast(jnp.where(sel, 0.0, -jnp.inf), jnp.int32)
        return carry

    def select_ranked_ties(need):
        def body(j, eq_seen):
            off = pl.multiple_of(j * kt, kt)
            key = skey_ref[:, pl.ds(off, kt)]
            eq = key == thr
            rank = _dot(jnp.where(eq, 1.0, 0.0).astype(BF16), tri_ref[...]) + eq_seen
            sel = ((key > thr) | (eq & (rank <= need))) & lane_ok(off)
            skey_ref[:, pl.ds(off, kt)] = pltpu.bitcast(jnp.where(sel, 0.0, -jnp.inf), jnp.int32)
            return rank[:, kt - 1:kt]
        return body

    def exact_fit():
        lax.fori_loop(0, n_tiles, select_all_ties, 0)

    def surplus_ties():
        need = topk - count(lambda x: x > thr)
        lax.fori_loop(0, n_tiles, select_ranked_ties(need), jnp.zeros((qb, 1), F32))

    lax.cond(jnp.max(jnp.abs(cnt_ge - topk)) == 0.0, exact_fit, surplus_ties)

    for h in range(A_HEADS):
        qs_ref[h * qb:(h + 1) * qb, :] = (aq_ref[:, h * 64:(h + 1) * 64] * (HEAD_DIM ** -0.5)).astype(BF16)
    mlane_ref[...] = jnp.full(mlane_ref.shape, -jnp.inf, F32)
    acc_ref[...] = jnp.zeros(acc_ref.shape, F32)

    def max_tile(j, carry):
        off = pl.multiple_of(j * kt, kt)
        bias = pltpu.bitcast(skey_ref[:, pl.ds(off, kt)], F32)
        k = kik_ref[pl.ds(off, kt), 0:64]
        s = lax.dot_general(qs_ref[...], k, NT_DIMS, preferred_element_type=F32)
        for h in range(A_HEADS):
            sh = s[h * qb:(h + 1) * qb, :] + bias
            mm = sh[:, 0:LANES]
            for c in range(1, kt // LANES):
                mm = jnp.maximum(mm, sh[:, c * LANES:(c + 1) * LANES])
            mlane_ref[h] = jnp.maximum(mlane_ref[h], mm)
        return carry

    lax.fori_loop(0, n_tiles, max_tile, 0)
    for h in range(A_HEADS):
        m_ref[h * qb:(h + 1) * qb, :] = jnp.max(mlane_ref[h], axis=1, keepdims=True)

    def pv_tile(j, carry):
        off = pl.multiple_of(j * kt, kt)
        bias = pltpu.bitcast(skey_ref[:, pl.ds(off, kt)], F32)
        k = kik_ref[pl.ds(off, kt), 0:64]
        s = lax.dot_general(qs_ref[...], k, NT_DIMS, preferred_element_type=F32)
        for h in range(A_HEADS):
            rows = slice(h * qb, (h + 1) * qb)
            p_ref[rows, :] = jnp.exp((s[rows, :] + bias) - m_ref[rows, :]).astype(BF16)
        acc_ref[...] += _dot(p_ref[...], v_ref[pl.ds(off, kt), :])
        return carry

    lax.fori_loop(0, n_tiles, pv_tile, 0)
    for h in range(A_HEADS):
        a = acc_ref[h * qb:(h + 1) * qb, :]
        o_ref[:, h * 64:(h + 1) * 64] = (a / a[:, 64:65])[:, 0:64]


def _dsa(aq, iq, aviw, kik_all, v_all, bsz, t_len, past, kt):
    qb = min(DSA_QUERY_ROWS, t_len)
    nq = t_len // qb
    lp = kik_all.shape[1]
    topk = min(TOPK_MAX, (past + t_len) // 4)
    tri = jnp.triu(jnp.ones((kt, kt), F32)).astype(BF16)
    qrow = lambda w: pl.BlockSpec((qb, w), lambda b, i: (b * nq + i, 0))
    keys = pl.BlockSpec((None, lp, LANES), lambda b, i: (b, 0, 0))
    return pl.pallas_call(
        functools.partial(_dsa_kernel, past=past, qb=qb, kt=kt, topk=topk),
        grid=(bsz, nq),
        in_specs=[qrow(512), qrow(512), qrow(LANES), keys, keys, pl.BlockSpec((kt, kt), lambda b, i: (0, 0))],
        out_specs=qrow(512),
        out_shape=jax.ShapeDtypeStruct((bsz * t_len, 512), F32),
        scratch_shapes=[
            pltpu.VMEM((qb, lp), jnp.int32),
            pltpu.VMEM((qb, lp), jnp.int16),
            pltpu.VMEM((IDX_HEADS * qb, 64), BF16),
            pltpu.VMEM((IDX_HEADS, qb, LANES), F32),
            pltpu.VMEM((A_HEADS * qb, 64), BF16),
            pltpu.VMEM((A_HEADS * qb, kt), BF16),
            pltpu.VMEM((A_HEADS * qb, 1), F32),
            pltpu.VMEM((A_HEADS, qb, LANES), F32),
            pltpu.VMEM((A_HEADS * qb, LANES), F32),
        ],
        compiler_params=_params("parallel", "arbitrary"),
        name="dsa",
    )(aq, iq, aviw, kik_all, v_all, tri)


def _retention_kernel(bqk_ref, bv_ref, bg_ref, s0_ref, gn_ref, o_ref, s_out_ref, s_ref, *, n_chunks):
    t = pl.program_id(0)

    @pl.when(t == 0)
    def _():
        s_ref[...] = s0_ref[...]

    n = CHUNK
    ri = lax.broadcasted_iota(jnp.int32, (n, n), 0).astype(F32)
    ci = lax.broadcasted_iota(jnp.int32, (n, n), 1).astype(F32)
    diff = ri - ci
    pos = lax.broadcasted_iota(jnp.int32, (n, 1), 0).astype(F32)
    decays = []
    for h in range(B_HEADS):
        log_g = math.log(1.0 - 2.0 ** (-5.0 - h))
        decays.append((jnp.where(diff >= 0, jnp.exp(jnp.maximum(diff, 0.0) * log_g), 0.0),
                       jnp.exp((pos + 1.0) * log_g), jnp.exp((n - 1.0 - pos) * log_g), math.exp(n * log_g)))
    for c in range(n_chunks):
        rows = slice(c * n, (c + 1) * n)
        for b in range(bqk_ref.shape[0]):
            for h in range(B_HEADS):
                intra, q_decay, k_decay, chunk_decay = decays[h]
                q = bqk_ref[b, rows, h * B_DK:(h + 1) * B_DK]
                k = bqk_ref[b, rows, 256 + h * B_DK:256 + (h + 1) * B_DK]
                v = bv_ref[b, rows, h * B_DV:(h + 1) * B_DV]
                s = s_ref[b, h]
                scores = _dot_f32x3(q, k, NT_DIMS) * intra
                o = _dot_f32x3(scores, v) + _dot_f32x3(q, s) * q_decay
                s_ref[b, h] = s * chunk_decay + _dot_f32x3(k * k_decay, v, TN_DIMS)
                mu = jnp.mean(o, axis=-1, keepdims=True)
                oc = o - mu
                var = jnp.mean(oc * oc, axis=-1, keepdims=True)
                g = bg_ref[b, rows, h * B_DV:(h + 1) * B_DV]
                o_ref[b, rows, h * B_DV:(h + 1) * B_DV] = (
                    oc * lax.rsqrt(var + LN_EPS) * gn_ref[:, h * B_DV:(h + 1) * B_DV] * (g * _sigmoid(g)))

    @pl.when(t == pl.num_programs(0) - 1)
    def _():
        s_out_ref[...] = s_ref[...]


def _retention(bqk, bv, bg, s0, b_gn, bsz, t_len, tt):
    row = pl.BlockSpec((bsz, tt, 512), lambda t: (0, t, 0))
    state = pl.BlockSpec((bsz, B_HEADS, B_DK, B_DV), lambda t: (0, 0, 0, 0))
    o_b, s_new = pl.pallas_call(
        functools.partial(_retention_kernel, n_chunks=tt // CHUNK),
        grid=(t_len // tt,),
        in_specs=[row, row, row, state, pl.BlockSpec((1, 512), lambda t: (0, 0))],
        out_specs=[row, state],
        out_shape=[jax.ShapeDtypeStruct((bsz, t_len, 512), F32),
                   jax.ShapeDtypeStruct((bsz, B_HEADS, B_DK, B_DV), F32)],
        scratch_shapes=[pltpu.VMEM((bsz, B_HEADS, B_DK, B_DV), F32)],
        compiler_params=_params("arbitrary"),
        name="retention",
    )(bqk.reshape(bsz, t_len, 512), bv.reshape(bsz, t_len, 512), bg.reshape(bsz, t_len, 512), s0, b_gn)
    return o_b.reshape(bsz * t_len, 512), s_new


def _out_ln_kernel(oa_ref, ob_ref, x_ref, w_ref, g_ref, b_ref, y_ref):
    y = _dot(oa_ref[...].astype(BF16), w_ref[0:512, :]) + _dot(ob_ref[...].astype(BF16), w_ref[512:1024, :])
    y_ref[...] = _layernorm_rows(ALPHA * x_ref[...] + y, g_ref[...], b_ref[...])


def _out_ln(oa, ob, x2, w_out, g, b, tm):
    n = x2.shape[0]
    row = lambda w: pl.BlockSpec((tm, w), lambda i: (i, 0))
    vec = pl.BlockSpec((1, D_MODEL), lambda i: (0, 0))
    return pl.pallas_call(
        _out_ln_kernel,
        grid=(n // tm,),
        in_specs=[row(512), row(512), row(D_MODEL), pl.BlockSpec((D_MODEL, D_MODEL), lambda i: (0, 0)), vec, vec],
        out_specs=row(D_MODEL),
        out_shape=jax.ShapeDtypeStruct((n, D_MODEL), F32),
        compiler_params=_params("parallel"),
        name="out_ln",
    )(oa, ob, x2, w_out, g, b)


def _out_ln_d_kernel(oc_ref, od_ref, r_ref, kr_ref, v_ref, gate_ref, x_ref, w_ref, hm_ref,
                     lnx_g_ref, lnx_b_ref, rk_ref, g_ref, b_ref, y_ref):
    hm = hm_ref[...]
    o = od_ref[...]
    mu = _dot_exact_rhs(o, hm) * (1.0 / HEAD_DIM)
    oc = o - mu
    var = _dot_exact_rhs(oc * oc, hm) * (1.0 / HEAD_DIM)
    normed = oc * lax.rsqrt(var + D_GN_EPS) * lnx_g_ref[...] + lnx_b_ref[...]
    v = v_ref[...]
    bonus = _dot_exact_rhs(r_ref[...] * kr_ref[...] * rk_ref[...], hm) * v
    od = (normed + bonus) * gate_ref[...]
    y = _dot(oc_ref[...].astype(BF16), w_ref[0:512, :]) + _dot(od.astype(BF16), w_ref[512:1024, :])
    y_ref[...] = _layernorm_rows(ALPHA * x_ref[...] + y, g_ref[...], b_ref[...])


def _head_mask():
    head = jnp.arange(D_W) // HEAD_DIM
    return (head[:, None] == head[None, :]).astype(BF16)


def _out_ln_d(oc, od, r, kr, v, gate, x2, w_out, lnx_g, lnx_b, r_k, g, b, tm):
    n = x2.shape[0]
    row = lambda w: pl.BlockSpec((tm, w), lambda i: (i, 0))
    vec = lambda w: pl.BlockSpec((1, w), lambda i: (0, 0))
    return pl.pallas_call(
        _out_ln_d_kernel,
        grid=(n // tm,),
        in_specs=[row(512)] * 6 + [row(D_MODEL), pl.BlockSpec((D_MODEL, D_MODEL), lambda i: (0, 0)),
                                   pl.BlockSpec((D_W, D_W), lambda i: (0, 0)),
                                   vec(D_W), vec(D_W), vec(D_W), vec(D_MODEL), vec(D_MODEL)],
        out_specs=row(D_MODEL),
        out_shape=jax.ShapeDtypeStruct((n, D_MODEL), F32),
        compiler_params=_params("parallel"),
        name="out_ln_d",
    )(oc, od, r, kr, v, gate, x2, w_out, _head_mask(), lnx_g, lnx_b, r_k, g, b)


MOE_EXPERTS_PER_STEP = 4


def _moe_kernel(x_ref, rw_ref, rb_ref, w13_ref, w2_ref, g_ref, b_ref, y_ref, acc_ref, gate_ref, xb_ref):
    e = pl.program_id(1)
    tm = x_ref.shape[0]
    lane = lax.broadcasted_iota(jnp.int32, (tm, N_EXPERTS), 1)

    lane_f = lane.astype(F32)

    def first_argmax(v):
        m = jnp.max(v, axis=1, keepdims=True)
        idx = jnp.min(jnp.where(v == m, lane_f, float(N_EXPERTS)), axis=1, keepdims=True)
        return m, idx.astype(jnp.int32)

    @pl.when(e == 0)
    def _():
        x = x_ref[...]
        xb_ref[...] = x.astype(BF16)
        aff = _sigmoid(_dot_hi(x, rw_ref[...]))
        biased = aff + rb_ref[...]
        best = jnp.zeros((tm, 1), jnp.int32)
        best_score = jnp.full((tm, 1), -jnp.inf, F32)
        for grp in range(N_GROUPS):
            vg = jnp.where(lane // EXPERTS_PER_GROUP == grp, biased, -jnp.inf)
            top1, idx1 = first_argmax(vg)
            top2, _ = first_argmax(jnp.where(lane == idx1, -jnp.inf, vg))
            score = top1 + top2
            better = score > best_score
            best = jnp.where(better, grp, best)
            best_score = jnp.where(better, score, best_score)
        masked = jnp.where(lane // EXPERTS_PER_GROUP == best, biased, -jnp.inf)
        _, idx1 = first_argmax(masked)
        _, idx2 = first_argmax(jnp.where(lane == idx1, -jnp.inf, masked))
        top_aff = jnp.where((lane == idx1) | (lane == idx2), aff, 0.0)
        gate_ref[...] = top_aff / jnp.sum(top_aff, axis=1, keepdims=True)
        acc_ref[...] = jnp.zeros(acc_ref.shape, F32)

    for k in range(MOE_EXPERTS_PER_STEP):
        h13 = _dot(xb_ref[...], w13_ref[k])
        h1 = h13[:, 0:D_EXPERT]
        h = (h1 * _sigmoid(h1)) * h13[:, D_EXPERT:2 * D_EXPERT]
        gate_e = jnp.sum(jnp.where(lane == e * MOE_EXPERTS_PER_STEP + k, gate_ref[...], 0.0), axis=1, keepdims=True)
        acc_ref[...] += gate_e * _dot(h.astype(BF16), w2_ref[k])

    @pl.when(e == pl.num_programs(1) - 1)
    def _():
        y_ref[...] = _layernorm_rows(ALPHA * x_ref[...] + acc_ref[...], g_ref[...], b_ref[...])


def _moe_ln(x2, router_w, router_b, w13, w2, g, b, tm):
    n = x2.shape[0]
    row = pl.BlockSpec((tm, D_MODEL), lambda i, e: (i, 0))
    vec = pl.BlockSpec((1, D_MODEL), lambda i, e: (0, 0))
    return pl.pallas_call(
        _moe_kernel,
        grid=(n // tm, N_EXPERTS // MOE_EXPERTS_PER_STEP),
        in_specs=[row,
                  pl.BlockSpec((D_MODEL, N_EXPERTS), lambda i, e: (0, 0)),
                  pl.BlockSpec((1, N_EXPERTS), lambda i, e: (0, 0)),
                  pl.BlockSpec((MOE_EXPERTS_PER_STEP, D_MODEL, 2 * D_EXPERT), lambda i, e: (e, 0, 0)),
                  pl.BlockSpec((MOE_EXPERTS_PER_STEP, D_EXPERT, D_MODEL), lambda i, e: (e, 0, 0)),
                  vec, vec],
        out_specs=row,
        out_shape=jax.ShapeDtypeStruct((n, D_MODEL), F32),
        scratch_shapes=[pltpu.VMEM((tm, D_MODEL), F32), pltpu.VMEM((tm, N_EXPERTS), F32),
                        pltpu.VMEM((tm, D_MODEL), BF16)],
        compiler_params=_params("parallel", "arbitrary"),
        name="moe_ln",
    )(x2, router_w, router_b, w13, w2, g, b)


def _proj_cd_kernel(x_ref, w_ref, cq_ref, ck_ref, cv_ref, pd_ref, ck16_ref, cv16_ref):
    xb = x_ref[...].astype(BF16)
    cq_ref[...] = _dot(xb, w_ref[:, 0:512])
    ck = _dot(xb, w_ref[:, 512:1024])
    ck_ref[...] = ck
    ck16_ref[...] = ck.astype(BF16)
    cv = _dot(xb, w_ref[:, 1024:1536])
    cv_ref[...] = cv
    cv16_ref[...] = cv.astype(BF16)
    pd_ref[...] = _dot(xb, w_ref[:, 1536:1536 + D_SHIFT_W])


def _proj_cd(x2, w_bf16, tm):
    n = x2.shape[0]
    row = lambda w: pl.BlockSpec((tm, w), lambda i: (i, 0))
    outs = [512, 512, 512, D_SHIFT_W]
    return pl.pallas_call(
        _proj_cd_kernel,
        grid=(n // tm,),
        in_specs=[row(D_MODEL), pl.BlockSpec((D_MODEL, 1536 + D_SHIFT_W), lambda i: (0, 0))],
        out_specs=[row(w) for w in outs] + [row(512), row(512)],
        out_shape=[jax.ShapeDtypeStruct((n, w), F32) for w in outs] + [jax.ShapeDtypeStruct((n, 512), BF16)] * 2,
        compiler_params=_params("parallel"),
        name="proj_cd",
    )(x2, w_bf16)


def _stick_kernel(q_ref, k_ref, v_ref, ust_ref, o_ref, *, past, tq, kb):
    i = pl.program_id(2)
    q0 = past + i * tq
    n_kb = (q0 + tq + kb - 1) // kb
    qpos = q0 + lax.broadcasted_iota(jnp.int32, (tq, 1), 0)
    lane_kb = lax.broadcasted_iota(jnp.int32, (1, kb), 1)
    n_heads = LANES // HEAD_DIM
    cols = [slice(hh * HEAD_DIM, (hh + 1) * HEAD_DIM) for hh in range(n_heads)]
    qs = [(q_ref[:, c] * (HEAD_DIM ** -0.5)).astype(BF16) for c in cols]

    def cond(c):
        j, carries, _ = c
        worst = carries[0]
        for carry in carries[1:]:
            worst = jnp.maximum(worst, carry)
        return jnp.logical_and(j >= 0, jnp.max(worst) > EXP_ZERO_BELOW)

    def body(c):
        j, carries, outs = c
        off = pl.multiple_of(j * kb, kb)
        strict = off + lane_kb < qpos
        new_carries, new_outs = [], []
        for hh in range(n_heads):
            k = k_ref[pl.ds(off, kb), cols[hh]]
            v = v_ref[pl.ds(off, kb), cols[hh]]
            z = lax.dot_general(qs[hh], k, NT_DIMS, preferred_element_type=F32)
            sp = _softplus(z)
            log_keep = jnp.where(strict, -sp, 0.0)
            later = carries[hh] + _dot_exact_rhs(log_keep, ust_ref[...])
            a = jnp.where(strict, jnp.exp((z - sp) + later), 0.0)
            new_outs.append(outs[hh] + _dot(a.astype(BF16), v))
            new_carries.append(carries[hh] + jnp.sum(log_keep, axis=1, keepdims=True))
        return j - 1, tuple(new_carries), tuple(new_outs)

    init = (n_kb - 1, tuple(jnp.zeros((tq, 1), F32) for _ in cols), tuple(jnp.zeros((tq, HEAD_DIM), F32) for _ in cols))
    _, _, outs = lax.while_loop(cond, body, init)
    for hh in range(n_heads):
        o_ref[:, cols[hh]] = outs[hh]


def _stick(cq, k_all, v_all, bsz, t_len, past, kb):
    tq = min(kb, t_len)
    nq = t_len // tq
    lp = k_all.shape[1]
    ust = jnp.tril(jnp.ones((kb, kb), F32), -1).astype(BF16)
    qrow = pl.BlockSpec((tq, LANES), lambda b, hp, i: (b * nq + i, hp))
    keys = pl.BlockSpec((None, lp, LANES), lambda b, hp, i: (b, 0, hp))
    return pl.pallas_call(
        functools.partial(_stick_kernel, past=past, tq=tq, kb=kb),
        grid=(bsz, C_HEADS * HEAD_DIM // LANES, nq),
        in_specs=[qrow, keys, keys, pl.BlockSpec((kb, kb), lambda b, hp, i: (0, 0))],
        out_specs=qrow,
        out_shape=jax.ShapeDtypeStruct((bsz * t_len, 512), F32),
        compiler_params=_params("parallel", "parallel", "arbitrary"),
        name="stick",
    )(cq, k_all, v_all, ust)


def _rwkv_pre_kernel(pd_ref, shift_ref, mu_ref, w0a0_ref, wa2_ref, gw2_ref, kk_ref_, ka_ref, hm_ref,
                     r_out, w_out, kr_out, v_out, kk_out, b_out, g_out, last_ref):
    t = pl.program_id(1)

    @pl.when(t == 0)
    def _():
        last_ref[...] = shift_ref[...]

    pd = pd_ref[...]
    tm = pd.shape[0]
    rolled = pltpu.roll(pd, 1, 0)
    first_row = lax.broadcasted_iota(jnp.int32, (tm, 1), 0) == 0
    prev = jnp.where(first_row, last_ref[...], rolled)
    last_ref[...] = pd[tm - 1:tm, :]
    pm = pd + (prev - pd) * mu_ref[...]
    r = pm[:, 0:512]
    k = pm[:, 512:1024]
    v = pm[:, 1024:1536]
    lwa = pm[:, 1536:1664]
    lg = pm[:, 1664:1792]
    lane = lax.broadcasted_iota(jnp.int32, lwa.shape, 1)
    lwa = jnp.where(lane < D_LORA_W, jnp.tanh(lwa), lwa)
    pre = w0a0_ref[...] + _dot_hi(lwa, wa2_ref[...])
    w_log = -_softplus(-pre[:, 0:512]) - 0.5
    decay = jnp.exp(-jnp.exp(w_log))
    a = _sigmoid(pre[:, 512:1024])
    g = _dot_hi(_sigmoid(lg), gw2_ref[...])
    kk = k * kk_ref_[...]
    kk = kk * lax.rsqrt(_dot_exact_rhs(kk * kk, hm_ref[...]) + 1e-12)
    r_out[...] = r
    w_out[...] = decay
    kr_out[...] = k * (1.0 + (a - 1.0) * ka_ref[...])
    v_out[...] = v
    kk_out[...] = kk
    b_out[...] = kk * a
    g_out[...] = g


def _rwkv_pre(pd, shift, mu, w0a0, wa2, gw2, k_k, k_a, bsz, t_len, tm):
    nt = t_len // tm
    row = lambda w: pl.BlockSpec((tm, w), lambda b, t: (b * nt + t, 0))
    const = lambda s: pl.BlockSpec(s, lambda b, t: (0,) * len(s))
    return pl.pallas_call(
        _rwkv_pre_kernel,
        grid=(bsz, nt),
        in_specs=[row(D_SHIFT_W), pl.BlockSpec((None, 1, D_SHIFT_W), lambda b, t: (b, 0, 0)),
                  const((1, D_SHIFT_W)), const((1, 2 * D_W)), const((LANES, 2 * D_W)), const((D_LORA_G, D_W)),
                  const((1, D_W)), const((1, D_W)), const((D_W, D_W))],
        out_specs=[row(D_W)] * 7,
        out_shape=[jax.ShapeDtypeStruct((bsz * t_len, D_W), F32)] * 7,
        scratch_shapes=[pltpu.VMEM((1, D_SHIFT_W), F32)],
        compiler_params=_params("parallel", "arbitrary"),
        name="rwkv_pre",
    )(pd, shift, mu, w0a0, wa2, gw2, k_k, k_a, _head_mask())


RWKV_VL = HEAD_DIM // 2


def _rwkv_scan_kernel(kk_ref, w_ref, b_ref, kr_ref, r_ref, v_ref, s0_ref, o_ref, s_out_ref, s_ref, *, tb):
    g = pl.program_id(0)

    @pl.when(g == 0)
    def _():
        s_ref[...] = s0_ref[...]

    def step(t, carry):
        kk = kk_ref[t]
        w = w_ref[t]
        b = b_ref[t]
        kr = kr_ref[t]
        r = r_ref[t]
        for vl in range(RWKV_VL):
            s = s_ref[vl]
            sa = jnp.sum(s * kk, axis=0, keepdims=True)
            s = s * w - sa * b + v_ref[t, vl:vl + 1, :] * kr
            s_ref[vl] = s
            o_ref[t, vl:vl + 1, :] = jnp.sum(s * r, axis=0, keepdims=True)
        return carry

    lax.fori_loop(0, tb, step, 0)

    @pl.when(g == pl.num_programs(0) - 1)
    def _():
        s_out_ref[...] = s_ref[...]


def _rwkv_scan(kk, w, b, kr, r, v, s0, t_len, tb):
    big = pl.BlockSpec((tb, HEAD_DIM, LANES), lambda g: (g, 0, 0))
    small = pl.BlockSpec((tb, RWKV_VL, LANES), lambda g: (g, 0, 0))
    state = pl.BlockSpec((RWKV_VL, HEAD_DIM, LANES), lambda g: (0, 0, 0))
    return pl.pallas_call(
        functools.partial(_rwkv_scan_kernel, tb=tb),
        grid=(t_len // tb,),
        in_specs=[big] * 5 + [small, state],
        out_specs=[small, state],
        out_shape=[jax.ShapeDtypeStruct((t_len, RWKV_VL, LANES), F32),
                   jax.ShapeDtypeStruct((RWKV_VL, HEAD_DIM, LANES), F32)],
        scratch_shapes=[pltpu.VMEM((RWKV_VL, HEAD_DIM, LANES), F32)],
        compiler_params=_params("arbitrary"),
        name="rwkv_scan",
    )(kk, w, b, kr, r, v, s0)


def _to_scan_keyed(x2, bsz, t_len):
    y = x2.reshape(bsz, t_len, D_HEADS, HEAD_DIM).transpose(1, 3, 0, 2).reshape(t_len, HEAD_DIM, bsz * D_HEADS)
    return jnp.concatenate([y, y], axis=-1)


def _to_scan_valued(x2, bsz, t_len):
    y = x2.reshape(bsz, t_len, D_HEADS, 2, RWKV_VL).transpose(1, 4, 3, 0, 2)
    return y.reshape(t_len, RWKV_VL, 2 * bsz * D_HEADS)


def _from_scan_valued(y, bsz, t_len):
    return y.reshape(t_len, RWKV_VL, 2, bsz, D_HEADS).transpose(3, 0, 4, 2, 1).reshape(bsz * t_len, D_W)


def _state_to_scan(s, bsz):
    y = s.reshape(bsz, D_HEADS, 2, RWKV_VL, HEAD_DIM).transpose(3, 4, 2, 0, 1)
    return y.reshape(RWKV_VL, HEAD_DIM, 2 * bsz * D_HEADS)


def _state_from_scan(y, bsz):
    return y.reshape(RWKV_VL, HEAD_DIM, 2, bsz, D_HEADS).transpose(3, 4, 2, 0, 1).reshape(
        bsz, D_HEADS, HEAD_DIM, HEAD_DIM)


def _pad_keys(x, mult):
    pad = (-x.shape[1]) % mult
    return x if pad == 0 else jnp.pad(x, ((0, 0), (0, pad), (0, 0)))


def _tile(n, cap):
    return min(n, cap)


def _run_group(x, past, p):
    bsz, t_len, _ = x.shape
    assert bsz * D_HEADS * 2 == LANES, "rwkv scan packs (value half, batch, head) into the lane axis"
    n = bsz * t_len
    x2 = x.reshape(n, D_MODEL)
    past_len = 0 if past is None else past[0].shape[2]
    tm = _tile(t_len, 512)

    cos, sin = _rope_tables(past_len, t_len)
    aq, iq, bqk, kik, aviw, bv, bg, kik16, v16 = _proj_ab(x2, p['w_in_ab'], cos, sin, t_len, tm)
    ak = kik[:, 0:64].reshape(1, bsz, t_len, 64)
    ik = kik[:, 64:128].reshape(1, bsz, t_len, 64)
    av = aviw[:, 0:64].reshape(1, bsz, t_len, 64)
    kik_all = kik16.reshape(bsz, t_len, LANES)
    v_all = v16.reshape(bsz, t_len, LANES)
    ones_col = (jnp.arange(LANES) == 64).astype(F32)
    if past is None:
        s_b = jnp.zeros((bsz, B_HEADS, B_DK, B_DV), F32)
    else:
        pk, pv, pik, sb = past[0][0], past[1][0], past[2][0], past[3][0]
        kik_all = jnp.concatenate([jnp.concatenate([pk, pik], axis=-1).astype(BF16), kik_all], axis=1)
        pv_slab = jnp.concatenate([pv, jnp.broadcast_to(ones_col[64:], pv.shape)], axis=-1)
        v_all = jnp.concatenate([pv_slab.astype(BF16), v_all], axis=1)
        s_b = sb
    kt = 512
    o_a = _dsa(aq, iq, aviw, _pad_keys(kik_all, kt), _pad_keys(v_all, kt), bsz, t_len, past_len, kt)
    o_b, s_b_new = _retention(bqk, bv, bg, s_b, p['b_gn'], bsz, t_len, _tile(t_len, 2 * CHUNK))
    x2 = _out_ln(o_a, o_b, x2, p['w_out_ab'], p['ln_g'][0, 0][None], p['ln_b'][0, 0][None], tm)
    tm_moe = _tile(n, 1024)
    x2 = _moe_ln(x2, p['router_w'], p['router_b'], p['w13'][0], p['w2'][0],
                 p['ln_g'][0, 1][None], p['ln_b'][0, 1][None], tm_moe)

    cq, ck, cv, pd, ck16, cv16 = _proj_cd(x2, p['w_in_cd'], tm)
    ck_all = ck16.reshape(bsz, t_len, 512)
    cv_all = cv16.reshape(bsz, t_len, 512)
    if past is None:
        s_d = jnp.zeros((bsz, D_HEADS, HEAD_DIM, HEAD_DIM), F32)
        shift = jnp.zeros((bsz, 1, D_SHIFT_W), F32)
    else:
        ck_all = jnp.concatenate([past[4][0].reshape(bsz, past_len, 512).astype(BF16), ck_all], axis=1)
        cv_all = jnp.concatenate([past[5][0].reshape(bsz, past_len, 512).astype(BF16), cv_all], axis=1)
        s_d, shift = past[6][0], past[7][0]
    kb = 256
    o_c = _stick(cq, _pad_keys(ck_all, kb), _pad_keys(cv_all, kb), bsz, t_len, past_len, kb)
    r, w, kr, v, kk, b, g = _rwkv_pre(pd, shift, p['d_mu'], p['d_w0a0'], p['d_wa2'], p['d_g2'],
                                      p['d_k_k'], p['d_k_a'], bsz, t_len, tm)
    keyed = [_to_scan_keyed(u, bsz, t_len) for u in (kk, w, b, kr, r)]
    o_scan, s_scan = _rwkv_scan(*keyed, _to_scan_valued(v, bsz, t_len), _state_to_scan(s_d, bsz),
                                t_len, _tile(t_len, 64))
    o_d = _from_scan_valued(o_scan, bsz, t_len)
    s_d_new = _state_from_scan(s_scan, bsz)
    x2 = _out_ln_d(o_c, o_d, r, kr, v, g, x2, p['w_out_cd'], p['d_lnx_g'], p['d_lnx_b'], p['d_r_k'],
                   p['ln_g'][1, 0][None], p['ln_b'][1, 0][None], tm)
    x2 = _moe_ln(x2, p['router_w'], p['router_b'], p['w13'][1], p['w2'][1],
                 p['ln_g'][1, 1][None], p['ln_b'][1, 1][None], tm_moe)

    states = (ak, av, ik, s_b_new[None],
              ck.reshape(1, bsz, t_len, C_HEADS, HEAD_DIM), cv.reshape(1, bsz, t_len, C_HEADS, HEAD_DIM),
              s_d_new[None], pd.reshape(bsz, t_len, D_SHIFT_W)[:, -1:][None])
    return x2.reshape(bsz, t_len, D_MODEL), states


def kernel(x_prompt, x_sample, cache_a_k, cache_a_v, cache_a_idx_k, state_b, cache_c_k, cache_c_v, state_d, state_d_shift, w_in_ab, w_out_ab, b_gn, w_in_cd, w_out_cd, d_mu, d_w0, d_w2, d_a0, d_a2, d_g2, d_k_k, d_k_a, d_r_k, d_lnx_g, d_lnx_b, ln_g, ln_b, router_w, router_b, moe_w1, moe_w3, moe_w2):
    zeros_w = jnp.zeros((D_LORA_W, D_W), F32)
    p = {
        'w_in_ab': _pack_w_in_ab(w_in_ab[0]),
        'w_out_ab': w_out_ab[0].astype(BF16),
        'b_gn': b_gn,
        'w_in_cd': w_in_cd[0].astype(BF16),
        'w_out_cd': w_out_cd[0].astype(BF16),
        'd_mu': d_mu,
        'd_w0a0': jnp.concatenate([d_w0, d_a0], axis=1),
        'd_wa2': jnp.concatenate([jnp.concatenate([d_w2[0], zeros_w], axis=1),
                                  jnp.concatenate([zeros_w, d_a2[0]], axis=1)], axis=0),
        'd_g2': d_g2[0],
        'd_k_k': d_k_k, 'd_k_a': d_k_a, 'd_r_k': d_r_k, 'd_lnx_g': d_lnx_g, 'd_lnx_b': d_lnx_b,
        'ln_g': ln_g, 'ln_b': ln_b,
        'router_w': router_w, 'router_b': router_b[None],
        'w13': jnp.concatenate([moe_w1, moe_w3], axis=-1).astype(BF16),
        'w2': moe_w2.astype(BF16),
    }
    y_p, sp = _run_group(x_prompt, None, p)
    past = (cache_a_k, cache_a_v, cache_a_idx_k, state_b, cache_c_k, cache_c_v, state_d, state_d_shift)
    y_s, ss = _run_group(x_sample, past, p)
    return (y_p, y_s, sp[0], sp[1], sp[2], ss[0], ss[1], ss[2], sp[3], ss[3], sp[4], sp[5], ss[4], ss[5],
            sp[6], ss[6], sp[7], ss[7])
```

```python
import functools
import math

import jax
import jax.numpy as jnp
import numpy as np
from jax import lax
from jax.experimental import pallas as pl
from jax.experimental.pallas import tpu as pltpu

F32 = jnp.float32
BF16 = jnp.bfloat16

D_MODEL = 1024
CHUNK = 64
DSA_QUERY_ROWS = 128
ROPE_THETA = 10000.0
HEAD_DIM = 64
LN_EPS = 1e-5
A_HEADS = 8
IDX_HEADS = 8
TOPK_MAX = 256
B_HEADS = 4
B_DK = 64
B_DV = 128
C_HEADS = 8
D_HEADS = 8
D_LORA_W = 64
D_LORA_A = 64
D_LORA_G = 128
D_GN_EPS = 64e-5
N_EXPERTS = 16
N_GROUPS = 4
EXPERTS_PER_GROUP = 4
D_EXPERT = 256
DEPTH = 2
ALPHA = (2 * DEPTH) ** 0.25
D_W = D_HEADS * HEAD_DIM
D_SHIFT_W = 3 * D_W + D_LORA_W + D_LORA_A + D_LORA_G

LANES = 128
SUBLANES = 8
VMEM_LIMIT_BYTES = 56 * 1024 * 1024

INT_MIN = -(2 ** 31)
EXP_ZERO_BELOW = -104.0

NT_DIMS = (((1,), (1,)), ((), ()))
TN_DIMS = (((0,), (0,)), ((), ()))


def _params(*sem):
    return pltpu.CompilerParams(dimension_semantics=sem, vmem_limit_bytes=VMEM_LIMIT_BYTES)


def _dot(a, b):
    return jnp.dot(a, b, preferred_element_type=F32)


def _dot_hi(a, b):
    return jnp.dot(a, b, preferred_element_type=F32, precision=lax.Precision.HIGHEST)


def _dot_f32x3(a, b, dims=(((1,), (0,)), ((), ()))):
    a_hi = a.astype(BF16)
    b_hi = b.astype(BF16)
    a_lo = (a - a_hi.astype(F32)).astype(BF16)
    b_lo = (b - b_hi.astype(F32)).astype(BF16)
    dot = lambda x, y: lax.dot_general(x, y, dims, preferred_element_type=F32)
    return dot(a_hi, b_hi) + dot(a_hi, b_lo) + dot(a_lo, b_hi)


def _split3(x):
    h1 = x.astype(BF16)
    r1 = x - h1.astype(F32)
    h2 = r1.astype(BF16)
    r2 = r1 - h2.astype(F32)
    return h1, h2, r2.astype(BF16)


def _dot_exact_rhs(x, m01):
    h1, h2, h3 = _split3(x)
    return _dot(h1, m01) + _dot(h2, m01) + _dot(h3, m01)


def _layernorm_rows(x, g, b):
    mu = jnp.mean(x, axis=-1, keepdims=True)
    xc = x - mu
    var = jnp.mean(xc * xc, axis=-1, keepdims=True)
    return xc * lax.rsqrt(var + LN_EPS) * g + b


def _sigmoid(x):
    return 1.0 / (1.0 + jnp.exp(-x))


def _softplus(x):
    return jnp.maximum(x, 0.0) + jnp.log1p(jnp.exp(-jnp.abs(x)))


def _rope_slab(x, cos, sin_signed):
    lane = lax.broadcasted_iota(jnp.int32, x.shape, 1)
    first_half = (lane % HEAD_DIM) < (HEAD_DIM // 2)
    swapped = jnp.where(first_half, pltpu.roll(x, LANES - HEAD_DIM // 2, 1), pltpu.roll(x, HEAD_DIM // 2, 1))
    return x * cos + swapped * sin_signed


AB_ROPED = 1664
AB_PACKED = 2816


def _pack_w_in_ab(w):
    aq, ak, av, iq, ik, iw, bq, bk, bv, bg = jnp.split(
        w, [512, 576, 640, 1152, 1216, 1224, 1480, 1736, 2248], axis=1)
    pad = jnp.zeros((w.shape[0], LANES - 64 - IDX_HEADS), w.dtype)
    return jnp.concatenate([aq, iq, bq, bk, ak, ik, av, iw, pad, bv, bg], axis=1).astype(BF16)


def _proj_ab_kernel(x_ref, w_ref, cos_ref, sin_ref, aq_ref, iq_ref, bqk_ref, kik_ref, aviw_ref, bv_ref, bg_ref,
                    kik16_ref, v16_ref):
    xb = x_ref[...].astype(BF16)
    cos = cos_ref[...]
    sin = sin_ref[...]

    def roped(col0, width, scale_from=None):
        y = _dot(xb, w_ref[:, col0:col0 + width])
        parts = []
        for c in range(width // LANES):
            slab = _rope_slab(y[:, c * LANES:(c + 1) * LANES], cos, sin)
            if scale_from is not None and c * LANES >= scale_from:
                slab = slab * (B_DK ** -0.5)
            parts.append(slab)
        return parts

    for c, slab in enumerate(roped(0, 512)):
        aq_ref[:, c * LANES:(c + 1) * LANES] = slab
    for c, slab in enumerate(roped(512, 512)):
        iq_ref[:, c * LANES:(c + 1) * LANES] = slab
    for c, slab in enumerate(roped(1024, 512, scale_from=256)):
        bqk_ref[:, c * LANES:(c + 1) * LANES] = slab
    kik = roped(1536, LANES)[0]
    kik_ref[...] = kik
    kik16_ref[...] = kik.astype(BF16)
    aviw = _dot(xb, w_ref[:, AB_ROPED:AB_ROPED + LANES])
    lane = lax.broadcasted_iota(jnp.int32, aviw.shape, 1)
    is_iw = (lane >= 64) & (lane < 64 + IDX_HEADS)
    aviw_ref[...] = jnp.where(is_iw, aviw * ((IDX_HEADS * HEAD_DIM) ** -0.5), aviw)
    v16_ref[...] = jnp.where(lane < 64, aviw, jnp.where(lane == 64, 1.0, 0.0)).astype(BF16)
    bv_ref[...] = _dot(xb, w_ref[:, 1792:2304])
    bg_ref[...] = _dot(xb, w_ref[:, 2304:2816])


def _proj_ab(x2, w_packed, cos, sin, t_len, tm):
    n = x2.shape[0]
    nt = t_len // tm
    row = lambda w: pl.BlockSpec((tm, w), lambda i: (i, 0))
    tab = pl.BlockSpec((tm, LANES), lambda i: (i % nt, 0))
    outs = [512, 512, 512, LANES, LANES, 512, 512]
    return pl.pallas_call(
        _proj_ab_kernel,
        grid=(n // tm,),
        in_specs=[row(D_MODEL), pl.BlockSpec((D_MODEL, AB_PACKED), lambda i: (0, 0)), tab, tab],
        out_specs=[row(w) for w in outs] + [row(LANES), row(LANES)],
        out_shape=[jax.ShapeDtypeStruct((n, w), F32) for w in outs] + [jax.ShapeDtypeStruct((n, LANES), BF16)] * 2,
        compiler_params=_params("parallel"),
        name="proj_ab",
    )(x2, w_packed, cos, sin)


def _rope_tables(past, t_len):
    half = HEAD_DIM // 2
    inv = ROPE_THETA ** (-jnp.arange(half, dtype=F32) / half)
    ang = (past + jnp.arange(t_len)).astype(F32)[:, None] * inv[None, :]
    c, s = jnp.cos(ang), jnp.sin(ang)
    return jnp.concatenate([c, c, c, c], axis=1), jnp.concatenate([-s, s, -s, s], axis=1)


def _dsa_kernel(aq_ref, iq_ref, aviw_ref, kik_ref, v_ref, tri_ref, o_ref,
                skey_ref, half_ref, iqs_ref, iwb_ref, qs_ref, p_ref, m_ref, mlane_ref, acc_ref,
                *, past, qb, kt, topk):
    i = pl.program_id(1)
    q0 = past + i * qb
    n_tiles = (q0 + qb + kt - 1) // kt
    row = lax.broadcasted_iota(jnp.int32, (qb, 1), 0)
    vis_end = ((q0 + row) // CHUNK + 1) * CHUNK
    lane_kt = lax.broadcasted_iota(jnp.int32, (1, kt), 1)

    for h in range(IDX_HEADS):
        iqs_ref[h * qb:(h + 1) * qb, :] = iq_ref[:, h * 64:(h + 1) * 64].astype(BF16)
        iwb_ref[h] = jnp.broadcast_to(aviw_ref[:, 64 + h:65 + h], (qb, LANES))

    def score_tile(j, carry):
        off = pl.multiple_of(j * kt, kt)
        ik = kik_ref[pl.ds(off, kt), 64:128]
        s = lax.dot_general(iqs_ref[...], ik, NT_DIMS, preferred_element_type=F32)
        parts = []
        for c in range(kt // LANES):
            a = jnp.zeros((qb, LANES), F32)
            for h in range(IDX_HEADS):
                a = a + iwb_ref[h] * jnp.maximum(s[h * qb:(h + 1) * qb, c * LANES:(c + 1) * LANES], 0.0)
            parts.append(a)
        acc = jnp.concatenate(parts, axis=1)
        acc = acc + 0.0
        acc = jnp.where(off + lane_kt < vis_end, acc, -jnp.inf)
        bits = pltpu.bitcast(acc, jnp.int32)
        key = jnp.where(bits < 0, bits ^ 0x7FFFFFFF, bits)
        skey_ref[:, pl.ds(off, kt)] = key
        half_ref[:, pl.ds(off, kt)] = jnp.right_shift(key, 16).astype(jnp.int16)
        return carry

    lax.fori_loop(0, n_tiles, score_tile, 0)

    def count(pred_fn):
        def body(j, cnt):
            off = pl.multiple_of(j * kt, kt)
            hit = pred_fn(skey_ref[:, pl.ds(off, kt)])
            for c in range(kt // LANES):
                cnt = cnt + jnp.where(hit[:, c * LANES:(c + 1) * LANES], 1.0, 0.0)
            return cnt
        cnt = lax.fori_loop(0, n_tiles, body, jnp.zeros((qb, LANES), F32))
        return jnp.sum(cnt, axis=1, keepdims=True)

    ones_lanes = jnp.ones((LANES, LANES), BF16)

    def count_half_ge(cand):
        cand16 = cand.astype(jnp.int16)
        one = jnp.ones((qb, LANES), BF16)
        nil = jnp.zeros((qb, LANES), BF16)

        def body(j, cnt):
            off = pl.multiple_of(j * kt, kt)
            tile = half_ref[:, pl.ds(off, kt)]
            for c in range(kt // LANES):
                cnt = cnt + jnp.where(tile[:, c * LANES:(c + 1) * LANES] >= cand16, one, nil)
            return cnt
        return _dot(lax.fori_loop(0, n_tiles, body, nil), ones_lanes)

    def bisect16(cnt_min, extra):
        zero = jnp.zeros((qb, LANES), jnp.int32)
        cnt = extra + count_half_ge(zero)
        ok = cnt >= topk
        start = (jnp.where(ok, zero, jnp.full((qb, LANES), -(2 ** 15), jnp.int32)), jnp.where(ok, cnt, cnt_min))

        def bit_step(it, c):
            t, cnt_t = c
            cand = t | jnp.left_shift(jnp.int32(1), 14 - it)
            cnt = extra + count_half_ge(cand)
            ok = cnt >= topk
            return jnp.where(ok, cand, t), jnp.where(ok, cnt, cnt_t)
        return lax.fori_loop(0, 15, bit_step, start)

    visited = jnp.full((qb, LANES), n_tiles * kt, jnp.int32).astype(F32)
    t_hi, cnt_hi = bisect16(visited, 0.0)
    top16 = 2 ** 15 - 1
    above = jnp.where(t_hi == top16, 0.0, count_half_ge(jnp.minimum(t_hi + 1, top16)))
    t_hi_col = t_hi[:, 0:1]

    def low_tile(j, carry):
        off = pl.multiple_of(j * kt, kt)
        key = skey_ref[:, pl.ds(off, kt)]
        low = (key & 0xFFFF) - 2 ** 15
        half_ref[:, pl.ds(off, kt)] = jnp.where(jnp.right_shift(key, 16) == t_hi_col, low, -(2 ** 15)).astype(jnp.int16)
        return carry

    lax.fori_loop(0, n_tiles, low_tile, 0)
    t_lo, cnt_ge = bisect16(cnt_hi, above)
    thr = (jnp.left_shift(t_hi, 16) | (t_lo + 2 ** 15))[:, 0:1]
    lane_ok = lambda off: off + lane_kt < vis_end

    def select_all_ties(j, carry):
        off = pl.multiple_of(j * kt, kt)
        sel = (skey_ref[:, pl.ds(off, kt)] >= thr) & lane_ok(off)
        skey_ref[:, pl.ds(off, kt)] = pltpu.bitcast(jnp.where(sel, 0.0, -jnp.inf), jnp.int32)
        return carry

    def select_ranked_ties(need):
        def body(j, eq_seen):
            off = pl.multiple_of(j * kt, kt)
            key = skey_ref[:, pl.ds(off, kt)]
            eq = key == thr
            rank = _dot(jnp.where(eq, 1.0, 0.0).astype(BF16), tri_ref[...]) + eq_seen
            sel = ((key > thr) | (eq & (rank <= need))) & lane_ok(off)
            skey_ref[:, pl.ds(off, kt)] = pltpu.bitcast(jnp.where(sel, 0.0, -jnp.inf), jnp.int32)
            return rank[:, kt - 1:kt]
        return body

    def exact_fit():
        lax.fori_loop(0, n_tiles, select_all_ties, 0)

    def surplus_ties():
        need = topk - count(lambda x: x > thr)
        lax.fori_loop(0, n_tiles, select_ranked_ties(need), jnp.zeros((qb, 1), F32))

    lax.cond(jnp.max(jnp.abs(cnt_ge - topk)) == 0.0, exact_fit, surplus_ties)

    for h in range(A_HEADS):
        qs_ref[h * qb:(h + 1) * qb, :] = (aq_ref[:, h * 64:(h + 1) * 64] * (HEAD_DIM ** -0.5)).astype(BF16)
    mlane_ref[...] = jnp.full(mlane_ref.shape, -jnp.inf, F32)
    acc_ref[...] = jnp.zeros(acc_ref.shape, F32)

    def max_tile(j, carry):
        off = pl.multiple_of(j * kt, kt)
        bias = pltpu.bitcast(skey_ref[:, pl.ds(off, kt)], F32)
        k = kik_ref[pl.ds(off, kt), 0:64]
        s = lax.dot_general(qs_ref[...], k, NT_DIMS, preferred_element_type=F32)
        for h in range(A_HEADS):
            sh = s[h * qb:(h + 1) * qb, :] + bias
            mm = sh[:, 0:LANES]
            for c in range(1, kt // LANES):
                mm = jnp.maximum(mm, sh[:, c * LANES:(c + 1) * LANES])
            mlane_ref[h] = jnp.maximum(mlane_ref[h], mm)
        return carry

    lax.fori_loop(0, n_tiles, max_tile, 0)
    for h in range(A_HEADS):
        m_ref[h * qb:(h + 1) * qb, :] = jnp.max(mlane_ref[h], axis=1, keepdims=True)

    def pv_tile(j, carry):
        off = pl.multiple_of(j * kt, kt)
        bias = pltpu.bitcast(skey_ref[:, pl.ds(off, kt)], F32)
        k = kik_ref[pl.ds(off, kt), 0:64]
        s = lax.dot_general(qs_ref[...], k, NT_DIMS, preferred_element_type=F32)
        for h in range(A_HEADS):
            rows = slice(h * qb, (h + 1) * qb)
            p_ref[rows, :] = jnp.exp((s[rows, :] + bias) - m_ref[rows, :]).astype(BF16)
        acc_ref[...] += _dot(p_ref[...], v_ref[pl.ds(off, kt), :])
        return carry

    lax.fori_loop(0, n_tiles, pv_tile, 0)
    for h in range(A_HEADS):
        a = acc_ref[h * qb:(h + 1) * qb, :]
        o_ref[:, h * 64:(h + 1) * 64] = (a / a[:, 64:65])[:, 0:64]


def _dsa(aq, iq, aviw, kik_all, v_all, bsz, t_len, past, kt):
    qb = min(DSA_QUERY_ROWS, t_len)
    nq = t_len // qb
    lp = kik_all.shape[1]
    topk = min(TOPK_MAX, (past + t_len) // 4)
    tri = jnp.triu(jnp.ones((kt, kt), F32)).astype(BF16)
    qrow = lambda w: pl.BlockSpec((qb, w), lambda b, i: (b * nq + i, 0))
    keys = pl.BlockSpec((None, lp, LANES), lambda b, i: (b, 0, 0))
    return pl.pallas_call(
        functools.partial(_dsa_kernel, past=past, qb=qb, kt=kt, topk=topk),
        grid=(bsz, nq),
        in_specs=[qrow(512), qrow(512), qrow(LANES), keys, keys, pl.BlockSpec((kt, kt), lambda b, i: (0, 0))],
        out_specs=qrow(512),
        out_shape=jax.ShapeDtypeStruct((bsz * t_len, 512), F32),
        scratch_shapes=[
            pltpu.VMEM((qb, lp), jnp.int32),
            pltpu.VMEM((qb, lp), jnp.int16),
            pltpu.VMEM((IDX_HEADS * qb, 64), BF16),
            pltpu.VMEM((IDX_HEADS, qb, LANES), F32),
            pltpu.VMEM((A_HEADS * qb, 64), BF16),
            pltpu.VMEM((A_HEADS * qb, kt), BF16),
            pltpu.VMEM((A_HEADS * qb, 1), F32),
            pltpu.VMEM((A_HEADS, qb, LANES), F32),
            pltpu.VMEM((A_HEADS * qb, LANES), F32),
        ],
        compiler_params=_params("parallel", "arbitrary"),
        name="dsa",
    )(aq, iq, aviw, kik_all, v_all, tri)


def _retention_kernel(bqk_ref, bv_ref, bg_ref, s0_ref, gn_ref, o_ref, s_out_ref, s_ref, *, n_chunks):
    t = pl.program_id(0)

    @pl.when(t == 0)
    def _():
        s_ref[...] = s0_ref[...]

    n = CHUNK
    ri = lax.broadcasted_iota(jnp.int32, (n, n), 0).astype(F32)
    ci = lax.broadcasted_iota(jnp.int32, (n, n), 1).astype(F32)
    diff = ri - ci
    pos = lax.broadcasted_iota(jnp.int32, (n, 1), 0).astype(F32)
    decays = []
    for h in range(B_HEADS):
        log_g = math.log(1.0 - 2.0 ** (-5.0 - h))
        decays.append((jnp.where(diff >= 0, jnp.exp(jnp.maximum(diff, 0.0) * log_g), 0.0),
                       jnp.exp((pos + 1.0) * log_g), jnp.exp((n - 1.0 - pos) * log_g), math.exp(n * log_g)))
    for c in range(n_chunks):
        rows = slice(c * n, (c + 1) * n)
        for b in range(bqk_ref.shape[0]):
            for h in range(B_HEADS):
                intra, q_decay, k_decay, chunk_decay = decays[h]
                q = bqk_ref[b, rows, h * B_DK:(h + 1) * B_DK]
                k = bqk_ref[b, rows, 256 + h * B_DK:256 + (h + 1) * B_DK]
                v = bv_ref[b, rows, h * B_DV:(h + 1) * B_DV]
                s = s_ref[b, h]
                scores = _dot_f32x3(q, k, NT_DIMS) * intra
                o = _dot_f32x3(scores, v) + _dot_f32x3(q, s) * q_decay
                s_ref[b, h] = s * chunk_decay + _dot_f32x3(k * k_decay, v, TN_DIMS)
                mu = jnp.mean(o, axis=-1, keepdims=True)
                oc = o - mu
                var = jnp.mean(oc * oc, axis=-1, keepdims=True)
                g = bg_ref[b, rows, h * B_DV:(h + 1) * B_DV]
                o_ref[b, rows, h * B_DV:(h + 1) * B_DV] = (
                    oc * lax.rsqrt(var + LN_EPS) * gn_ref[:, h * B_DV:(h + 1) * B_DV] * (g * _sigmoid(g)))

    @pl.when(t == pl.num_programs(0) - 1)
    def _():
        s_out_ref[...] = s_ref[...]


def _retention(bqk, bv, bg, s0, b_gn, bsz, t_len, tt):
    row = pl.BlockSpec((bsz, tt, 512), lambda t: (0, t, 0))
    state = pl.BlockSpec((bsz, B_HEADS, B_DK, B_DV), lambda t: (0, 0, 0, 0))
    o_b, s_new = pl.pallas_call(
        functools.partial(_retention_kernel, n_chunks=tt // CHUNK),
        grid=(t_len // tt,),
        in_specs=[row, row, row, state, pl.BlockSpec((1, 512), lambda t: (0, 0))],
        out_specs=[row, state],
        out_shape=[jax.ShapeDtypeStruct((bsz, t_len, 512), F32),
                   jax.ShapeDtypeStruct((bsz, B_HEADS, B_DK, B_DV), F32)],
        scratch_shapes=[pltpu.VMEM((bsz, B_HEADS, B_DK, B_DV), F32)],
        compiler_params=_params("arbitrary"),
        name="retention",
    )(bqk.reshape(bsz, t_len, 512), bv.reshape(bsz, t_len, 512), bg.reshape(bsz, t_len, 512), s0, b_gn)
    return o_b.reshape(bsz * t_len, 512), s_new


def _out_ln_kernel(oa_ref, ob_ref, x_ref, w_ref, g_ref, b_ref, y_ref):
    y = _dot(oa_ref[...].astype(BF16), w_ref[0:512, :]) + _dot(ob_ref[...].astype(BF16), w_ref[512:1024, :])
    y_ref[...] = _layernorm_rows(ALPHA * x_ref[...] + y, g_ref[...], b_ref[...])


def _out_ln(oa, ob, x2, w_out, g, b, tm):
    n = x2.shape[0]
    row = lambda w: pl.BlockSpec((tm, w), lambda i: (i, 0))
    vec = pl.BlockSpec((1, D_MODEL), lambda i: (0, 0))
    return pl.pallas_call(
        _out_ln_kernel,
        grid=(n // tm,),
        in_specs=[row(512), row(512), row(D_MODEL), pl.BlockSpec((D_MODEL, D_MODEL), lambda i: (0, 0)), vec, vec],
        out_specs=row(D_MODEL),
        out_shape=jax.ShapeDtypeStruct((n, D_MODEL), F32),
        compiler_params=_params("parallel"),
        name="out_ln",
    )(oa, ob, x2, w_out, g, b)


def _out_ln_d_kernel(oc_ref, od_ref, r_ref, kr_ref, v_ref, gate_ref, x_ref, w_ref, hm_ref,
                     lnx_g_ref, lnx_b_ref, rk_ref, g_ref, b_ref, y_ref):
    hm = hm_ref[...]
    o = od_ref[...]
    mu = _dot_exact_rhs(o, hm) * (1.0 / HEAD_DIM)
    oc = o - mu
    var = _dot_exact_rhs(oc * oc, hm) * (1.0 / HEAD_DIM)
    normed = oc * lax.rsqrt(var + D_GN_EPS) * lnx_g_ref[...] + lnx_b_ref[...]
    v = v_ref[...]
    bonus = _dot_exact_rhs(r_ref[...] * kr_ref[...] * rk_ref[...], hm) * v
    od = (normed + bonus) * gate_ref[...]
    y = _dot(oc_ref[...].astype(BF16), w_ref[0:512, :]) + _dot(od.astype(BF16), w_ref[512:1024, :])
    y_ref[...] = _layernorm_rows(ALPHA * x_ref[...] + y, g_ref[...], b_ref[...])


def _head_mask():
    head = jnp.arange(D_W) // HEAD_DIM
    return (head[:, None] == head[None, :]).astype(BF16)


def _out_ln_d(oc, od, r, kr, v, gate, x2, w_out, lnx_g, lnx_b, r_k, g, b, tm):
    n = x2.shape[0]
    row = lambda w: pl.BlockSpec((tm, w), lambda i: (i, 0))
    vec = lambda w: pl.BlockSpec((1, w), lambda i: (0, 0))
    return pl.pallas_call(
        _out_ln_d_kernel,
        grid=(n // tm,),
        in_specs=[row(512)] * 6 + [row(D_MODEL), pl.BlockSpec((D_MODEL, D_MODEL), lambda i: (0, 0)),
                                   pl.BlockSpec((D_W, D_W), lambda i: (0, 0)),
                                   vec(D_W), vec(D_W), vec(D_W), vec(D_MODEL), vec(D_MODEL)],
        out_specs=row(D_MODEL),
        out_shape=jax.ShapeDtypeStruct((n, D_MODEL), F32),
        compiler_params=_params("parallel"),
        name="out_ln_d",
    )(oc, od, r, kr, v, gate, x2, w_out, _head_mask(), lnx_g, lnx_b, r_k, g, b)


MOE_EXPERTS_PER_STEP = 4


def _moe_kernel(x_ref, rw_ref, rb_ref, w13_ref, w2_ref, g_ref, b_ref, y_ref, acc_ref, gate_ref, xb_ref):
    e = pl.program_id(1)
    tm = x_ref.shape[0]
    lane = lax.broadcasted_iota(jnp.int32, (tm, N_EXPERTS), 1)

    lane_f = lane.astype(F32)

    def first_argmax(v):
        m = jnp.max(v, axis=1, keepdims=True)
        idx = jnp.min(jnp.where(v == m, lane_f, float(N_EXPERTS)), axis=1, keepdims=True)
        return m, idx.astype(jnp.int32)

    @pl.when(e == 0)
    def _():
        x = x_ref[...]
        xb_ref[...] = x.astype(BF16)
        aff = _sigmoid(_dot_hi(x, rw_ref[...]))
        biased = aff + rb_ref[...]
        best = jnp.zeros((tm, 1), jnp.int32)
        best_score = jnp.full((tm, 1), -jnp.inf, F32)
        for grp in range(N_GROUPS):
            vg = jnp.where(lane // EXPERTS_PER_GROUP == grp, biased, -jnp.inf)
            top1, idx1 = first_argmax(vg)
            top2, _ = first_argmax(jnp.where(lane == idx1, -jnp.inf, vg))
            score = top1 + top2
            better = score > best_score
            best = jnp.where(better, grp, best)
            best_score = jnp.where(better, score, best_score)
        masked = jnp.where(lane // EXPERTS_PER_GROUP == best, biased, -jnp.inf)
        _, idx1 = first_argmax(masked)
        _, idx2 = first_argmax(jnp.where(lane == idx1, -jnp.inf, masked))
        top_aff = jnp.where((lane == idx1) | (lane == idx2), aff, 0.0)
        gate_ref[...] = top_aff / jnp.sum(top_aff, axis=1, keepdims=True)
        acc_ref[...] = jnp.zeros(acc_ref.shape, F32)

    for k in range(MOE_EXPERTS_PER_STEP):
        h13 = _dot(xb_ref[...], w13_ref[k])
        h1 = h13[:, 0:D_EXPERT]
        h = (h1 * _sigmoid(h1)) * h13[:, D_EXPERT:2 * D_EXPERT]
        gate_e = jnp.sum(jnp.where(lane == e * MOE_EXPERTS_PER_STEP + k, gate_ref[...], 0.0), axis=1, keepdims=True)
        acc_ref[...] += gate_e * _dot(h.astype(BF16), w2_ref[k])

    @pl.when(e == pl.num_programs(1) - 1)
    def _():
        y_ref[...] = _layernorm_rows(ALPHA * x_ref[...] + acc_ref[...], g_ref[...], b_ref[...])


def _moe_ln(x2, router_w, router_b, w13, w2, g, b, tm):
    n = x2.shape[0]
    row = pl.BlockSpec((tm, D_MODEL), lambda i, e: (i, 0))
    vec = pl.BlockSpec((1, D_MODEL), lambda i, e: (0, 0))
    return pl.pallas_call(
        _moe_kernel,
        grid=(n // tm, N_EXPERTS // MOE_EXPERTS_PER_STEP),
        in_specs=[row,
                  pl.BlockSpec((D_MODEL, N_EXPERTS), lambda i, e: (0, 0)),
                  pl.BlockSpec((1, N_EXPERTS), lambda i, e: (0, 0)),
                  pl.BlockSpec((MOE_EXPERTS_PER_STEP, D_MODEL, 2 * D_EXPERT), lambda i, e: (e, 0, 0)),
                  pl.BlockSpec((MOE_EXPERTS_PER_STEP, D_EXPERT, D_MODEL), lambda i, e: (e, 0, 0)),
                  vec, vec],
        out_specs=row,
        out_shape=jax.ShapeDtypeStruct((n, D_MODEL), F32),
        scratch_shapes=[pltpu.VMEM((tm, D_MODEL), F32), pltpu.VMEM((tm, N_EXPERTS), F32),
                        pltpu.VMEM((tm, D_MODEL), BF16)],
        compiler_params=_params("parallel", "arbitrary"),
        name="moe_ln",
    )(x2, router_w, router_b, w13, w2, g, b)


def _proj_cd_kernel(x_ref, w_ref, cq_ref, ck_ref, cv_ref, pd_ref, ck16_ref, cv16_ref):
    xb = x_ref[...].astype(BF16)
    cq_ref[...] = _dot(xb, w_ref[:, 0:512])
    ck = _dot(xb, w_ref[:, 512:1024])
    ck_ref[...] = ck
    ck16_ref[...] = ck.astype(BF16)
    cv = _dot(xb, w_ref[:, 1024:1536])
    cv_ref[...] = cv
    cv16_ref[...] = cv.astype(BF16)
    pd_ref[...] = _dot(xb, w_ref[:, 1536:1536 + D_SHIFT_W])


def _proj_cd(x2, w_bf16, tm):
    n = x2.shape[0]
    row = lambda w: pl.BlockSpec((tm, w), lambda i: (i, 0))
    outs = [512, 512, 512, D_SHIFT_W]
    return pl.pallas_call(
        _proj_cd_kernel,
        grid=(n // tm,),
        in_specs=[row(D_MODEL), pl.BlockSpec((D_MODEL, 1536 + D_SHIFT_W), lambda i: (0, 0))],
        out_specs=[row(w) for w in outs] + [row(512), row(512)],
        out_shape=[jax.ShapeDtypeStruct((n, w), F32) for w in outs] + [jax.ShapeDtypeStruct((n, 512), BF16)] * 2,
        compiler_params=_params("parallel"),
        name="proj_cd",
    )(x2, w_bf16)


def _stick_kernel(q_ref, k_ref, v_ref, ust_ref, o_ref, *, past, tq, kb):
    i = pl.program_id(2)
    q0 = past + i * tq
    n_kb = (q0 + tq + kb - 1) // kb
    qpos = q0 + lax.broadcasted_iota(jnp.int32, (tq, 1), 0)
    lane_kb = lax.broadcasted_iota(jnp.int32, (1, kb), 1)
    n_heads = LANES // HEAD_DIM
    cols = [slice(hh * HEAD_DIM, (hh + 1) * HEAD_DIM) for hh in range(n_heads)]
    qs = [(q_ref[:, c] * (HEAD_DIM ** -0.5)).astype(BF16) for c in cols]

    def cond(c):
        j, carries, _ = c
        worst = carries[0]
        for carry in carries[1:]:
            worst = jnp.maximum(worst, carry)
        return jnp.logical_and(j >= 0, jnp.max(worst) > EXP_ZERO_BELOW)

    def body(c):
        j, carries, outs = c
        off = pl.multiple_of(j * kb, kb)
        strict = off + lane_kb < qpos
        new_carries, new_outs = [], []
        for hh in range(n_heads):
            k = k_ref[pl.ds(off, kb), cols[hh]]
            v = v_ref[pl.ds(off, kb), cols[hh]]
            z = lax.dot_general(qs[hh], k, NT_DIMS, preferred_element_type=F32)
            sp = _softplus(z)
            log_keep = jnp.where(strict, -sp, 0.0)
            later = carries[hh] + _dot_exact_rhs(log_keep, ust_ref[...])
            a = jnp.where(strict, jnp.exp((z - sp) + later), 0.0)
            new_outs.append(outs[hh] + _dot(a.astype(BF16), v))
            new_carries.append(carries[hh] + jnp.sum(log_keep, axis=1, keepdims=True))
        return j - 1, tuple(new_carries), tuple(new_outs)

    init = (n_kb - 1, tuple(jnp.zeros((tq, 1), F32) for _ in cols), tuple(jnp.zeros((tq, HEAD_DIM), F32) for _ in cols))
    _, _, outs = lax.while_loop(cond, body, init)
    for hh in range(n_heads):
        o_ref[:, cols[hh]] = outs[hh]


def _stick(cq, k_all, v_all, bsz, t_len, past, kb):
    tq = min(kb, t_len)
    nq = t_len // tq
    lp = k_all.shape[1]
    ust = jnp.tril(jnp.ones((kb, kb), F32), -1).astype(BF16)
    qrow = pl.BlockSpec((tq, LANES), lambda b, hp, i: (b * nq + i, hp))
    keys = pl.BlockSpec((None, lp, LANES), lambda b, hp, i: (b, 0, hp))
    return pl.pallas_call(
        functools.partial(_stick_kernel, past=past, tq=tq, kb=kb),
        grid=(bsz, C_HEADS * HEAD_DIM // LANES, nq),
        in_specs=[qrow, keys, keys, pl.BlockSpec((kb, kb), lambda b, hp, i: (0, 0))],
        out_specs=qrow,
        out_shape=jax.ShapeDtypeStruct((bsz * t_len, 512), F32),
        compiler_params=_params("parallel", "parallel", "arbitrary"),
        name="stick",
    )(cq, k_all, v_all, ust)


def _rwkv_pre_kernel(pd_ref, shift_ref, mu_ref, w0a0_ref, wa2_ref, gw2_ref, kk_ref_, ka_ref, hm_ref,
                     r_out, w_out, kr_out, v_out, kk_out, b_out, g_out, last_ref):
    t = pl.program_id(1)

    @pl.when(t == 0)
    def _():
        last_ref[...] = shift_ref[...]

    pd = pd_ref[...]
    tm = pd.shape[0]
    rolled = pltpu.roll(pd, 1, 0)
    first_row = lax.broadcasted_iota(jnp.int32, (tm, 1), 0) == 0
    prev = jnp.where(first_row, last_ref[...], rolled)
    last_ref[...] = pd[tm - 1:tm, :]
    pm = pd + (prev - pd) * mu_ref[...]
    r = pm[:, 0:512]
    k = pm[:, 512:1024]
    v = pm[:, 1024:1536]
    lwa = pm[:, 1536:1664]
    lg = pm[:, 1664:1792]
    lane = lax.broadcasted_iota(jnp.int32, lwa.shape, 1)
    lwa = jnp.where(lane < D_LORA_W, jnp.tanh(lwa), lwa)
    pre = w0a0_ref[...] + _dot_hi(lwa, wa2_ref[...])
    w_log = -_softplus(-pre[:, 0:512]) - 0.5
    decay = jnp.exp(-jnp.exp(w_log))
    a = _sigmoid(pre[:, 512:1024])
    g = _dot_hi(_sigmoid(lg), gw2_ref[...])
    kk = k * kk_ref_[...]
    kk = kk * lax.rsqrt(_dot_exact_rhs(kk * kk, hm_ref[...]) + 1e-12)
    r_out[...] = r
    w_out[...] = decay
    kr_out[...] = k * (1.0 + (a - 1.0) * ka_ref[...])
    v_out[...] = v
    kk_out[...] = kk
    b_out[...] = kk * a
    g_out[...] = g


def _rwkv_pre(pd, shift, mu, w0a0, wa2, gw2, k_k, k_a, bsz, t_len, tm):
    nt = t_len // tm
    row = lambda w: pl.BlockSpec((tm, w), lambda b, t: (b * nt + t, 0))
    const = lambda s: pl.BlockSpec(s, lambda b, t: (0,) * len(s))
    return pl.pallas_call(
        _rwkv_pre_kernel,
        grid=(bsz, nt),
        in_specs=[row(D_SHIFT_W), pl.BlockSpec((None, 1, D_SHIFT_W), lambda b, t: (b, 0, 0)),
                  const((1, D_SHIFT_W)), const((1, 2 * D_W)), const((LANES, 2 * D_W)), const((D_LORA_G, D_W)),
                  const((1, D_W)), const((1, D_W)), const((D_W, D_W))],
        out_specs=[row(D_W)] * 7,
        out_shape=[jax.ShapeDtypeStruct((bsz * t_len, D_W), F32)] * 7,
        scratch_shapes=[pltpu.VMEM((1, D_SHIFT_W), F32)],
        compiler_params=_params("parallel", "arbitrary"),
        name="rwkv_pre",
    )(pd, shift, mu, w0a0, wa2, gw2, k_k, k_a, _head_mask())


RWKV_VL = HEAD_DIM // 2


def _rwkv_scan_kernel(kk_ref, w_ref, b_ref, kr_ref, r_ref, v_ref, s0_ref, o_ref, s_out_ref, s_ref, *, tb):
    g = pl.program_id(0)

    @pl.when(g == 0)
    def _():
        s_ref[...] = s0_ref[...]

    def step(t, carry):
        kk = kk_ref[t]
        w = w_ref[t]
        b = b_ref[t]
        kr = kr_ref[t]
        r = r_ref[t]
        for vl in range(RWKV_VL):
            s = s_ref[vl]
            sa = jnp.sum(s * kk, axis=0, keepdims=True)
            s = s * w - sa * b + v_ref[t, vl:vl + 1, :] * kr
            s_ref[vl] = s
            o_ref[t, vl:vl + 1, :] = jnp.sum(s * r, axis=0, keepdims=True)
        return carry

    lax.fori_loop(0, tb, step, 0)

    @pl.when(g == pl.num_programs(0) - 1)
    def _():
        s_out_ref[...] = s_ref[...]


def _rwkv_scan(kk, w, b, kr, r, v, s0, t_len, tb):
    big = pl.BlockSpec((tb, HEAD_DIM, LANES), lambda g: (g, 0, 0))
    small = pl.BlockSpec((tb, RWKV_VL, LANES), lambda g: (g, 0, 0))
    state = pl.BlockSpec((RWKV_VL, HEAD_DIM, LANES), lambda g: (0, 0, 0))
    return pl.pallas_call(
        functools.partial(_rwkv_scan_kernel, tb=tb),
        grid=(t_len // tb,),
        in_specs=[big] * 5 + [small, state],
        out_specs=[small, state],
        out_shape=[jax.ShapeDtypeStruct((t_len, RWKV_VL, LANES), F32),
                   jax.ShapeDtypeStruct((RWKV_VL, HEAD_DIM, LANES), F32)],
        scratch_shapes=[pltpu.VMEM((RWKV_VL, HEAD_DIM, LANES), F32)],
        compiler_params=_params("arbitrary"),
        name="rwkv_scan",
    )(kk, w, b, kr, r, v, s0)


def _to_scan_keyed(x2, bsz, t_len):
    y = x2.reshape(bsz, t_len, D_HEADS, HEAD_DIM).transpose(1, 3, 0, 2).reshape(t_len, HEAD_DIM, bsz * D_HEADS)
    return jnp.concatenate([y, y], axis=-1)


def _to_scan_valued(x2, bsz, t_len):
    y = x2.reshape(bsz, t_len, D_HEADS, 2, RWKV_VL).transpose(1, 4, 3, 0, 2)
    return y.reshape(t_len, RWKV_VL, 2 * bsz * D_HEADS)


def _from_scan_valued(y, bsz, t_len):
    return y.reshape(t_len, RWKV_VL, 2, bsz, D_HEADS).transpose(3, 0, 4, 2, 1).reshape(bsz * t_len, D_W)


def _state_to_scan(s, bsz):
    y = s.reshape(bsz, D_HEADS, 2, RWKV_VL, HEAD_DIM).transpose(3, 4, 2, 0, 1)
    return y.reshape(RWKV_VL, HEAD_DIM, 2 * bsz * D_HEADS)


def _state_from_scan(y, bsz):
    return y.reshape(RWKV_VL, HEAD_DIM, 2, bsz, D_HEADS).transpose(3, 4, 2, 0, 1).reshape(
        bsz, D_HEADS, HEAD_DIM, HEAD_DIM)


def _pad_keys(x, mult):
    pad = (-x.shape[1]) % mult
    return x if pad == 0 else jnp.pad(x, ((0, 0), (0, pad), (0, 0)))


def _tile(n, cap):
    return min(n, cap)


def _run_group(x, past, p):
    bsz, t_len, _ = x.shape
    assert bsz * D_HEADS * 2 == LANES, "rwkv scan packs (value half, batch, head) into the lane axis"
    n = bsz * t_len
    x2 = x.reshape(n, D_MODEL)
    past_len = 0 if past is None else past[0].shape[2]
    tm = _tile(t_len, 512)

    cos, sin = _rope_tables(past_len, t_len)
    aq, iq, bqk, kik, aviw, bv, bg, kik16, v16 = _proj_ab(x2, p['w_in_ab'], cos, sin, t_len, tm)
    ak = kik[:, 0:64].reshape(1, bsz, t_len, 64)
    ik = kik[:, 64:128].reshape(1, bsz, t_len, 64)
    av = aviw[:, 0:64].reshape(1, bsz, t_len, 64)
    kik_all = kik16.reshape(bsz, t_len, LANES)
    v_all = v16.reshape(bsz, t_len, LANES)
    ones_col = (jnp.arange(LANES) == 64).astype(F32)
    if past is None:
        s_b = jnp.zeros((bsz, B_HEADS, B_DK, B_DV), F32)
    else:
        pk, pv, pik, sb = past[0][0], past[1][0], past[2][0], past[3][0]
        kik_all = jnp.concatenate([jnp.concatenate([pk, pik], axis=-1).astype(BF16), kik_all], axis=1)
        pv_slab = jnp.concatenate([pv, jnp.broadcast_to(ones_col[64:], pv.shape)], axis=-1)
        v_all = jnp.concatenate([pv_slab.astype(BF16), v_all], axis=1)
        s_b = sb
    kt = 1024
    o_a = _dsa(aq, iq, aviw, _pad_keys(kik_all, kt), _pad_keys(v_all, kt), bsz, t_len, past_len, kt)
    o_b, s_b_new = _retention(bqk, bv, bg, s_b, p['b_gn'], bsz, t_len, _tile(t_len, 2 * CHUNK))
    x2 = _out_ln(o_a, o_b, x2, p['w_out_ab'], p['ln_g'][0, 0][None], p['ln_b'][0, 0][None], tm)
    tm_moe = _tile(n, 1024)
    x2 = _moe_ln(x2, p['router_w'], p['router_b'], p['w13'][0], p['w2'][0],
                 p['ln_g'][0, 1][None], p['ln_b'][0, 1][None], tm_moe)

    cq, ck, cv, pd, ck16, cv16 = _proj_cd(x2, p['w_in_cd'], tm)
    ck_all = ck16.reshape(bsz, t_len, 512)
    cv_all = cv16.reshape(bsz, t_len, 512)
    if past is None:
        s_d = jnp.zeros((bsz, D_HEADS, HEAD_DIM, HEAD_DIM), F32)
        shift = jnp.zeros((bsz, 1, D_SHIFT_W), F32)
    else:
        ck_all = jnp.concatenate([past[4][0].reshape(bsz, past_len, 512).astype(BF16), ck_all], axis=1)
        cv_all = jnp.concatenate([past[5][0].reshape(bsz, past_len, 512).astype(BF16), cv_all], axis=1)
        s_d, shift = past[6][0], past[7][0]
    kb = 256
    o_c = _stick(cq, _pad_keys(ck_all, kb), _pad_keys(cv_all, kb), bsz, t_len, past_len, kb)
    r, w, kr, v, kk, b, g = _rwkv_pre(pd, shift, p['d_mu'], p['d_w0a0'], p['d_wa2'], p['d_g2'],
                                      p['d_k_k'], p['d_k_a'], bsz, t_len, tm)
    keyed = [_to_scan_keyed(u, bsz, t_len) for u in (kk, w, b, kr, r)]
    o_scan, s_scan = _rwkv_scan(*keyed, _to_scan_valued(v, bsz, t_len), _state_to_scan(s_d, bsz),
                                t_len, _tile(t_len, 64))
    o_d = _from_scan_valued(o_scan, bsz, t_len)
    s_d_new = _state_from_scan(s_scan, bsz)
    x2 = _out_ln_d(o_c, o_d, r, kr, v, g, x2, p['w_out_cd'], p['d_lnx_g'], p['d_lnx_b'], p['d_r_k'],
                   p['ln_g'][1, 0][None], p['ln_b'][1, 0][None], tm)
    x2 = _moe_ln(x2, p['router_w'], p['router_b'], p['w13'][1], p['w2'][1],
                 p['ln_g'][1, 1][None], p['ln_b'][1, 1][None], tm_moe)

    states = (ak, av, ik, s_b_new[None],
              ck.reshape(1, bsz, t_len, C_HEADS, HEAD_DIM), cv.reshape(1, bsz, t_len, C_HEADS, HEAD_DIM),
              s_d_new[None], pd.reshape(bsz, t_len, D_SHIFT_W)[:, -1:][None])
    return x2.reshape(bsz, t_len, D_MODEL), states


def kernel(x_prompt, x_sample, cache_a_k, cache_a_v, cache_a_idx_k, state_b, cache_c_k, cache_c_v, state_d, state_d_shift, w_in_ab, w_out_ab, b_gn, w_in_cd, w_out_cd, d_mu, d_w0, d_w2, d_a0, d_a2, d_g2, d_k_k, d_k_a, d_r_k, d_lnx_g, d_lnx_b, ln_g, ln_b, router_w, router_b, moe_w1, moe_w3, moe_w2):
    zeros_w = jnp.zeros((D_LORA_W, D_W), F32)
    p = {
        'w_in_ab': _pack_w_in_ab(w_in_ab[0]),
        'w_out_ab': w_out_ab[0].astype(BF16),
        'b_gn': b_gn,
        'w_in_cd': w_in_cd[0].astype(BF16),
        'w_out_cd': w_out_cd[0].astype(BF16),
        'd_mu': d_mu,
        'd_w0a0': jnp.concatenate([d_w0, d_a0], axis=1),
        'd_wa2': jnp.concatenate([jnp.concatenate([d_w2[0], zeros_w], axis=1),
                                  jnp.concatenate([zeros_w, d_a2[0]], axis=1)], axis=0),
        'd_g2': d_g2[0],
        'd_k_k': d_k_k, 'd_k_a': d_k_a, 'd_r_k': d_r_k, 'd_lnx_g': d_lnx_g, 'd_lnx_b': d_lnx_b,
        'ln_g': ln_g, 'ln_b': ln_b,
        'router_w': router_w, 'router_b': router_b[None],
        'w13': jnp.concatenate([moe_w1, moe_w3], axis=-1).astype(BF16),
        'w2': moe_w2.astype(BF16),
    }
    y_p, sp = _run_group(x_prompt, None, p)
    past = (cache_a_k, cache_a_v, cache_a_idx_k, state_b, cache_c_k, cache_c_v, state_d, state_d_shift)
    y_s, ss = _run_group(x_sample, past, p)
    return (y_p, y_s, sp[0], sp[1], sp[2], ss[0], ss[1], ss[2], sp[3], ss[3], sp[4], sp[5], ss[4], ss[5],
            sp[6], ss[6], sp[7], ss[7])
```

```python
import functools
import math

import jax
import jax.numpy as jnp
import numpy as np
from jax import lax
from jax.experimental import pallas as pl
from jax.experimental.pallas import tpu as pltpu

F32 = jnp.float32
BF16 = jnp.bfloat16

D_MODEL = 1024
CHUNK = 64
DSA_QUERY_ROWS = 128
ROPE_THETA = 10000.0
HEAD_DIM = 64
LN_EPS = 1e-5
A_HEADS = 8
IDX_HEADS = 8
TOPK_MAX = 256
B_HEADS = 4
B_DK = 64
B_DV = 128
C_HEADS = 8
D_HEADS = 8
D_LORA_W = 64
D_LORA_A = 64
D_LORA_G = 128
D_GN_EPS = 64e-5
N_EXPERTS = 16
N_GROUPS = 4
EXPERTS_PER_GROUP = 4
D_EXPERT = 256
DEPTH = 2
ALPHA = (2 * DEPTH) ** 0.25
D_W = D_HEADS * HEAD_DIM
D_SHIFT_W = 3 * D_W + D_LORA_W + D_LORA_A + D_LORA_G

LANES = 128
SUBLANES = 8
VMEM_LIMIT_BYTES = 56 * 1024 * 1024

INT_MIN = -(2 ** 31)
EXP_ZERO_BELOW = -104.0

NT_DIMS = (((1,), (1,)), ((), ()))
TN_DIMS = (((0,), (0,)), ((), ()))


def _params(*sem):
    return pltpu.CompilerParams(dimension_semantics=sem, vmem_limit_bytes=VMEM_LIMIT_BYTES)


def _dot(a, b):
    return jnp.dot(a, b, preferred_element_type=F32)


def _dot_hi(a, b):
    return jnp.dot(a, b, preferred_element_type=F32, precision=lax.Precision.HIGHEST)


def _dot_f32x3(a, b, dims=(((1,), (0,)), ((), ()))):
    a_hi = a.astype(BF16)
    b_hi = b.astype(BF16)
    a_lo = (a - a_hi.astype(F32)).astype(BF16)
    b_lo = (b - b_hi.astype(F32)).astype(BF16)
    dot = lambda x, y: lax.dot_general(x, y, dims, preferred_element_type=F32)
    return dot(a_hi, b_hi) + dot(a_hi, b_lo) + dot(a_lo, b_hi)


def _split3(x):
    h1 = x.astype(BF16)
    r1 = x - h1.astype(F32)
    h2 = r1.astype(BF16)
    r2 = r1 - h2.astype(F32)
    return h1, h2, r2.astype(BF16)


def _dot_exact_rhs(x, m01):
    h1, h2, h3 = _split3(x)
    return _dot(h1, m01) + _dot(h2, m01) + _dot(h3, m01)


def _dot_exact_rhs2(x, m01):
    h1 = x.astype(BF16)
    h2 = (x - h1.astype(F32)).astype(BF16)
    return _dot(h1, m01) + _dot(h2, m01)


def _layernorm_rows(x, g, b):
    mu = jnp.mean(x, axis=-1, keepdims=True)
    xc = x - mu
    var = jnp.mean(xc * xc, axis=-1, keepdims=True)
    return xc * lax.rsqrt(var + LN_EPS) * g + b


def _sigmoid(x):
    return 1.0 / (1.0 + jnp.exp(-x))


def _softplus(x):
    return jnp.maximum(x, 0.0) + jnp.log1p(jnp.exp(-jnp.abs(x)))


def _rope_slab(x, cos, sin_signed):
    lane = lax.broadcasted_iota(jnp.int32, x.shape, 1)
    first_half = (lane % HEAD_DIM) < (HEAD_DIM // 2)
    swapped = jnp.where(first_half, pltpu.roll(x, LANES - HEAD_DIM // 2, 1), pltpu.roll(x, HEAD_DIM // 2, 1))
    return x * cos + swapped * sin_signed


AB_ROPED = 1664
AB_PACKED = 2816


def _pack_w_in_ab(w):
    aq, ak, av, iq, ik, iw, bq, bk, bv, bg = jnp.split(
        w, [512, 576, 640, 1152, 1216, 1224, 1480, 1736, 2248], axis=1)
    pad = jnp.zeros((w.shape[0], LANES - 64 - IDX_HEADS), w.dtype)
    return jnp.concatenate([aq, iq, bq, bk, ak, ik, av, iw, pad, bv, bg], axis=1).astype(BF16)


def _proj_ab_kernel(x_ref, w_ref, cos_ref, sin_ref, aq_ref, iq_ref, bqk_ref, kik_ref, aviw_ref, bv_ref, bg_ref,
                    kik16_ref, v16_ref):
    xb = x_ref[...].astype(BF16)
    cos = cos_ref[...]
    sin = sin_ref[...]

    def roped(col0, width, scale_from=None):
        y = _dot(xb, w_ref[:, col0:col0 + width])
        parts = []
        for c in range(width // LANES):
            slab = _rope_slab(y[:, c * LANES:(c + 1) * LANES], cos, sin)
            if scale_from is not None and c * LANES >= scale_from:
                slab = slab * (B_DK ** -0.5)
            parts.append(slab)
        return parts

    for c, slab in enumerate(roped(0, 512)):
        aq_ref[:, c * LANES:(c + 1) * LANES] = slab
    for c, slab in enumerate(roped(512, 512)):
        iq_ref[:, c * LANES:(c + 1) * LANES] = slab
    for c, slab in enumerate(roped(1024, 512, scale_from=256)):
        bqk_ref[:, c * LANES:(c + 1) * LANES] = slab
    kik = roped(1536, LANES)[0]
    kik_ref[...] = kik
    kik16_ref[...] = kik.astype(BF16)
    aviw = _dot(xb, w_ref[:, AB_ROPED:AB_ROPED + LANES])
    lane = lax.broadcasted_iota(jnp.int32, aviw.shape, 1)
    is_iw = (lane >= 64) & (lane < 64 + IDX_HEADS)
    aviw_ref[...] = jnp.where(is_iw, aviw * ((IDX_HEADS * HEAD_DIM) ** -0.5), aviw)
    v16_ref[...] = jnp.where(lane < 64, aviw, jnp.where(lane == 64, 1.0, 0.0)).astype(BF16)
    bv_ref[...] = _dot(xb, w_ref[:, 1792:2304])
    bg_ref[...] = _dot(xb, w_ref[:, 2304:2816])


def _proj_ab(x2, w_packed, cos, sin, t_len, tm):
    n = x2.shape[0]
    nt = t_len // tm
    row = lambda w: pl.BlockSpec((tm, w), lambda i: (i, 0))
    tab = pl.BlockSpec((tm, LANES), lambda i: (i % nt, 0))
    outs = [512, 512, 512, LANES, LANES, 512, 512]
    return pl.pallas_call(
        _proj_ab_kernel,
        grid=(n // tm,),
        in_specs=[row(D_MODEL), pl.BlockSpec((D_MODEL, AB_PACKED), lambda i: (0, 0)), tab, tab],
        out_specs=[row(w) for w in outs] + [row(LANES), row(LANES)],
        out_shape=[jax.ShapeDtypeStruct((n, w), F32) for w in outs] + [jax.ShapeDtypeStruct((n, LANES), BF16)] * 2,
        compiler_params=_params("parallel"),
        name="proj_ab",
    )(x2, w_packed, cos, sin)


def _rope_tables(past, t_len):
    half = HEAD_DIM // 2
    inv = ROPE_THETA ** (-jnp.arange(half, dtype=F32) / half)
    ang = (past + jnp.arange(t_len)).astype(F32)[:, None] * inv[None, :]
    c, s = jnp.cos(ang), jnp.sin(ang)
    return jnp.concatenate([c, c, c, c], axis=1), jnp.concatenate([-s, s, -s, s], axis=1)


def _dsa_kernel(aq_ref, iq_ref, aviw_ref, kik_ref, v_ref, tri_ref, o_ref,
                skey_ref, half_ref, iqs_ref, iwb_ref, qs_ref, p_ref, m_ref, mlane_ref, acc_ref,
                *, past, qb, kt, topk):
    i = pl.program_id(1)
    q0 = past + i * qb
    n_tiles = (q0 + qb + kt - 1) // kt
    row = lax.broadcasted_iota(jnp.int32, (qb, 1), 0)
    vis_end = ((q0 + row) // CHUNK + 1) * CHUNK
    lane_kt = lax.broadcasted_iota(jnp.int32, (1, kt), 1)

    for h in range(IDX_HEADS):
        iqs_ref[h * qb:(h + 1) * qb, :] = iq_ref[:, h * 64:(h + 1) * 64].astype(BF16)
        iwb_ref[h] = jnp.broadcast_to(aviw_ref[:, 64 + h:65 + h], (qb, LANES))

    def score_tile(j, carry):
        off = pl.multiple_of(j * kt, kt)
        ik = kik_ref[pl.ds(off, kt), 64:128]
        s = lax.dot_general(iqs_ref[...], ik, NT_DIMS, preferred_element_type=F32)
        parts = []
        for c in range(kt // LANES):
            a = jnp.zeros((qb, LANES), F32)
            for h in range(IDX_HEADS):
                a = a + iwb_ref[h] * jnp.maximum(s[h * qb:(h + 1) * qb, c * LANES:(c + 1) * LANES], 0.0)
            parts.append(a)
        acc = jnp.concatenate(parts, axis=1)
        acc = acc + 0.0
        acc = jnp.where(off + lane_kt < vis_end, acc, -jnp.inf)
        bits = pltpu.bitcast(acc, jnp.int32)
        key = jnp.where(bits < 0, bits ^ 0x7FFFFFFF, bits)
        skey_ref[:, pl.ds(off, kt)] = key
        half_ref[:, pl.ds(off, kt)] = jnp.right_shift(key, 16).astype(jnp.int16)
        return carry

    lax.fori_loop(0, n_tiles, score_tile, 0)

    def count(pred_fn):
        def body(j, cnt):
            off = pl.multiple_of(j * kt, kt)
            hit = pred_fn(skey_ref[:, pl.ds(off, kt)])
            for c in range(kt // LANES):
                cnt = cnt + jnp.where(hit[:, c * LANES:(c + 1) * LANES], 1.0, 0.0)
            return cnt
        cnt = lax.fori_loop(0, n_tiles, body, jnp.zeros((qb, LANES), F32))
        return jnp.sum(cnt, axis=1, keepdims=True)

    def count_half_ge(cand):
        cand16 = jnp.broadcast_to(cand, (qb, LANES)).astype(jnp.int16)
        one = jnp.ones((qb, LANES), jnp.int16)
        nil = jnp.zeros((qb, LANES), jnp.int16)

        def body(j, cnt):
            off = pl.multiple_of(j * kt, kt)
            tile = half_ref[:, pl.ds(off, kt)]
            for c in range(kt // LANES):
                cnt = cnt + jnp.where(tile[:, c * LANES:(c + 1) * LANES] >= cand16, one, nil)
            return cnt
        cnt = lax.fori_loop(0, n_tiles, body, nil)
        return jnp.sum(cnt.astype(F32), axis=1, keepdims=True)

    def bisect16(cnt_min, extra):
        zero = jnp.zeros((qb, 1), jnp.int32)
        cnt = extra + count_half_ge(zero)
        ok = cnt >= topk
        start = (jnp.where(ok, zero, jnp.full((qb, 1), -(2 ** 15), jnp.int32)), jnp.where(ok, cnt, cnt_min))

        def bit_step(it, c):
            t, cnt_t = c
            cand = t | jnp.left_shift(jnp.int32(1), 14 - it)
            cnt = extra + count_half_ge(cand)
            ok = cnt >= topk
            return jnp.where(ok, cand, t), jnp.where(ok, cnt, cnt_t)
        return lax.fori_loop(0, 15, bit_step, start)

    visited = jnp.full((qb, 1), n_tiles * kt, jnp.int32).astype(F32)
    t_hi, cnt_hi = bisect16(visited, 0.0)
    top16 = 2 ** 15 - 1
    above = jnp.where(t_hi == top16, 0.0, count_half_ge(jnp.minimum(t_hi + 1, top16)))

    def low_tile(j, carry):
        off = pl.multiple_of(j * kt, kt)
        key = skey_ref[:, pl.ds(off, kt)]
        low = (key & 0xFFFF) - 2 ** 15
        half_ref[:, pl.ds(off, kt)] = jnp.where(jnp.right_shift(key, 16) == t_hi, low, -(2 ** 15)).astype(jnp.int16)
        return carry

    lax.fori_loop(0, n_tiles, low_tile, 0)
    t_lo, cnt_ge = bisect16(cnt_hi, above)
    thr = jnp.left_shift(t_hi, 16) | (t_lo + 2 ** 15)
    lane_ok = lambda off: off + lane_kt < vis_end

    def select_all_ties(j, carry):
        off = pl.multiple_of(j * kt, kt)
        sel = (skey_ref[:, pl.ds(off, kt)] >= thr) & lane_ok(off)
        skey_ref[:, pl.ds(off, kt)] = pltpu.bitcast(jnp.where(sel, 0.0, -jnp.inf), jnp.int32)
        return carry

    def select_ranked_ties(need):
        def body(j, eq_seen):
            off = pl.multiple_of(j * kt, kt)
            key = skey_ref[:, pl.ds(off, kt)]
            eq = key == thr
            rank = _dot(jnp.where(eq, 1.0, 0.0).astype(BF16), tri_ref[...]) + eq_seen
            sel = ((key > thr) | (eq & (rank <= need))) & lane_ok(off)
            skey_ref[:, pl.ds(off, kt)] = pltpu.bitcast(jnp.where(sel, 0.0, -jnp.inf), jnp.int32)
            return rank[:, kt - 1:kt]
        return body

    def exact_fit():
        lax.fori_loop(0, n_tiles, select_all_ties, 0)

    def surplus_ties():
        need = topk - count(lambda x: x > thr)
        lax.fori_loop(0, n_tiles, select_ranked_ties(need), jnp.zeros((qb, 1), F32))

    lax.cond(jnp.max(jnp.abs(cnt_ge - topk)) == 0.0, exact_fit, surplus_ties)

    for h in range(A_HEADS):
        qs_ref[h * qb:(h + 1) * qb, :] = (aq_ref[:, h * 64:(h + 1) * 64] * (HEAD_DIM ** -0.5)).astype(BF16)
    mlane_ref[...] = jnp.full(mlane_ref.shape, -jnp.inf, F32)
    acc_ref[...] = jnp.zeros(acc_ref.shape, F32)

    def max_tile(j, carry):
        off = pl.multiple_of(j * kt, kt)
        bias = pltpu.bitcast(skey_ref[:, pl.ds(off, kt)], F32)
        k = kik_ref[pl.ds(off, kt), 0:64]
        s = lax.dot_general(qs_ref[...], k, NT_DIMS, preferred_element_type=F32)
        for h in range(A_HEADS):
            sh = s[h * qb:(h + 1) * qb, :] + bias
            mm = sh[:, 0:LANES]
            for c in range(1, kt // LANES):
                mm = jnp.maximum(mm, sh[:, c * LANES:(c + 1) * LANES])
            mlane_ref[h] = jnp.maximum(mlane_ref[h], mm)
        return carry

    lax.fori_loop(0, n_tiles, max_tile, 0)
    for h in range(A_HEADS):
        m_ref[h * qb:(h + 1) * qb, :] = jnp.max(mlane_ref[h], axis=1, keepdims=True)

    def pv_tile(j, carry):
        off = pl.multiple_of(j * kt, kt)
        bias = pltpu.bitcast(skey_ref[:, pl.ds(off, kt)], F32)
        k = kik_ref[pl.ds(off, kt), 0:64]
        s = lax.dot_general(qs_ref[...], k, NT_DIMS, preferred_element_type=F32)
        for h in range(A_HEADS):
            rows = slice(h * qb, (h + 1) * qb)
            p_ref[rows, :] = jnp.exp((s[rows, :] + bias) - m_ref[rows, :]).astype(BF16)
        acc_ref[...] += _dot(p_ref[...], v_ref[pl.ds(off, kt), :])
        return carry

    lax.fori_loop(0, n_tiles, pv_tile, 0)
    for h in range(A_HEADS):
        a = acc_ref[h * qb:(h + 1) * qb, :]
        o_ref[:, h * 64:(h + 1) * 64] = (a / a[:, 64:65])[:, 0:64]


def _dsa(aq, iq, aviw, kik_all, v_all, bsz, t_len, past, kt):
    qb = min(DSA_QUERY_ROWS, t_len)
    nq = t_len // qb
    lp = kik_all.shape[1]
    topk = min(TOPK_MAX, (past + t_len) // 4)
    tri = jnp.triu(jnp.ones((kt, kt), F32)).astype(BF16)
    qrow = lambda w: pl.BlockSpec((qb, w), lambda b, i: (b * nq + i, 0))
    keys = pl.BlockSpec((None, lp, LANES), lambda b, i: (b, 0, 0))
    return pl.pallas_call(
        functools.partial(_dsa_kernel, past=past, qb=qb, kt=kt, topk=topk),
        grid=(bsz, nq),
        in_specs=[qrow(512), qrow(512), qrow(LANES), keys, keys, pl.BlockSpec((kt, kt), lambda b, i: (0, 0))],
        out_specs=qrow(512),
        out_shape=jax.ShapeDtypeStruct((bsz * t_len, 512), F32),
        scratch_shapes=[
            pltpu.VMEM((qb, lp), jnp.int32),
            pltpu.VMEM((qb, lp), jnp.int16),
            pltpu.VMEM((IDX_HEADS * qb, 64), BF16),
            pltpu.VMEM((IDX_HEADS, qb, LANES), F32),
            pltpu.VMEM((A_HEADS * qb, 64), BF16),
            pltpu.VMEM((A_HEADS * qb, kt), BF16),
            pltpu.VMEM((A_HEADS * qb, 1), F32),
            pltpu.VMEM((A_HEADS, qb, LANES), F32),
            pltpu.VMEM((A_HEADS * qb, LANES), F32),
        ],
        compiler_params=_params("parallel", "arbitrary"),
        name="dsa",
    )(aq, iq, aviw, kik_all, v_all, tri)


def _retention_kernel(bqk_ref, bv_ref, bg_ref, s0_ref, gn_ref, o_ref, s_out_ref, s_ref, *, n_chunks):
    t = pl.program_id(0)

    @pl.when(t == 0)
    def _():
        s_ref[...] = s0_ref[...]

    n = CHUNK
    ri = lax.broadcasted_iota(jnp.int32, (n, n), 0).astype(F32)
    ci = lax.broadcasted_iota(jnp.int32, (n, n), 1).astype(F32)
    diff = ri - ci
    pos = lax.broadcasted_iota(jnp.int32, (n, 1), 0).astype(F32)
    decays = []
    for h in range(B_HEADS):
        log_g = math.log(1.0 - 2.0 ** (-5.0 - h))
        decays.append((jnp.where(diff >= 0, jnp.exp(jnp.maximum(diff, 0.0) * log_g), 0.0),
                       jnp.exp((pos + 1.0) * log_g), jnp.exp((n - 1.0 - pos) * log_g), math.exp(n * log_g)))
    for c in range(n_chunks):
        rows = slice(c * n, (c + 1) * n)
        for b in range(bqk_ref.shape[0]):
            for h in range(B_HEADS):
                intra, q_decay, k_decay, chunk_decay = decays[h]
                q = bqk_ref[b, rows, h * B_DK:(h + 1) * B_DK]
                k = bqk_ref[b, rows, 256 + h * B_DK:256 + (h + 1) * B_DK]
                v = bv_ref[b, rows, h * B_DV:(h + 1) * B_DV]
                s = s_ref[b, h]
                scores = _dot_f32x3(q, k, NT_DIMS) * intra
                o = _dot_f32x3(scores, v) + _dot_f32x3(q, s) * q_decay
                s_ref[b, h] = s * chunk_decay + _dot_f32x3(k * k_decay, v, TN_DIMS)
                mu = jnp.mean(o, axis=-1, keepdims=True)
                oc = o - mu
                var = jnp.mean(oc * oc, axis=-1, keepdims=True)
                g = bg_ref[b, rows, h * B_DV:(h + 1) * B_DV]
                o_ref[b, rows, h * B_DV:(h + 1) * B_DV] = (
                    oc * lax.rsqrt(var + LN_EPS) * gn_ref[:, h * B_DV:(h + 1) * B_DV] * (g * _sigmoid(g)))

    @pl.when(t == pl.num_programs(0) - 1)
    def _():
        s_out_ref[...] = s_ref[...]


def _retention(bqk, bv, bg, s0, b_gn, bsz, t_len, tt):
    row = pl.BlockSpec((bsz, tt, 512), lambda t: (0, t, 0))
    state = pl.BlockSpec((bsz, B_HEADS, B_DK, B_DV), lambda t: (0, 0, 0, 0))
    o_b, s_new = pl.pallas_call(
        functools.partial(_retention_kernel, n_chunks=tt // CHUNK),
        grid=(t_len // tt,),
        in_specs=[row, row, row, state, pl.BlockSpec((1, 512), lambda t: (0, 0))],
        out_specs=[row, state],
        out_shape=[jax.ShapeDtypeStruct((bsz, t_len, 512), F32),
                   jax.ShapeDtypeStruct((bsz, B_HEADS, B_DK, B_DV), F32)],
        scratch_shapes=[pltpu.VMEM((bsz, B_HEADS, B_DK, B_DV), F32)],
        compiler_params=_params("arbitrary"),
        name="retention",
    )(bqk.reshape(bsz, t_len, 512), bv.reshape(bsz, t_len, 512), bg.reshape(bsz, t_len, 512), s0, b_gn)
    return o_b.reshape(bsz * t_len, 512), s_new


def _out_ln_kernel(oa_ref, ob_ref, x_ref, w_ref, g_ref, b_ref, y_ref):
    y = _dot(oa_ref[...].astype(BF16), w_ref[0:512, :]) + _dot(ob_ref[...].astype(BF16), w_ref[512:1024, :])
    y_ref[...] = _layernorm_rows(ALPHA * x_ref[...] + y, g_ref[...], b_ref[...])


def _out_ln(oa, ob, x2, w_out, g, b, tm):
    n = x2.shape[0]
    row = lambda w: pl.BlockSpec((tm, w), lambda i: (i, 0))
    vec = pl.BlockSpec((1, D_MODEL), lambda i: (0, 0))
    return pl.pallas_call(
        _out_ln_kernel,
        grid=(n // tm,),
        in_specs=[row(512), row(512), row(D_MODEL), pl.BlockSpec((D_MODEL, D_MODEL), lambda i: (0, 0)), vec, vec],
        out_specs=row(D_MODEL),
        out_shape=jax.ShapeDtypeStruct((n, D_MODEL), F32),
        compiler_params=_params("parallel"),
        name="out_ln",
    )(oa, ob, x2, w_out, g, b)


def _out_ln_d_kernel(oc_ref, od_ref, r_ref, kr_ref, v_ref, gate_ref, x_ref, w_ref, hm_ref,
                     lnx_g_ref, lnx_b_ref, rk_ref, g_ref, b_ref, y_ref):
    hm = hm_ref[...]
    o = od_ref[...]
    mu = _dot_exact_rhs(o, hm) * (1.0 / HEAD_DIM)
    oc = o - mu
    var = _dot_exact_rhs(oc * oc, hm) * (1.0 / HEAD_DIM)
    normed = oc * lax.rsqrt(var + D_GN_EPS) * lnx_g_ref[...] + lnx_b_ref[...]
    v = v_ref[...]
    bonus = _dot_exact_rhs(r_ref[...] * kr_ref[...] * rk_ref[...], hm) * v
    od = (normed + bonus) * gate_ref[...]
    y = _dot(oc_ref[...].astype(BF16), w_ref[0:512, :]) + _dot(od.astype(BF16), w_ref[512:1024, :])
    y_ref[...] = _layernorm_rows(ALPHA * x_ref[...] + y, g_ref[...], b_ref[...])


def _head_mask():
    head = jnp.arange(D_W) // HEAD_DIM
    return (head[:, None] == head[None, :]).astype(BF16)


def _out_ln_d(oc, od, r, kr, v, gate, x2, w_out, lnx_g, lnx_b, r_k, g, b, tm):
    n = x2.shape[0]
    row = lambda w: pl.BlockSpec((tm, w), lambda i: (i, 0))
    vec = lambda w: pl.BlockSpec((1, w), lambda i: (0, 0))
    return pl.pallas_call(
        _out_ln_d_kernel,
        grid=(n // tm,),
        in_specs=[row(512)] * 6 + [row(D_MODEL), pl.BlockSpec((D_MODEL, D_MODEL), lambda i: (0, 0)),
                                   pl.BlockSpec((D_W, D_W), lambda i: (0, 0)),
                                   vec(D_W), vec(D_W), vec(D_W), vec(D_MODEL), vec(D_MODEL)],
        out_specs=row(D_MODEL),
        out_shape=jax.ShapeDtypeStruct((n, D_MODEL), F32),
        compiler_params=_params("parallel"),
        name="out_ln_d",
    )(oc, od, r, kr, v, gate, x2, w_out, _head_mask(), lnx_g, lnx_b, r_k, g, b)


MOE_EXPERTS_PER_STEP = 4


def _moe_kernel(x_ref, rw_ref, rb_ref, w13_ref, w2_ref, g_ref, b_ref, y_ref, acc_ref, gate_ref, xb_ref):
    e = pl.program_id(1)
    tm = x_ref.shape[0]
    lane = lax.broadcasted_iota(jnp.int32, (tm, N_EXPERTS), 1)

    lane_f = lane.astype(F32)

    def first_argmax(v):
        m = jnp.max(v, axis=1, keepdims=True)
        idx = jnp.min(jnp.where(v == m, lane_f, float(N_EXPERTS)), axis=1, keepdims=True)
        return m, idx.astype(jnp.int32)

    @pl.when(e == 0)
    def _():
        x = x_ref[...]
        xb_ref[...] = x.astype(BF16)
        aff = _sigmoid(_dot_hi(x, rw_ref[...]))
        biased = aff + rb_ref[...]
        best = jnp.zeros((tm, 1), jnp.int32)
        best_score = jnp.full((tm, 1), -jnp.inf, F32)
        for grp in range(N_GROUPS):
            vg = jnp.where(lane // EXPERTS_PER_GROUP == grp, biased, -jnp.inf)
            top1, idx1 = first_argmax(vg)
            top2, _ = first_argmax(jnp.where(lane == idx1, -jnp.inf, vg))
            score = top1 + top2
            better = score > best_score
            best = jnp.where(better, grp, best)
            best_score = jnp.where(better, score, best_score)
        masked = jnp.where(lane // EXPERTS_PER_GROUP == best, biased, -jnp.inf)
        _, idx1 = first_argmax(masked)
        _, idx2 = first_argmax(jnp.where(lane == idx1, -jnp.inf, masked))
        top_aff = jnp.where((lane == idx1) | (lane == idx2), aff, 0.0)
        gate_ref[...] = top_aff / jnp.sum(top_aff, axis=1, keepdims=True)
        acc_ref[...] = jnp.zeros(acc_ref.shape, F32)

    for k in range(MOE_EXPERTS_PER_STEP):
        h13 = _dot(xb_ref[...], w13_ref[k])
        h1 = h13[:, 0:D_EXPERT]
        h = (h1 * _sigmoid(h1)) * h13[:, D_EXPERT:2 * D_EXPERT]
        gate_e = jnp.sum(jnp.where(lane == e * MOE_EXPERTS_PER_STEP + k, gate_ref[...], 0.0), axis=1, keepdims=True)
        acc_ref[...] += gate_e * _dot(h.astype(BF16), w2_ref[k])

    @pl.when(e == pl.num_programs(1) - 1)
    def _():
        y_ref[...] = _layernorm_rows(ALPHA * x_ref[...] + acc_ref[...], g_ref[...], b_ref[...])


def _moe_ln(x2, router_w, router_b, w13, w2, g, b, tm):
    n = x2.shape[0]
    row = pl.BlockSpec((tm, D_MODEL), lambda i, e: (i, 0))
    vec = pl.BlockSpec((1, D_MODEL), lambda i, e: (0, 0))
    return pl.pallas_call(
        _moe_kernel,
        grid=(n // tm, N_EXPERTS // MOE_EXPERTS_PER_STEP),
        in_specs=[row,
                  pl.BlockSpec((D_MODEL, N_EXPERTS), lambda i, e: (0, 0)),
                  pl.BlockSpec((1, N_EXPERTS), lambda i, e: (0, 0)),
                  pl.BlockSpec((MOE_EXPERTS_PER_STEP, D_MODEL, 2 * D_EXPERT), lambda i, e: (e, 0, 0)),
                  pl.BlockSpec((MOE_EXPERTS_PER_STEP, D_EXPERT, D_MODEL), lambda i, e: (e, 0, 0)),
                  vec, vec],
        out_specs=row,
        out_shape=jax.ShapeDtypeStruct((n, D_MODEL), F32),
        scratch_shapes=[pltpu.VMEM((tm, D_MODEL), F32), pltpu.VMEM((tm, N_EXPERTS), F32),
                        pltpu.VMEM((tm, D_MODEL), BF16)],
        compiler_params=_params("parallel", "arbitrary"),
        name="moe_ln",
    )(x2, router_w, router_b, w13, w2, g, b)


def _proj_cd_kernel(x_ref, w_ref, cq_ref, ck_ref, cv_ref, pd_ref, ck16_ref, cv16_ref):
    xb = x_ref[...].astype(BF16)
    cq_ref[...] = _dot(xb, w_ref[:, 0:512])
    ck = _dot(xb, w_ref[:, 512:1024])
    ck_ref[...] = ck
    ck16_ref[...] = ck.astype(BF16)
    cv = _dot(xb, w_ref[:, 1024:1536])
    cv_ref[...] = cv
    cv16_ref[...] = cv.astype(BF16)
    pd_ref[...] = _dot(xb, w_ref[:, 1536:1536 + D_SHIFT_W])


def _proj_cd(x2, w_bf16, tm):
    n = x2.shape[0]
    row = lambda w: pl.BlockSpec((tm, w), lambda i: (i, 0))
    outs = [512, 512, 512, D_SHIFT_W]
    return pl.pallas_call(
        _proj_cd_kernel,
        grid=(n // tm,),
        in_specs=[row(D_MODEL), pl.BlockSpec((D_MODEL, 1536 + D_SHIFT_W), lambda i: (0, 0))],
        out_specs=[row(w) for w in outs] + [row(512), row(512)],
        out_shape=[jax.ShapeDtypeStruct((n, w), F32) for w in outs] + [jax.ShapeDtypeStruct((n, 512), BF16)] * 2,
        compiler_params=_params("parallel"),
        name="proj_cd",
    )(x2, w_bf16)


def _stick_kernel(q_ref, k_ref, v_ref, ust_ref, o_ref, *, past, tq, kb):
    i = pl.program_id(2)
    q0 = past + i * tq
    n_kb = (q0 + tq + kb - 1) // kb
    qpos = q0 + lax.broadcasted_iota(jnp.int32, (tq, 1), 0)
    lane_kb = lax.broadcasted_iota(jnp.int32, (1, kb), 1)
    n_heads = LANES // HEAD_DIM
    cols = [slice(hh * HEAD_DIM, (hh + 1) * HEAD_DIM) for hh in range(n_heads)]
    qs = [(q_ref[:, c] * (HEAD_DIM ** -0.5)).astype(BF16) for c in cols]

    def cond(c):
        j, carries, _ = c
        worst = carries[0]
        for carry in carries[1:]:
            worst = jnp.maximum(worst, carry)
        return jnp.logical_and(j >= 0, jnp.max(worst) > EXP_ZERO_BELOW)

    def body(c):
        j, carries, outs = c
        off = pl.multiple_of(j * kb, kb)
        strict = off + lane_kb < qpos
        new_carries, new_outs = [], []
        for hh in range(n_heads):
            k = k_ref[pl.ds(off, kb), cols[hh]]
            v = v_ref[pl.ds(off, kb), cols[hh]]
            z = lax.dot_general(qs[hh], k, NT_DIMS, preferred_element_type=F32)
            sp = _softplus(z)
            log_keep = jnp.where(strict, -sp, 0.0)
            later = carries[hh] + _dot_exact_rhs2(log_keep, ust_ref[...])
            a = jnp.where(strict, jnp.exp((z - sp) + later), 0.0)
            new_outs.append(outs[hh] + _dot(a.astype(BF16), v))
            new_carries.append(carries[hh] + jnp.sum(log_keep, axis=1, keepdims=True))
        return j - 1, tuple(new_carries), tuple(new_outs)

    init = (n_kb - 1, tuple(jnp.zeros((tq, 1), F32) for _ in cols), tuple(jnp.zeros((tq, HEAD_DIM), F32) for _ in cols))
    _, _, outs = lax.while_loop(cond, body, init)
    for hh in range(n_heads):
        o_ref[:, cols[hh]] = outs[hh]


def _stick(cq, k_all, v_all, bsz, t_len, past, kb):
    tq = min(kb, t_len)
    nq = t_len // tq
    lp = k_all.shape[1]
    ust = jnp.tril(jnp.ones((kb, kb), F32), -1).astype(BF16)
    qrow = pl.BlockSpec((tq, LANES), lambda b, hp, i: (b * nq + i, hp))
    keys = pl.BlockSpec((None, lp, LANES), lambda b, hp, i: (b, 0, hp))
    return pl.pallas_call(
        functools.partial(_stick_kernel, past=past, tq=tq, kb=kb),
        grid=(bsz, C_HEADS * HEAD_DIM // LANES, nq),
        in_specs=[qrow, keys, keys, pl.BlockSpec((kb, kb), lambda b, hp, i: (0, 0))],
        out_specs=qrow,
        out_shape=jax.ShapeDtypeStruct((bsz * t_len, 512), F32),
        compiler_params=_params("parallel", "parallel", "arbitrary"),
        name="stick",
    )(cq, k_all, v_all, ust)


def _rwkv_pre_kernel(pd_ref, shift_ref, mu_ref, w0a0_ref, wa2_ref, gw2_ref, kk_ref_, ka_ref, hm_ref,
                     r_out, w_out, kr_out, v_out, kk_out, b_out, g_out, last_ref):
    t = pl.program_id(1)

    @pl.when(t == 0)
    def _():
        last_ref[...] = shift_ref[...]

    pd = pd_ref[...]
    tm = pd.shape[0]
    rolled = pltpu.roll(pd, 1, 0)
    first_row = lax.broadcasted_iota(jnp.int32, (tm, 1), 0) == 0
    prev = jnp.where(first_row, last_ref[...], rolled)
    last_ref[...] = pd[tm - 1:tm, :]
    pm = pd + (prev - pd) * mu_ref[...]
    r = pm[:, 0:512]
    k = pm[:, 512:1024]
    v = pm[:, 1024:1536]
    lwa = pm[:, 1536:1664]
    lg = pm[:, 1664:1792]
    lane = lax.broadcasted_iota(jnp.int32, lwa.shape, 1)
    lwa = jnp.where(lane < D_LORA_W, jnp.tanh(lwa), lwa)
    pre = w0a0_ref[...] + _dot_hi(lwa, wa2_ref[...])
    w_log = -_softplus(-pre[:, 0:512]) - 0.5
    decay = jnp.exp(-jnp.exp(w_log))
    a = _sigmoid(pre[:, 512:1024])
    g = _dot_hi(_sigmoid(lg), gw2_ref[...])
    kk = k * kk_ref_[...]
    kk = kk * lax.rsqrt(_dot_exact_rhs(kk * kk, hm_ref[...]) + 1e-12)
    r_out[...] = r
    w_out[...] = decay
    kr_out[...] = k * (1.0 + (a - 1.0) * ka_ref[...])
    v_out[...] = v
    kk_out[...] = kk
    b_out[...] = kk * a
    g_out[...] = g


def _rwkv_pre(pd, shift, mu, w0a0, wa2, gw2, k_k, k_a, bsz, t_len, tm):
    nt = t_len // tm
    row = lambda w: pl.BlockSpec((tm, w), lambda b, t: (b * nt + t, 0))
    const = lambda s: pl.BlockSpec(s, lambda b, t: (0,) * len(s))
    return pl.pallas_call(
        _rwkv_pre_kernel,
        grid=(bsz, nt),
        in_specs=[row(D_SHIFT_W), pl.BlockSpec((None, 1, D_SHIFT_W), lambda b, t: (b, 0, 0)),
                  const((1, D_SHIFT_W)), const((1, 2 * D_W)), const((LANES, 2 * D_W)), const((D_LORA_G, D_W)),
                  const((1, D_W)), const((1, D_W)), const((D_W, D_W))],
        out_specs=[row(D_W)] * 7,
        out_shape=[jax.ShapeDtypeStruct((bsz * t_len, D_W), F32)] * 7,
        scratch_shapes=[pltpu.VMEM((1, D_SHIFT_W), F32)],
        compiler_params=_params("parallel", "arbitrary"),
        name="rwkv_pre",
    )(pd, shift, mu, w0a0, wa2, gw2, k_k, k_a, _head_mask())


RWKV_KL = HEAD_DIM // 2


def _rwkv_scan_kernel(kk_ref, w_ref, b_ref, kr_ref, r_ref, v_ref, s0_ref, o_ref, s_out_ref, s_ref, *, tb):
    g = pl.program_id(0)

    @pl.when(g == 0)
    def _():
        s_ref[...] = s0_ref[...]

    def both_halves(x):
        return x + pltpu.roll(x, LANES // 2, 1)

    def step(t, carry):
        for vrows in (slice(0, HEAD_DIM // 2), slice(HEAD_DIM // 2, HEAD_DIM)):
            acc = [jnp.zeros((HEAD_DIM // 2, LANES), F32) for _ in range(2)]
            for kl in range(RWKV_KL):
                acc[kl % 2] = acc[kl % 2] + s_ref[kl, vrows, :] * kk_ref[t, kl:kl + 1, :]
            sa = both_halves(acc[0] + acc[1])
            v = v_ref[t, vrows, :]
            out = [jnp.zeros((HEAD_DIM // 2, LANES), F32) for _ in range(2)]
            for kl in range(RWKV_KL):
                row = slice(kl, kl + 1)
                s = s_ref[kl, vrows, :] * w_ref[t, row, :] - sa * b_ref[t, row, :] + v * kr_ref[t, row, :]
                s_ref[kl, vrows, :] = s
                out[kl % 2] = out[kl % 2] + s * r_ref[t, row, :]
            o_ref[t, vrows, :] = both_halves(out[0] + out[1])
        return carry

    lax.fori_loop(0, tb, step, 0)

    @pl.when(g == pl.num_programs(0) - 1)
    def _():
        s_out_ref[...] = s_ref[...]


def _rwkv_scan(kk, w, b, kr, r, v, s0, t_len, tb):
    keyed = pl.BlockSpec((tb, RWKV_KL, LANES), lambda g: (g, 0, 0))
    valued = pl.BlockSpec((tb, HEAD_DIM, LANES), lambda g: (g, 0, 0))
    state = pl.BlockSpec((RWKV_KL, HEAD_DIM, LANES), lambda g: (0, 0, 0))
    return pl.pallas_call(
        functools.partial(_rwkv_scan_kernel, tb=tb),
        grid=(t_len // tb,),
        in_specs=[keyed] * 5 + [valued, state],
        out_specs=[valued, state],
        out_shape=[jax.ShapeDtypeStruct((t_len, HEAD_DIM, LANES), F32),
                   jax.ShapeDtypeStruct((RWKV_KL, HEAD_DIM, LANES), F32)],
        scratch_shapes=[pltpu.VMEM((RWKV_KL, HEAD_DIM, LANES), F32)],
        compiler_params=_params("arbitrary"),
        name="rwkv_scan",
    )(kk, w, b, kr, r, v, s0)


def _to_scan_keyed(x2, bsz, t_len):
    y = x2.reshape(bsz, t_len, D_HEADS, 2, RWKV_KL).transpose(1, 4, 3, 0, 2)
    return y.reshape(t_len, RWKV_KL, 2 * bsz * D_HEADS)


def _to_scan_valued(x2, bsz, t_len):
    y = x2.reshape(bsz, t_len, D_HEADS, HEAD_DIM).transpose(1, 3, 0, 2).reshape(t_len, HEAD_DIM, bsz * D_HEADS)
    return jnp.concatenate([y, y], axis=-1)


def _from_scan_valued(y, bsz, t_len):
    y = y[:, :, 0:bsz * D_HEADS].reshape(t_len, HEAD_DIM, bsz, D_HEADS)
    return y.transpose(2, 0, 3, 1).reshape(bsz * t_len, D_W)


def _state_to_scan(s, bsz):
    y = s.reshape(bsz, D_HEADS, HEAD_DIM, 2, RWKV_KL).transpose(4, 2, 3, 0, 1)
    return y.reshape(RWKV_KL, HEAD_DIM, 2 * bsz * D_HEADS)


def _state_from_scan(y, bsz):
    return y.reshape(RWKV_KL, HEAD_DIM, 2, bsz, D_HEADS).transpose(3, 4, 1, 2, 0).reshape(
        bsz, D_HEADS, HEAD_DIM, HEAD_DIM)


def _pad_keys(x, mult):
    pad = (-x.shape[1]) % mult
    return x if pad == 0 else jnp.pad(x, ((0, 0), (0, pad), (0, 0)))


def _tile(n, cap):
    return min(n, cap)


def _run_group(x, past, p):
    bsz, t_len, _ = x.shape
    assert bsz * D_HEADS * 2 == LANES, "rwkv scan packs (key half, batch, head) into the lane axis"
    n = bsz * t_len
    x2 = x.reshape(n, D_MODEL)
    past_len = 0 if past is None else past[0].shape[2]
    tm = _tile(t_len, 512)

    cos, sin = _rope_tables(past_len, t_len)
    aq, iq, bqk, kik, aviw, bv, bg, kik16, v16 = _proj_ab(x2, p['w_in_ab'], cos, sin, t_len, tm)
    ak = kik[:, 0:64].reshape(1, bsz, t_len, 64)
    ik = kik[:, 64:128].reshape(1, bsz, t_len, 64)
    av = aviw[:, 0:64].reshape(1, bsz, t_len, 64)
    kik_all = kik16.reshape(bsz, t_len, LANES)
    v_all = v16.reshape(bsz, t_len, LANES)
    ones_col = (jnp.arange(LANES) == 64).astype(F32)
    if past is None:
        s_b = jnp.zeros((bsz, B_HEADS, B_DK, B_DV), F32)
    else:
        pk, pv, pik, sb = past[0][0], past[1][0], past[2][0], past[3][0]
        kik_all = jnp.concatenate([jnp.concatenate([pk, pik], axis=-1).astype(BF16), kik_all], axis=1)
        pv_slab = jnp.concatenate([pv, jnp.broadcast_to(ones_col[64:], pv.shape)], axis=-1)
        v_all = jnp.concatenate([pv_slab.astype(BF16), v_all], axis=1)
        s_b = sb
    kt = 1024
    o_a = _dsa(aq, iq, aviw, _pad_keys(kik_all, kt), _pad_keys(v_all, kt), bsz, t_len, past_len, kt)
    o_b, s_b_new = _retention(bqk, bv, bg, s_b, p['b_gn'], bsz, t_len, _tile(t_len, 2 * CHUNK))
    x2 = _out_ln(o_a, o_b, x2, p['w_out_ab'], p['ln_g'][0, 0][None], p['ln_b'][0, 0][None], tm)
    tm_moe = _tile(n, 1024)
    x2 = _moe_ln(x2, p['router_w'], p['router_b'], p['w13'][0], p['w2'][0],
                 p['ln_g'][0, 1][None], p['ln_b'][0, 1][None], tm_moe)

    cq, ck, cv, pd, ck16, cv16 = _proj_cd(x2, p['w_in_cd'], tm)
    ck_all = ck16.reshape(bsz, t_len, 512)
    cv_all = cv16.reshape(bsz, t_len, 512)
    if past is None:
        s_d = jnp.zeros((bsz, D_HEADS, HEAD_DIM, HEAD_DIM), F32)
        shift = jnp.zeros((bsz, 1, D_SHIFT_W), F32)
    else:
        ck_all = jnp.concatenate([past[4][0].reshape(bsz, past_len, 512).astype(BF16), ck_all], axis=1)
        cv_all = jnp.concatenate([past[5][0].reshape(bsz, past_len, 512).astype(BF16), cv_all], axis=1)
        s_d, shift = past[6][0], past[7][0]
    kb = 256
    o_c = _stick(cq, _pad_keys(ck_all, kb), _pad_keys(cv_all, kb), bsz, t_len, past_len, kb)
    r, w, kr, v, kk, b, g = _rwkv_pre(pd, shift, p['d_mu'], p['d_w0a0'], p['d_wa2'], p['d_g2'],
                                      p['d_k_k'], p['d_k_a'], bsz, t_len, tm)
    keyed = [_to_scan_keyed(u, bsz, t_len) for u in (kk, w, b, kr, r)]
    o_scan, s_scan = _rwkv_scan(*keyed, _to_scan_valued(v, bsz, t_len), _state_to_scan(s_d, bsz),
                                t_len, _tile(t_len, 64))
    o_d = _from_scan_valued(o_scan, bsz, t_len)
    s_d_new = _state_from_scan(s_scan, bsz)
    x2 = _out_ln_d(o_c, o_d, r, kr, v, g, x2, p['w_out_cd'], p['d_lnx_g'], p['d_lnx_b'], p['d_r_k'],
                   p['ln_g'][1, 0][None], p['ln_b'][1, 0][None], tm)
    x2 = _moe_ln(x2, p['router_w'], p['router_b'], p['w13'][1], p['w2'][1],
                 p['ln_g'][1, 1][None], p['ln_b'][1, 1][None], tm_moe)

    states = (ak, av, ik, s_b_new[None],
              ck.reshape(1, bsz, t_len, C_HEADS, HEAD_DIM), cv.reshape(1, bsz, t_len, C_HEADS, HEAD_DIM),
              s_d_new[None], pd.reshape(bsz, t_len, D_SHIFT_W)[:, -1:][None])
    return x2.reshape(bsz, t_len, D_MODEL), states


def kernel(x_prompt, x_sample, cache_a_k, cache_a_v, cache_a_idx_k, state_b, cache_c_k, cache_c_v, state_d, state_d_shift, w_in_ab, w_out_ab, b_gn, w_in_cd, w_out_cd, d_mu, d_w0, d_w2, d_a0, d_a2, d_g2, d_k_k, d_k_a, d_r_k, d_lnx_g, d_lnx_b, ln_g, ln_b, router_w, router_b, moe_w1, moe_w3, moe_w2):
    zeros_w = jnp.zeros((D_LORA_W, D_W), F32)
    p = {
        'w_in_ab': _pack_w_in_ab(w_in_ab[0]),
        'w_out_ab': w_out_ab[0].astype(BF16),
        'b_gn': b_gn,
        'w_in_cd': w_in_cd[0].astype(BF16),
        'w_out_cd': w_out_cd[0].astype(BF16),
        'd_mu': d_mu,
        'd_w0a0': jnp.concatenate([d_w0, d_a0], axis=1),
        'd_wa2': jnp.concatenate([jnp.concatenate([d_w2[0], zeros_w], axis=1),
                                  jnp.concatenate([zeros_w, d_a2[0]], axis=1)], axis=0),
        'd_g2': d_g2[0],
        'd_k_k': d_k_k, 'd_k_a': d_k_a, 'd_r_k': d_r_k, 'd_lnx_g': d_lnx_g, 'd_lnx_b': d_lnx_b,
        'ln_g': ln_g, 'ln_b': ln_b,
        'router_w': router_w, 'router_b': router_b[None],
        'w13': jnp.concatenate([moe_w1, moe_w3], axis=-1).astype(BF16),
        'w2': moe_w2.astype(BF16),
    }
    y_p, sp = _run_group(x_prompt, None, p)
    past = (cache_a_k, cache_a_v, cache_a_idx_k, state_b, cache_c_k, cache_c_v, state_d, state_d_shift)
    y_s, ss = _run_group(x_sample, past, p)
    return (y_p, y_s, sp[0], sp[1], sp[2], ss[0], ss[1], ss[2], sp[3], ss[3], sp[4], sp[5], ss[4], ss[5],
            sp[6], ss[6], sp[7], ss[7])
```

```python
import functools
import math

import jax
import jax.numpy as jnp
import numpy as np
from jax import lax
from jax.experimental import pallas as pl
from jax.experimental.pallas import tpu as pltpu

F32 = jnp.float32
BF16 = jnp.bfloat16

D_MODEL = 1024
CHUNK = 64
DSA_QUERY_ROWS = 128
ROPE_THETA = 10000.0
HEAD_DIM = 64
LN_EPS = 1e-5
A_HEADS = 8
IDX_HEADS = 8
TOPK_MAX = 256
B_HEADS = 4
B_DK = 64
B_DV = 128
C_HEADS = 8
D_HEADS = 8
D_LORA_W = 64
D_LORA_A = 64
D_LORA_G = 128
D_GN_EPS = 64e-5
N_EXPERTS = 16
N_GROUPS = 4
EXPERTS_PER_GROUP = 4
D_EXPERT = 256
DEPTH = 2
ALPHA = (2 * DEPTH) ** 0.25
D_W = D_HEADS * HEAD_DIM
D_SHIFT_W = 3 * D_W + D_LORA_W + D_LORA_A + D_LORA_G

LANES = 128
SUBLANES = 8
VMEM_LIMIT_BYTES = 56 * 1024 * 1024

SHIFT_MARGIN = 1.001
MIN_SOFTMAX_SUM = 1e-30
EXP_ZERO_BELOW = -104.0

NT_DIMS = (((1,), (1,)), ((), ()))
TN_DIMS = (((0,), (0,)), ((), ()))


def _params(*sem):
    return pltpu.CompilerParams(dimension_semantics=sem, vmem_limit_bytes=VMEM_LIMIT_BYTES)


def _dot(a, b):
    return jnp.dot(a, b, preferred_element_type=F32)


def _dot_hi(a, b):
    return jnp.dot(a, b, preferred_element_type=F32, precision=lax.Precision.HIGHEST)


def _dot_f32x3(a, b, dims=(((1,), (0,)), ((), ()))):
    a_hi = a.astype(BF16)
    b_hi = b.astype(BF16)
    a_lo = (a - a_hi.astype(F32)).astype(BF16)
    b_lo = (b - b_hi.astype(F32)).astype(BF16)
    dot = lambda x, y: lax.dot_general(x, y, dims, preferred_element_type=F32)
    return dot(a_hi, b_hi) + dot(a_hi, b_lo) + dot(a_lo, b_hi)


def _split3(x):
    h1 = x.astype(BF16)
    r1 = x - h1.astype(F32)
    h2 = r1.astype(BF16)
    r2 = r1 - h2.astype(F32)
    return h1, h2, r2.astype(BF16)


def _dot_exact_rhs(x, m01):
    h1, h2, h3 = _split3(x)
    return _dot(h1, m01) + _dot(h2, m01) + _dot(h3, m01)


def _dot_exact_rhs2(x, m01):
    h1 = x.astype(BF16)
    h2 = (x - h1.astype(F32)).astype(BF16)
    return _dot(h1, m01) + _dot(h2, m01)


def _layernorm_rows(x, g, b):
    mu = jnp.mean(x, axis=-1, keepdims=True)
    xc = x - mu
    var = jnp.mean(xc * xc, axis=-1, keepdims=True)
    return xc * lax.rsqrt(var + LN_EPS) * g + b


def _sigmoid(x):
    return 1.0 / (1.0 + jnp.exp(-x))


def _softplus(x):
    return jnp.maximum(x, 0.0) + jnp.log1p(jnp.exp(-jnp.abs(x)))


def _rope_slab(x, cos, sin_signed):
    lane = lax.broadcasted_iota(jnp.int32, x.shape, 1)
    first_half = (lane % HEAD_DIM) < (HEAD_DIM // 2)
    swapped = jnp.where(first_half, pltpu.roll(x, LANES - HEAD_DIM // 2, 1), pltpu.roll(x, HEAD_DIM // 2, 1))
    return x * cos + swapped * sin_signed


AB_ROPED = 1664
AB_PACKED = 2816


def _pack_w_in_ab(w):
    aq, ak, av, iq, ik, iw, bq, bk, bv, bg = jnp.split(
        w, [512, 576, 640, 1152, 1216, 1224, 1480, 1736, 2248], axis=1)
    pad = jnp.zeros((w.shape[0], LANES - 64 - IDX_HEADS), w.dtype)
    return jnp.concatenate([aq, iq, bq, bk, ak, ik, av, iw, pad, bv, bg], axis=1).astype(BF16)


def _proj_ab_kernel(x_ref, w_ref, cos_ref, sin_ref, aq_ref, iq_ref, bqk_ref, kik_ref, aviw_ref, bv_ref, bg_ref,
                    kik16_ref, v16_ref):
    xb = x_ref[...].astype(BF16)
    cos = cos_ref[...]
    sin = sin_ref[...]

    def roped(col0, width, scale_from=None):
        y = _dot(xb, w_ref[:, col0:col0 + width])
        parts = []
        for c in range(width // LANES):
            slab = _rope_slab(y[:, c * LANES:(c + 1) * LANES], cos, sin)
            if scale_from is not None and c * LANES >= scale_from:
                slab = slab * (B_DK ** -0.5)
            parts.append(slab)
        return parts

    for c, slab in enumerate(roped(0, 512)):
        aq_ref[:, c * LANES:(c + 1) * LANES] = slab
    for c, slab in enumerate(roped(512, 512)):
        iq_ref[:, c * LANES:(c + 1) * LANES] = slab
    for c, slab in enumerate(roped(1024, 512, scale_from=256)):
        bqk_ref[:, c * LANES:(c + 1) * LANES] = slab
    kik = roped(1536, LANES)[0]
    kik_ref[...] = kik
    kik16_ref[...] = kik.astype(BF16)
    aviw = _dot(xb, w_ref[:, AB_ROPED:AB_ROPED + LANES])
    lane = lax.broadcasted_iota(jnp.int32, aviw.shape, 1)
    is_iw = (lane >= 64) & (lane < 64 + IDX_HEADS)
    aviw_ref[...] = jnp.where(is_iw, aviw * ((IDX_HEADS * HEAD_DIM) ** -0.5), aviw)
    v16_ref[...] = jnp.where(lane < 64, aviw, jnp.where(lane == 64, 1.0, 0.0)).astype(BF16)
    bv_ref[...] = _dot(xb, w_ref[:, 1792:2304])
    bg_ref[...] = _dot(xb, w_ref[:, 2304:2816])


def _proj_ab(x2, w_packed, cos, sin, t_len, tm):
    n = x2.shape[0]
    nt = t_len // tm
    row = lambda w: pl.BlockSpec((tm, w), lambda i: (i, 0))
    tab = pl.BlockSpec((tm, LANES), lambda i: (i % nt, 0))
    outs = [512, 512, 512, LANES, LANES, 512, 512]
    return pl.pallas_call(
        _proj_ab_kernel,
        grid=(n // tm,),
        in_specs=[row(D_MODEL), pl.BlockSpec((D_MODEL, AB_PACKED), lambda i: (0, 0)), tab, tab],
        out_specs=[row(w) for w in outs] + [row(LANES), row(LANES)],
        out_shape=[jax.ShapeDtypeStruct((n, w), F32) for w in outs] + [jax.ShapeDtypeStruct((n, LANES), BF16)] * 2,
        compiler_params=_params("parallel"),
        name="proj_ab",
    )(x2, w_packed, cos, sin)


def _rope_tables(past, t_len):
    half = HEAD_DIM // 2
    inv = ROPE_THETA ** (-jnp.arange(half, dtype=F32) / half)
    ang = (past + jnp.arange(t_len)).astype(F32)[:, None] * inv[None, :]
    c, s = jnp.cos(ang), jnp.sin(ang)
    return jnp.concatenate([c, c, c, c], axis=1), jnp.concatenate([-s, s, -s, s], axis=1)


def _dsa_kernel(aq_ref, iq_ref, aviw_ref, kik_ref, v_ref, tri_ref, o_ref,
                skey_ref, half_ref, iqs_ref, iwb_ref, qs_ref, p_ref, m_ref, mlane_ref, acc_ref, kmax_ref,
                *, past, qb, kt, topk):
    i = pl.program_id(1)
    q0 = past + i * qb
    n_tiles = (q0 + qb + kt - 1) // kt
    row = lax.broadcasted_iota(jnp.int32, (qb, 1), 0)
    vis_end = ((q0 + row) // CHUNK + 1) * CHUNK
    lane_kt = lax.broadcasted_iota(jnp.int32, (1, kt), 1)

    @pl.when(i == 0)
    def _():
        def body(j, m):
            k = kik_ref[pl.ds(pl.multiple_of(j * kt, kt), kt), 0:64].astype(F32)
            return jnp.maximum(m, jnp.sum(k * k, axis=1, keepdims=True))
        norms = lax.fori_loop(0, kik_ref.shape[0] // kt, body, jnp.zeros((kt, 1), F32))
        kmax_ref[...] = jnp.max(norms, axis=0, keepdims=True)

    for h in range(IDX_HEADS):
        iqs_ref[h * qb:(h + 1) * qb, :] = iq_ref[:, h * 64:(h + 1) * 64].astype(BF16)
        iwb_ref[h] = jnp.broadcast_to(aviw_ref[:, 64 + h:65 + h], (qb, LANES))

    def score_tile(j, carry):
        off = pl.multiple_of(j * kt, kt)
        ik = kik_ref[pl.ds(off, kt), 64:128]
        s = lax.dot_general(iqs_ref[...], ik, NT_DIMS, preferred_element_type=F32)
        parts = []
        for c in range(kt // LANES):
            a = jnp.zeros((qb, LANES), F32)
            for h in range(IDX_HEADS):
                a = a + iwb_ref[h] * jnp.maximum(s[h * qb:(h + 1) * qb, c * LANES:(c + 1) * LANES], 0.0)
            parts.append(a)
        acc = jnp.concatenate(parts, axis=1)
        acc = acc + 0.0
        acc = jnp.where(off + lane_kt < vis_end, acc, -jnp.inf)
        bits = pltpu.bitcast(acc, jnp.int32)
        key = jnp.where(bits < 0, bits ^ 0x7FFFFFFF, bits)
        skey_ref[:, pl.ds(off, kt)] = key
        half_ref[:, pl.ds(off, kt)] = jnp.right_shift(key, 16).astype(jnp.int16)
        return carry

    lax.fori_loop(0, n_tiles, score_tile, 0)

    def count(pred_fn):
        def body(j, cnt):
            off = pl.multiple_of(j * kt, kt)
            hit = pred_fn(skey_ref[:, pl.ds(off, kt)])
            for c in range(kt // LANES):
                cnt = cnt + jnp.where(hit[:, c * LANES:(c + 1) * LANES], 1.0, 0.0)
            return cnt
        cnt = lax.fori_loop(0, n_tiles, body, jnp.zeros((qb, LANES), F32))
        return jnp.sum(cnt, axis=1, keepdims=True)

    def count_half_ge(cand):
        cand16 = jnp.broadcast_to(cand, (qb, LANES)).astype(jnp.int16)
        one = jnp.ones((qb, LANES), jnp.int16)
        nil = jnp.zeros((qb, LANES), jnp.int16)

        def body(j, cnt):
            off = pl.multiple_of(j * kt, kt)
            tile = half_ref[:, pl.ds(off, kt)]
            for c in range(kt // LANES):
                cnt = cnt + jnp.where(tile[:, c * LANES:(c + 1) * LANES] >= cand16, one, nil)
            return cnt
        cnt = lax.fori_loop(0, n_tiles, body, nil)
        return jnp.sum(cnt.astype(F32), axis=1, keepdims=True)

    def bisect16(cnt_min, extra):
        zero = jnp.zeros((qb, 1), jnp.int32)
        cnt = extra + count_half_ge(zero)
        ok = cnt >= topk
        start = (jnp.where(ok, zero, jnp.full((qb, 1), -(2 ** 15), jnp.int32)), jnp.where(ok, cnt, cnt_min))

        def bit_step(it, c):
            t, cnt_t = c
            cand = t | jnp.left_shift(jnp.int32(1), 14 - it)
            cnt = extra + count_half_ge(cand)
            ok = cnt >= topk
            return jnp.where(ok, cand, t), jnp.where(ok, cnt, cnt_t)
        return lax.fori_loop(0, 15, bit_step, start)

    visited = jnp.full((qb, 1), n_tiles * kt, jnp.int32).astype(F32)
    t_hi, cnt_hi = bisect16(visited, 0.0)
    top16 = 2 ** 15 - 1
    above = jnp.where(t_hi == top16, 0.0, count_half_ge(jnp.minimum(t_hi + 1, top16)))

    def low_tile(j, carry):
        off = pl.multiple_of(j * kt, kt)
        key = skey_ref[:, pl.ds(off, kt)]
        low = (key & 0xFFFF) - 2 ** 15
        half_ref[:, pl.ds(off, kt)] = jnp.where(jnp.right_shift(key, 16) == t_hi, low, -(2 ** 15)).astype(jnp.int16)
        return carry

    lax.fori_loop(0, n_tiles, low_tile, 0)
    t_lo, cnt_ge = bisect16(cnt_hi, above)
    thr = jnp.left_shift(t_hi, 16) | (t_lo + 2 ** 15)
    lane_ok = lambda off: off + lane_kt < vis_end

    def select_all_ties(j, carry):
        off = pl.multiple_of(j * kt, kt)
        sel = (skey_ref[:, pl.ds(off, kt)] >= thr) & lane_ok(off)
        skey_ref[:, pl.ds(off, kt)] = pltpu.bitcast(jnp.where(sel, 0.0, -jnp.inf), jnp.int32)
        return carry

    def select_ranked_ties(need):
        def body(j, eq_seen):
            off = pl.multiple_of(j * kt, kt)
            key = skey_ref[:, pl.ds(off, kt)]
            eq = key == thr
            rank = _dot(jnp.where(eq, 1.0, 0.0).astype(BF16), tri_ref[...]) + eq_seen
            sel = ((key > thr) | (eq & (rank <= need))) & lane_ok(off)
            skey_ref[:, pl.ds(off, kt)] = pltpu.bitcast(jnp.where(sel, 0.0, -jnp.inf), jnp.int32)
            return rank[:, kt - 1:kt]
        return body

    def exact_fit():
        lax.fori_loop(0, n_tiles, select_all_ties, 0)

    def surplus_ties():
        need = topk - count(lambda x: x > thr)
        lax.fori_loop(0, n_tiles, select_ranked_ties(need), jnp.zeros((qb, 1), F32))

    lax.cond(jnp.max(jnp.abs(cnt_ge - topk)) == 0.0, exact_fit, surplus_ties)

    for h in range(A_HEADS):
        qs_ref[h * qb:(h + 1) * qb, :] = (aq_ref[:, h * 64:(h + 1) * 64] * (HEAD_DIM ** -0.5)).astype(BF16)

    def max_tile(j, carry):
        off = pl.multiple_of(j * kt, kt)
        bias = pltpu.bitcast(skey_ref[:, pl.ds(off, kt)], F32)
        k = kik_ref[pl.ds(off, kt), 0:64]
        s = lax.dot_general(qs_ref[...], k, NT_DIMS, preferred_element_type=F32)
        for h in range(A_HEADS):
            sh = s[h * qb:(h + 1) * qb, :] + bias
            mm = sh[:, 0:LANES]
            for c in range(1, kt // LANES):
                mm = jnp.maximum(mm, sh[:, c * LANES:(c + 1) * LANES])
            mlane_ref[h] = jnp.maximum(mlane_ref[h], mm)
        return carry

    def pv_tile(j, carry):
        off = pl.multiple_of(j * kt, kt)
        bias = pltpu.bitcast(skey_ref[:, pl.ds(off, kt)], F32)
        k = kik_ref[pl.ds(off, kt), 0:64]
        s = lax.dot_general(qs_ref[...], k, NT_DIMS, preferred_element_type=F32)
        for h in range(A_HEADS):
            rows = slice(h * qb, (h + 1) * qb)
            p_ref[rows, :] = jnp.exp((s[rows, :] + bias) - m_ref[rows, :]).astype(BF16)
        acc_ref[...] += _dot(p_ref[...], v_ref[pl.ds(off, kt), :])
        return carry

    q32 = qs_ref[...].astype(F32)
    m_ref[...] = jnp.sqrt(jnp.sum(q32 * q32, axis=1, keepdims=True) * kmax_ref[...]) * SHIFT_MARGIN
    acc_ref[...] = jnp.zeros(acc_ref.shape, F32)
    lax.fori_loop(0, n_tiles, pv_tile, 0)

    def exact_shift():
        mlane_ref[...] = jnp.full(mlane_ref.shape, -jnp.inf, F32)
        lax.fori_loop(0, n_tiles, max_tile, 0)
        for h in range(A_HEADS):
            m_ref[h * qb:(h + 1) * qb, :] = jnp.max(mlane_ref[h], axis=1, keepdims=True)
        acc_ref[...] = jnp.zeros(acc_ref.shape, F32)
        lax.fori_loop(0, n_tiles, pv_tile, 0)

    lax.cond(jnp.min(acc_ref[:, 64:65]) >= MIN_SOFTMAX_SUM, lambda: None, exact_shift)
    for h in range(A_HEADS):
        a = acc_ref[h * qb:(h + 1) * qb, :]
        o_ref[:, h * 64:(h + 1) * 64] = (a / a[:, 64:65])[:, 0:64]


def _dsa(aq, iq, aviw, kik_all, v_all, bsz, t_len, past, kt):
    qb = min(DSA_QUERY_ROWS, t_len)
    nq = t_len // qb
    lp = kik_all.shape[1]
    topk = min(TOPK_MAX, (past + t_len) // 4)
    tri = jnp.triu(jnp.ones((kt, kt), F32)).astype(BF16)
    qrow = lambda w: pl.BlockSpec((qb, w), lambda b, i: (b * nq + i, 0))
    keys = pl.BlockSpec((None, lp, LANES), lambda b, i: (b, 0, 0))
    return pl.pallas_call(
        functools.partial(_dsa_kernel, past=past, qb=qb, kt=kt, topk=topk),
        grid=(bsz, nq),
        in_specs=[qrow(512), qrow(512), qrow(LANES), keys, keys, pl.BlockSpec((kt, kt), lambda b, i: (0, 0))],
        out_specs=qrow(512),
        out_shape=jax.ShapeDtypeStruct((bsz * t_len, 512), F32),
        scratch_shapes=[
            pltpu.VMEM((qb, lp), jnp.int32),
            pltpu.VMEM((qb, lp), jnp.int16),
            pltpu.VMEM((IDX_HEADS * qb, 64), BF16),
            pltpu.VMEM((IDX_HEADS, qb, LANES), F32),
            pltpu.VMEM((A_HEADS * qb, 64), BF16),
            pltpu.VMEM((A_HEADS * qb, kt), BF16),
            pltpu.VMEM((A_HEADS * qb, 1), F32),
            pltpu.VMEM((A_HEADS, qb, LANES), F32),
            pltpu.VMEM((A_HEADS * qb, LANES), F32),
            pltpu.VMEM((1, 1), F32),
        ],
        compiler_params=_params("parallel", "arbitrary"),
        name="dsa",
    )(aq, iq, aviw, kik_all, v_all, tri)


def _retention_kernel(bqk_ref, bv_ref, bg_ref, s0_ref, gn_ref, o_ref, s_out_ref, s_ref, *, n_chunks):
    t = pl.program_id(0)

    @pl.when(t == 0)
    def _():
        s_ref[...] = s0_ref[...]

    n = CHUNK
    ri = lax.broadcasted_iota(jnp.int32, (n, n), 0).astype(F32)
    ci = lax.broadcasted_iota(jnp.int32, (n, n), 1).astype(F32)
    diff = ri - ci
    pos = lax.broadcasted_iota(jnp.int32, (n, 1), 0).astype(F32)
    decays = []
    for h in range(B_HEADS):
        log_g = math.log(1.0 - 2.0 ** (-5.0 - h))
        decays.append((jnp.where(diff >= 0, jnp.exp(jnp.maximum(diff, 0.0) * log_g), 0.0),
                       jnp.exp((pos + 1.0) * log_g), jnp.exp((n - 1.0 - pos) * log_g), math.exp(n * log_g)))
    for c in range(n_chunks):
        rows = slice(c * n, (c + 1) * n)
        for b in range(bqk_ref.shape[0]):
            for h in range(B_HEADS):
                intra, q_decay, k_decay, chunk_decay = decays[h]
                q = bqk_ref[b, rows, h * B_DK:(h + 1) * B_DK]
                k = bqk_ref[b, rows, 256 + h * B_DK:256 + (h + 1) * B_DK]
                v = bv_ref[b, rows, h * B_DV:(h + 1) * B_DV]
                s = s_ref[b, h]
                scores = _dot_f32x3(q, k, NT_DIMS) * intra
                o = _dot_f32x3(scores, v) + _dot_f32x3(q, s) * q_decay
                s_ref[b, h] = s * chunk_decay + _dot_f32x3(k * k_decay, v, TN_DIMS)
                mu = jnp.mean(o, axis=-1, keepdims=True)
                oc = o - mu
                var = jnp.mean(oc * oc, axis=-1, keepdims=True)
                g = bg_ref[b, rows, h * B_DV:(h + 1) * B_DV]
                o_ref[b, rows, h * B_DV:(h + 1) * B_DV] = (
                    oc * lax.rsqrt(var + LN_EPS) * gn_ref[:, h * B_DV:(h + 1) * B_DV] * (g * _sigmoid(g)))

    @pl.when(t == pl.num_programs(0) - 1)
    def _():
        s_out_ref[...] = s_ref[...]


def _retention(bqk, bv, bg, s0, b_gn, bsz, t_len, tt):
    row = pl.BlockSpec((bsz, tt, 512), lambda t: (0, t, 0))
    state = pl.BlockSpec((bsz, B_HEADS, B_DK, B_DV), lambda t: (0, 0, 0, 0))
    o_b, s_new = pl.pallas_call(
        functools.partial(_retention_kernel, n_chunks=tt // CHUNK),
        grid=(t_len // tt,),
        in_specs=[row, row, row, state, pl.BlockSpec((1, 512), lambda t: (0, 0))],
        out_specs=[row, state],
        out_shape=[jax.ShapeDtypeStruct((bsz, t_len, 512), F32),
                   jax.ShapeDtypeStruct((bsz, B_HEADS, B_DK, B_DV), F32)],
        scratch_shapes=[pltpu.VMEM((bsz, B_HEADS, B_DK, B_DV), F32)],
        compiler_params=_params("arbitrary"),
        name="retention",
    )(bqk.reshape(bsz, t_len, 512), bv.reshape(bsz, t_len, 512), bg.reshape(bsz, t_len, 512), s0, b_gn)
    return o_b.reshape(bsz * t_len, 512), s_new


def _out_ln_kernel(oa_ref, ob_ref, x_ref, w_ref, g_ref, b_ref, y_ref):
    y = _dot(oa_ref[...].astype(BF16), w_ref[0:512, :]) + _dot(ob_ref[...].astype(BF16), w_ref[512:1024, :])
    y_ref[...] = _layernorm_rows(ALPHA * x_ref[...] + y, g_ref[...], b_ref[...])


def _out_ln(oa, ob, x2, w_out, g, b, tm):
    n = x2.shape[0]
    row = lambda w: pl.BlockSpec((tm, w), lambda i: (i, 0))
    vec = pl.BlockSpec((1, D_MODEL), lambda i: (0, 0))
    return pl.pallas_call(
        _out_ln_kernel,
        grid=(n // tm,),
        in_specs=[row(512), row(512), row(D_MODEL), pl.BlockSpec((D_MODEL, D_MODEL), lambda i: (0, 0)), vec, vec],
        out_specs=row(D_MODEL),
        out_shape=jax.ShapeDtypeStruct((n, D_MODEL), F32),
        compiler_params=_params("parallel"),
        name="out_ln",
    )(oa, ob, x2, w_out, g, b)


def _out_ln_d_kernel(oc_ref, od_ref, r_ref, kr_ref, v_ref, gate_ref, x_ref, w_ref, hm_ref,
                     lnx_g_ref, lnx_b_ref, rk_ref, g_ref, b_ref, y_ref):
    hm = hm_ref[...]
    o = od_ref[...]
    mu = _dot_exact_rhs(o, hm) * (1.0 / HEAD_DIM)
    oc = o - mu
    var = _dot_exact_rhs(oc * oc, hm) * (1.0 / HEAD_DIM)
    normed = oc * lax.rsqrt(var + D_GN_EPS) * lnx_g_ref[...] + lnx_b_ref[...]
    v = v_ref[...]
    bonus = _dot_exact_rhs(r_ref[...] * kr_ref[...] * rk_ref[...], hm) * v
    od = (normed + bonus) * gate_ref[...]
    y = _dot(oc_ref[...].astype(BF16), w_ref[0:512, :]) + _dot(od.astype(BF16), w_ref[512:1024, :])
    y_ref[...] = _layernorm_rows(ALPHA * x_ref[...] + y, g_ref[...], b_ref[...])


def _head_mask():
    head = jnp.arange(D_W) // HEAD_DIM
    return (head[:, None] == head[None, :]).astype(BF16)


def _out_ln_d(oc, od, r, kr, v, gate, x2, w_out, lnx_g, lnx_b, r_k, g, b, tm):
    n = x2.shape[0]
    row = lambda w: pl.BlockSpec((tm, w), lambda i: (i, 0))
    vec = lambda w: pl.BlockSpec((1, w), lambda i: (0, 0))
    return pl.pallas_call(
        _out_ln_d_kernel,
        grid=(n // tm,),
        in_specs=[row(512)] * 6 + [row(D_MODEL), pl.BlockSpec((D_MODEL, D_MODEL), lambda i: (0, 0)),
                                   pl.BlockSpec((D_W, D_W), lambda i: (0, 0)),
                                   vec(D_W), vec(D_W), vec(D_W), vec(D_MODEL), vec(D_MODEL)],
        out_specs=row(D_MODEL),
        out_shape=jax.ShapeDtypeStruct((n, D_MODEL), F32),
        compiler_params=_params("parallel"),
        name="out_ln_d",
    )(oc, od, r, kr, v, gate, x2, w_out, _head_mask(), lnx_g, lnx_b, r_k, g, b)


MOE_EXPERTS_PER_STEP = 4


def _moe_kernel(x_ref, rw_ref, rb_ref, w13_ref, w2_ref, g_ref, b_ref, y_ref, acc_ref, gate_ref, xb_ref):
    e = pl.program_id(1)
    tm = x_ref.shape[0]
    lane = lax.broadcasted_iota(jnp.int32, (tm, N_EXPERTS), 1)

    lane_f = lane.astype(F32)

    def first_argmax(v):
        m = jnp.max(v, axis=1, keepdims=True)
        idx = jnp.min(jnp.where(v == m, lane_f, float(N_EXPERTS)), axis=1, keepdims=True)
        return m, idx.astype(jnp.int32)

    @pl.when(e == 0)
    def _():
        x = x_ref[...]
        xb_ref[...] = x.astype(BF16)
        aff = _sigmoid(_dot_hi(x, rw_ref[...]))
        biased = aff + rb_ref[...]
        best = jnp.zeros((tm, 1), jnp.int32)
        best_score = jnp.full((tm, 1), -jnp.inf, F32)
        for grp in range(N_GROUPS):
            vg = jnp.where(lane // EXPERTS_PER_GROUP == grp, biased, -jnp.inf)
            top1, idx1 = first_argmax(vg)
            top2, _ = first_argmax(jnp.where(lane == idx1, -jnp.inf, vg))
            score = top1 + top2
            better = score > best_score
            best = jnp.where(better, grp, best)
            best_score = jnp.where(better, score, best_score)
        masked = jnp.where(lane // EXPERTS_PER_GROUP == best, biased, -jnp.inf)
        _, idx1 = first_argmax(masked)
        _, idx2 = first_argmax(jnp.where(lane == idx1, -jnp.inf, masked))
        top_aff = jnp.where((lane == idx1) | (lane == idx2), aff, 0.0)
        gate_ref[...] = top_aff / jnp.sum(top_aff, axis=1, keepdims=True)
        acc_ref[...] = jnp.zeros(acc_ref.shape, F32)

    for k in range(MOE_EXPERTS_PER_STEP):
        h13 = _dot(xb_ref[...], w13_ref[k])
        h1 = h13[:, 0:D_EXPERT]
        h = (h1 * _sigmoid(h1)) * h13[:, D_EXPERT:2 * D_EXPERT]
        gate_e = jnp.sum(jnp.where(lane == e * MOE_EXPERTS_PER_STEP + k, gate_ref[...], 0.0), axis=1, keepdims=True)
        acc_ref[...] += gate_e * _dot(h.astype(BF16), w2_ref[k])

    @pl.when(e == pl.num_programs(1) - 1)
    def _():
        y_ref[...] = _layernorm_rows(ALPHA * x_ref[...] + acc_ref[...], g_ref[...], b_ref[...])


def _moe_ln(x2, router_w, router_b, w13, w2, g, b, tm):
    n = x2.shape[0]
    row = pl.BlockSpec((tm, D_MODEL), lambda i, e: (i, 0))
    vec = pl.BlockSpec((1, D_MODEL), lambda i, e: (0, 0))
    return pl.pallas_call(
        _moe_kernel,
        grid=(n // tm, N_EXPERTS // MOE_EXPERTS_PER_STEP),
        in_specs=[row,
                  pl.BlockSpec((D_MODEL, N_EXPERTS), lambda i, e: (0, 0)),
                  pl.BlockSpec((1, N_EXPERTS), lambda i, e: (0, 0)),
                  pl.BlockSpec((MOE_EXPERTS_PER_STEP, D_MODEL, 2 * D_EXPERT), lambda i, e: (e, 0, 0)),
                  pl.BlockSpec((MOE_EXPERTS_PER_STEP, D_EXPERT, D_MODEL), lambda i, e: (e, 0, 0)),
                  vec, vec],
        out_specs=row,
        out_shape=jax.ShapeDtypeStruct((n, D_MODEL), F32),
        scratch_shapes=[pltpu.VMEM((tm, D_MODEL), F32), pltpu.VMEM((tm, N_EXPERTS), F32),
                        pltpu.VMEM((tm, D_MODEL), BF16)],
        compiler_params=_params("parallel", "arbitrary"),
        name="moe_ln",
    )(x2, router_w, router_b, w13, w2, g, b)


def _proj_cd_kernel(x_ref, w_ref, cq_ref, ck_ref, cv_ref, pd_ref, ck16_ref, cv16_ref):
    xb = x_ref[...].astype(BF16)
    cq_ref[...] = _dot(xb, w_ref[:, 0:512])
    ck = _dot(xb, w_ref[:, 512:1024])
    ck_ref[...] = ck
    ck16_ref[...] = ck.astype(BF16)
    cv = _dot(xb, w_ref[:, 1024:1536])
    cv_ref[...] = cv
    cv16_ref[...] = cv.astype(BF16)
    pd_ref[...] = _dot(xb, w_ref[:, 1536:1536 + D_SHIFT_W])


def _proj_cd(x2, w_bf16, tm):
    n = x2.shape[0]
    row = lambda w: pl.BlockSpec((tm, w), lambda i: (i, 0))
    outs = [512, 512, 512, D_SHIFT_W]
    return pl.pallas_call(
        _proj_cd_kernel,
        grid=(n // tm,),
        in_specs=[row(D_MODEL), pl.BlockSpec((D_MODEL, 1536 + D_SHIFT_W), lambda i: (0, 0))],
        out_specs=[row(w) for w in outs] + [row(512), row(512)],
        out_shape=[jax.ShapeDtypeStruct((n, w), F32) for w in outs] + [jax.ShapeDtypeStruct((n, 512), BF16)] * 2,
        compiler_params=_params("parallel"),
        name="proj_cd",
    )(x2, w_bf16)


def _stick_kernel(q_ref, k_ref, v_ref, ust_ref, o_ref, *, past, tq, kb):
    i = pl.program_id(2)
    q0 = past + i * tq
    n_kb = (q0 + tq + kb - 1) // kb
    qpos = q0 + lax.broadcasted_iota(jnp.int32, (tq, 1), 0)
    lane_kb = lax.broadcasted_iota(jnp.int32, (1, kb), 1)
    n_heads = LANES // HEAD_DIM
    cols = [slice(hh * HEAD_DIM, (hh + 1) * HEAD_DIM) for hh in range(n_heads)]
    qs = [(q_ref[:, c] * (HEAD_DIM ** -0.5)).astype(BF16) for c in cols]

    def cond(c):
        j, carries, _ = c
        worst = carries[0]
        for carry in carries[1:]:
            worst = jnp.maximum(worst, carry)
        return jnp.logical_and(j >= 0, jnp.max(worst) > EXP_ZERO_BELOW)

    def body(c):
        j, carries, outs = c
        off = pl.multiple_of(j * kb, kb)
        strict = off + lane_kb < qpos
        new_carries, new_outs = [], []
        for hh in range(n_heads):
            k = k_ref[pl.ds(off, kb), cols[hh]]
            v = v_ref[pl.ds(off, kb), cols[hh]]
            z = lax.dot_general(qs[hh], k, NT_DIMS, preferred_element_type=F32)
            sp = _softplus(z)
            log_keep = jnp.where(strict, -sp, 0.0)
            later = carries[hh] + _dot_exact_rhs2(log_keep, ust_ref[...])
            a = jnp.where(strict, jnp.exp((z - sp) + later), 0.0)
            new_outs.append(outs[hh] + _dot(a.astype(BF16), v))
            new_carries.append(carries[hh] + jnp.sum(log_keep, axis=1, keepdims=True))
        return j - 1, tuple(new_carries), tuple(new_outs)

    init = (n_kb - 1, tuple(jnp.zeros((tq, 1), F32) for _ in cols), tuple(jnp.zeros((tq, HEAD_DIM), F32) for _ in cols))
    _, _, outs = lax.while_loop(cond, body, init)
    for hh in range(n_heads):
        o_ref[:, cols[hh]] = outs[hh]


def _stick(cq, k_all, v_all, bsz, t_len, past, kb):
    tq = min(kb, t_len)
    nq = t_len // tq
    lp = k_all.shape[1]
    ust = jnp.tril(jnp.ones((kb, kb), F32), -1).astype(BF16)
    qrow = pl.BlockSpec((tq, LANES), lambda b, hp, i: (b * nq + i, hp))
    keys = pl.BlockSpec((None, lp, LANES), lambda b, hp, i: (b, 0, hp))
    return pl.pallas_call(
        functools.partial(_stick_kernel, past=past, tq=tq, kb=kb),
        grid=(bsz, C_HEADS * HEAD_DIM // LANES, nq),
        in_specs=[qrow, keys, keys, pl.BlockSpec((kb, kb), lambda b, hp, i: (0, 0))],
        out_specs=qrow,
        out_shape=jax.ShapeDtypeStruct((bsz * t_len, 512), F32),
        compiler_params=_params("parallel", "parallel", "arbitrary"),
        name="stick",
    )(cq, k_all, v_all, ust)


def _rwkv_pre_kernel(pd_ref, shift_ref, mu_ref, w0a0_ref, wa2_ref, gw2_ref, kk_ref_, ka_ref, hm_ref,
                     r_out, w_out, kr_out, v_out, kk_out, b_out, g_out, last_ref):
    t = pl.program_id(1)

    @pl.when(t == 0)
    def _():
        last_ref[...] = shift_ref[...]

    pd = pd_ref[...]
    tm = pd.shape[0]
    rolled = pltpu.roll(pd, 1, 0)
    first_row = lax.broadcasted_iota(jnp.int32, (tm, 1), 0) == 0
    prev = jnp.where(first_row, last_ref[...], rolled)
    last_ref[...] = pd[tm - 1:tm, :]
    pm = pd + (prev - pd) * mu_ref[...]
    r = pm[:, 0:512]
    k = pm[:, 512:1024]
    v = pm[:, 1024:1536]
    lwa = pm[:, 1536:1664]
    lg = pm[:, 1664:1792]
    lane = lax.broadcasted_iota(jnp.int32, lwa.shape, 1)
    lwa = jnp.where(lane < D_LORA_W, jnp.tanh(lwa), lwa)
    pre = w0a0_ref[...] + _dot_hi(lwa, wa2_ref[...])
    w_log = -_softplus(-pre[:, 0:512]) - 0.5
    decay = jnp.exp(-jnp.exp(w_log))
    a = _sigmoid(pre[:, 512:1024])
    g = _dot_hi(_sigmoid(lg), gw2_ref[...])
    kk = k * kk_ref_[...]
    kk = kk * lax.rsqrt(_dot_exact_rhs(kk * kk, hm_ref[...]) + 1e-12)
    r_out[...] = r
    w_out[...] = decay
    kr_out[...] = k * (1.0 + (a - 1.0) * ka_ref[...])
    v_out[...] = v
    kk_out[...] = kk
    b_out[...] = kk * a
    g_out[...] = g


def _rwkv_pre(pd, shift, mu, w0a0, wa2, gw2, k_k, k_a, bsz, t_len, tm):
    nt = t_len // tm
    row = lambda w: pl.BlockSpec((tm, w), lambda b, t: (b * nt + t, 0))
    const = lambda s: pl.BlockSpec(s, lambda b, t: (0,) * len(s))
    return pl.pallas_call(
        _rwkv_pre_kernel,
        grid=(bsz, nt),
        in_specs=[row(D_SHIFT_W), pl.BlockSpec((None, 1, D_SHIFT_W), lambda b, t: (b, 0, 0)),
                  const((1, D_SHIFT_W)), const((1, 2 * D_W)), const((LANES, 2 * D_W)), const((D_LORA_G, D_W)),
                  const((1, D_W)), const((1, D_W)), const((D_W, D_W))],
        out_specs=[row(D_W)] * 7,
        out_shape=[jax.ShapeDtypeStruct((bsz * t_len, D_W), F32)] * 7,
        scratch_shapes=[pltpu.VMEM((1, D_SHIFT_W), F32)],
        compiler_params=_params("parallel", "arbitrary"),
        name="rwkv_pre",
    )(pd, shift, mu, w0a0, wa2, gw2, k_k, k_a, _head_mask())


RWKV_KL = HEAD_DIM // 2


def _rwkv_scan_kernel(kk_ref, w_ref, b_ref, kr_ref, r_ref, v_ref, s0_ref, o_ref, s_out_ref, s_ref, *, tb):
    g = pl.program_id(0)

    @pl.when(g == 0)
    def _():
        s_ref[...] = s0_ref[...]

    def both_halves(x):
        return x + pltpu.roll(x, LANES // 2, 1)

    def step(t, carry):
        for vrows in (slice(0, HEAD_DIM // 2), slice(HEAD_DIM // 2, HEAD_DIM)):
            acc = [jnp.zeros((HEAD_DIM // 2, LANES), F32) for _ in range(2)]
            for kl in range(RWKV_KL):
                acc[kl % 2] = acc[kl % 2] + s_ref[kl, vrows, :] * kk_ref[t, kl:kl + 1, :]
            sa = both_halves(acc[0] + acc[1])
            v = v_ref[t, vrows, :]
            out = [jnp.zeros((HEAD_DIM // 2, LANES), F32) for _ in range(2)]
            for kl in range(RWKV_KL):
                row = slice(kl, kl + 1)
                s = s_ref[kl, vrows, :] * w_ref[t, row, :] - sa * b_ref[t, row, :] + v * kr_ref[t, row, :]
                s_ref[kl, vrows, :] = s
                out[kl % 2] = out[kl % 2] + s * r_ref[t, row, :]
            o_ref[t, vrows, :] = both_halves(out[0] + out[1])
        return carry

    lax.fori_loop(0, tb, step, 0)

    @pl.when(g == pl.num_programs(0) - 1)
    def _():
        s_out_ref[...] = s_ref[...]


def _rwkv_scan(kk, w, b, kr, r, v, s0, t_len, tb):
    keyed = pl.BlockSpec((tb, RWKV_KL, LANES), lambda g: (g, 0, 0))
    valued = pl.BlockSpec((tb, HEAD_DIM, LANES), lambda g: (g, 0, 0))
    state = pl.BlockSpec((RWKV_KL, HEAD_DIM, LANES), lambda g: (0, 0, 0))
    return pl.pallas_call(
        functools.partial(_rwkv_scan_kernel, tb=tb),
        grid=(t_len // tb,),
        in_specs=[keyed] * 5 + [valued, state],
        out_specs=[valued, state],
        out_shape=[jax.ShapeDtypeStruct((t_len, HEAD_DIM, LANES), F32),
                   jax.ShapeDtypeStruct((RWKV_KL, HEAD_DIM, LANES), F32)],
        scratch_shapes=[pltpu.VMEM((RWKV_KL, HEAD_DIM, LANES), F32)],
        compiler_params=_params("arbitrary"),
        name="rwkv_scan",
    )(kk, w, b, kr, r, v, s0)


def _to_scan_keyed(x2, bsz, t_len):
    y = x2.reshape(bsz, t_len, D_HEADS, 2, RWKV_KL).transpose(1, 4, 3, 0, 2)
    return y.reshape(t_len, RWKV_KL, 2 * bsz * D_HEADS)


def _to_scan_valued(x2, bsz, t_len):
    y = x2.reshape(bsz, t_len, D_HEADS, HEAD_DIM).transpose(1, 3, 0, 2).reshape(t_len, HEAD_DIM, bsz * D_HEADS)
    return jnp.concatenate([y, y], axis=-1)


def _from_scan_valued(y, bsz, t_len):
    y = y[:, :, 0:bsz * D_HEADS].reshape(t_len, HEAD_DIM, bsz, D_HEADS)
    return y.transpose(2, 0, 3, 1).reshape(bsz * t_len, D_W)


def _state_to_scan(s, bsz):
    y = s.reshape(bsz, D_HEADS, HEAD_DIM, 2, RWKV_KL).transpose(4, 2, 3, 0, 1)
    return y.reshape(RWKV_KL, HEAD_DIM, 2 * bsz * D_HEADS)


def _state_from_scan(y, bsz):
    return y.reshape(RWKV_KL, HEAD_DIM, 2, bsz, D_HEADS).transpose(3, 4, 1, 2, 0).reshape(
        bsz, D_HEADS, HEAD_DIM, HEAD_DIM)


def _pad_keys(x, mult):
    pad = (-x.shape[1]) % mult
    return x if pad == 0 else jnp.pad(x, ((0, 0), (0, pad), (0, 0)))


def _tile(n, cap):
    return min(n, cap)


def _run_group(x, past, p):
    bsz, t_len, _ = x.shape
    assert bsz * D_HEADS * 2 == LANES, "rwkv scan packs (key half, batch, head) into the lane axis"
    n = bsz * t_len
    x2 = x.reshape(n, D_MODEL)
    past_len = 0 if past is None else past[0].shape[2]
    tm = _tile(t_len, 512)

    cos, sin = _rope_tables(past_len, t_len)
    aq, iq, bqk, kik, aviw, bv, bg, kik16, v16 = _proj_ab(x2, p['w_in_ab'], cos, sin, t_len, tm)
    ak = kik[:, 0:64].reshape(1, bsz, t_len, 64)
    ik = kik[:, 64:128].reshape(1, bsz, t_len, 64)
    av = aviw[:, 0:64].reshape(1, bsz, t_len, 64)
    kik_all = kik16.reshape(bsz, t_len, LANES)
    v_all = v16.reshape(bsz, t_len, LANES)
    ones_col = (jnp.arange(LANES) == 64).astype(F32)
    if past is None:
        s_b = jnp.zeros((bsz, B_HEADS, B_DK, B_DV), F32)
    else:
        pk, pv, pik, sb = past[0][0], past[1][0], past[2][0], past[3][0]
        kik_all = jnp.concatenate([jnp.concatenate([pk, pik], axis=-1).astype(BF16), kik_all], axis=1)
        pv_slab = jnp.concatenate([pv, jnp.broadcast_to(ones_col[64:], pv.shape)], axis=-1)
        v_all = jnp.concatenate([pv_slab.astype(BF16), v_all], axis=1)
        s_b = sb
    kt = 1024
    o_a = _dsa(aq, iq, aviw, _pad_keys(kik_all, kt), _pad_keys(v_all, kt), bsz, t_len, past_len, kt)
    o_b, s_b_new = _retention(bqk, bv, bg, s_b, p['b_gn'], bsz, t_len, _tile(t_len, 2 * CHUNK))
    x2 = _out_ln(o_a, o_b, x2, p['w_out_ab'], p['ln_g'][0, 0][None], p['ln_b'][0, 0][None], tm)
    tm_moe = _tile(n, 1024)
    x2 = _moe_ln(x2, p['router_w'], p['router_b'], p['w13'][0], p['w2'][0],
                 p['ln_g'][0, 1][None], p['ln_b'][0, 1][None], tm_moe)

    cq, ck, cv, pd, ck16, cv16 = _proj_cd(x2, p['w_in_cd'], tm)
    ck_all = ck16.reshape(bsz, t_len, 512)
    cv_all = cv16.reshape(bsz, t_len, 512)
    if past is None:
        s_d = jnp.zeros((bsz, D_HEADS, HEAD_DIM, HEAD_DIM), F32)
        shift = jnp.zeros((bsz, 1, D_SHIFT_W), F32)
    else:
        ck_all = jnp.concatenate([past[4][0].reshape(bsz, past_len, 512).astype(BF16), ck_all], axis=1)
        cv_all = jnp.concatenate([past[5][0].reshape(bsz, past_len, 512).astype(BF16), cv_all], axis=1)
        s_d, shift = past[6][0], past[7][0]
    kb = 256
    o_c = _stick(cq, _pad_keys(ck_all, kb), _pad_keys(cv_all, kb), bsz, t_len, past_len, kb)
    r, w, kr, v, kk, b, g = _rwkv_pre(pd, shift, p['d_mu'], p['d_w0a0'], p['d_wa2'], p['d_g2'],
                                      p['d_k_k'], p['d_k_a'], bsz, t_len, tm)
    keyed = [_to_scan_keyed(u, bsz, t_len) for u in (kk, w, b, kr, r)]
    o_scan, s_scan = _rwkv_scan(*keyed, _to_scan_valued(v, bsz, t_len), _state_to_scan(s_d, bsz),
                                t_len, _tile(t_len, 64))
    o_d = _from_scan_valued(o_scan, bsz, t_len)
    s_d_new = _state_from_scan(s_scan, bsz)
    x2 = _out_ln_d(o_c, o_d, r, kr, v, g, x2, p['w_out_cd'], p['d_lnx_g'], p['d_lnx_b'], p['d_r_k'],
                   p['ln_g'][1, 0][None], p['ln_b'][1, 0][None], tm)
    x2 = _moe_ln(x2, p['router_w'], p['router_b'], p['w13'][1], p['w2'][1],
                 p['ln_g'][1, 1][None], p['ln_b'][1, 1][None], tm_moe)

    states = (ak, av, ik, s_b_new[None],
              ck.reshape(1, bsz, t_len, C_HEADS, HEAD_DIM), cv.reshape(1, bsz, t_len, C_HEADS, HEAD_DIM),
              s_d_new[None], pd.reshape(bsz, t_len, D_SHIFT_W)[:, -1:][None])
    return x2.reshape(bsz, t_len, D_MODEL), states


def kernel(x_prompt, x_sample, cache_a_k, cache_a_v, cache_a_idx_k, state_b, cache_c_k, cache_c_v, state_d, state_d_shift, w_in_ab, w_out_ab, b_gn, w_in_cd, w_out_cd, d_mu, d_w0, d_w2, d_a0, d_a2, d_g2, d_k_k, d_k_a, d_r_k, d_lnx_g, d_lnx_b, ln_g, ln_b, router_w, router_b, moe_w1, moe_w3, moe_w2):
    zeros_w = jnp.zeros((D_LORA_W, D_W), F32)
    p = {
        'w_in_ab': _pack_w_in_ab(w_in_ab[0]),
        'w_out_ab': w_out_ab[0].astype(BF16),
        'b_gn': b_gn,
        'w_in_cd': w_in_cd[0].astype(BF16),
        'w_out_cd': w_out_cd[0].astype(BF16),
        'd_mu': d_mu,
        'd_w0a0': jnp.concatenate([d_w0, d_a0], axis=1),
        'd_wa2': jnp.concatenate([jnp.concatenate([d_w2[0], zeros_w], axis=1),
                                  jnp.concatenate([zeros_w, d_a2[0]], axis=1)], axis=0),
        'd_g2': d_g2[0],
        'd_k_k': d_k_k, 'd_k_a': d_k_a, 'd_r_k': d_r_k, 'd_lnx_g': d_lnx_g, 'd_lnx_b': d_lnx_b,
        'ln_g': ln_g, 'ln_b': ln_b,
        'router_w': router_w, 'router_b': router_b[None],
        'w13': jnp.concatenate([moe_w1, moe_w3], axis=-1).astype(BF16),
        'w2': moe_w2.astype(BF16),
    }
    y_p, sp = _run_group(x_prompt, None, p)
    past = (cache_a_k, cache_a_v, cache_a_idx_k, state_b, cache_c_k, cache_c_v, state_d, state_d_shift)
    y_s, ss = _run_group(x_sample, past, p)
    return (y_p, y_s, sp[0], sp[1], sp[2], ss[0], ss[1], ss[2], sp[3], ss[3], sp[4], sp[5], ss[4], ss[5],
            sp[6], ss[6], sp[7], ss[7])
```

```python
import functools
import math

import jax
import jax.numpy as jnp
import numpy as np
from jax import lax
from jax.experimental import pallas as pl
from jax.experimental.pallas import tpu as pltpu

F32 = jnp.float32
BF16 = jnp.bfloat16

D_MODEL = 1024
CHUNK = 64
DSA_QUERY_ROWS = 256
DSA_MATMUL_ROWS = 128
ROPE_THETA = 10000.0
HEAD_DIM = 64
LN_EPS = 1e-5
A_HEADS = 8
IDX_HEADS = 8
TOPK_MAX = 256
B_HEADS = 4
B_DK = 64
B_DV = 128
C_HEADS = 8
D_HEADS = 8
D_LORA_W = 64
D_LORA_A = 64
D_LORA_G = 128
D_GN_EPS = 64e-5
N_EXPERTS = 16
N_GROUPS = 4
EXPERTS_PER_GROUP = 4
D_EXPERT = 256
DEPTH = 2
ALPHA = (2 * DEPTH) ** 0.25
D_W = D_HEADS * HEAD_DIM
D_SHIFT_W = 3 * D_W + D_LORA_W + D_LORA_A + D_LORA_G

LANES = 128
SUBLANES = 8
VMEM_LIMIT_BYTES = 56 * 1024 * 1024

SHIFT_MARGIN = 1.001
MIN_SOFTMAX_SUM = 1e-30
EXP_ZERO_BELOW = -104.0

NT_DIMS = (((1,), (1,)), ((), ()))
TN_DIMS = (((0,), (0,)), ((), ()))


def _params(*sem):
    return pltpu.CompilerParams(dimension_semantics=sem, vmem_limit_bytes=VMEM_LIMIT_BYTES)


def _dot(a, b):
    return jnp.dot(a, b, preferred_element_type=F32)


def _dot_hi(a, b):
    return jnp.dot(a, b, preferred_element_type=F32, precision=lax.Precision.HIGHEST)


def _dot_f32x3(a, b, dims=(((1,), (0,)), ((), ()))):
    a_hi = a.astype(BF16)
    b_hi = b.astype(BF16)
    a_lo = (a - a_hi.astype(F32)).astype(BF16)
    b_lo = (b - b_hi.astype(F32)).astype(BF16)
    dot = lambda x, y: lax.dot_general(x, y, dims, preferred_element_type=F32)
    return dot(a_hi, b_hi) + dot(a_hi, b_lo) + dot(a_lo, b_hi)


def _split3(x):
    h1 = x.astype(BF16)
    r1 = x - h1.astype(F32)
    h2 = r1.astype(BF16)
    r2 = r1 - h2.astype(F32)
    return h1, h2, r2.astype(BF16)


def _dot_exact_rhs(x, m01):
    h1, h2, h3 = _split3(x)
    return _dot(h1, m01) + _dot(h2, m01) + _dot(h3, m01)


def _dot_exact_rhs2(x, m01):
    h1 = x.astype(BF16)
    h2 = (x - h1.astype(F32)).astype(BF16)
    return _dot(h1, m01) + _dot(h2, m01)


def _layernorm_rows(x, g, b):
    mu = jnp.mean(x, axis=-1, keepdims=True)
    xc = x - mu
    var = jnp.mean(xc * xc, axis=-1, keepdims=True)
    return xc * lax.rsqrt(var + LN_EPS) * g + b


def _sigmoid(x):
    return 1.0 / (1.0 + jnp.exp(-x))


def _softplus(x):
    return jnp.maximum(x, 0.0) + jnp.log(1.0 + jnp.exp(-jnp.abs(x)))


def _rope_slab(x, cos, sin_signed):
    lane = lax.broadcasted_iota(jnp.int32, x.shape, 1)
    first_half = (lane % HEAD_DIM) < (HEAD_DIM // 2)
    swapped = jnp.where(first_half, pltpu.roll(x, LANES - HEAD_DIM // 2, 1), pltpu.roll(x, HEAD_DIM // 2, 1))
    return x * cos + swapped * sin_signed


AB_ROPED = 1664
AB_PACKED = 2816


def _pack_w_in_ab(w):
    aq, ak, av, iq, ik, iw, bq, bk, bv, bg = jnp.split(
        w, [512, 576, 640, 1152, 1216, 1224, 1480, 1736, 2248], axis=1)
    pad = jnp.zeros((w.shape[0], LANES - 64 - IDX_HEADS), w.dtype)
    return jnp.concatenate([aq, iq, bq, bk, ak, ik, av, iw, pad, bv, bg], axis=1).astype(BF16)


def _proj_ab_kernel(x_ref, w_ref, cos_ref, sin_ref, aq_ref, iq_ref, bqk_ref, kik_ref, aviw_ref, bv_ref, bg_ref,
                    kik16_ref, v16_ref):
    xb = x_ref[...].astype(BF16)
    cos = cos_ref[...]
    sin = sin_ref[...]

    def roped(col0, width, scale_from=None):
        y = _dot(xb, w_ref[:, col0:col0 + width])
        parts = []
        for c in range(width // LANES):
            slab = _rope_slab(y[:, c * LANES:(c + 1) * LANES], cos, sin)
            if scale_from is not None and c * LANES >= scale_from:
                slab = slab * (B_DK ** -0.5)
            parts.append(slab)
        return parts

    for c, slab in enumerate(roped(0, 512)):
        aq_ref[:, c * LANES:(c + 1) * LANES] = slab
    for c, slab in enumerate(roped(512, 512)):
        iq_ref[:, c * LANES:(c + 1) * LANES] = slab
    for c, slab in enumerate(roped(1024, 512, scale_from=256)):
        bqk_ref[:, c * LANES:(c + 1) * LANES] = slab
    kik = roped(1536, LANES)[0]
    kik_ref[...] = kik
    kik16_ref[...] = kik.astype(BF16)
    aviw = _dot(xb, w_ref[:, AB_ROPED:AB_ROPED + LANES])
    lane = lax.broadcasted_iota(jnp.int32, aviw.shape, 1)
    is_iw = (lane >= 64) & (lane < 64 + IDX_HEADS)
    aviw_ref[...] = jnp.where(is_iw, aviw * ((IDX_HEADS * HEAD_DIM) ** -0.5), aviw)
    v16_ref[...] = jnp.where(lane < 64, aviw, jnp.where(lane == 64, 1.0, 0.0)).astype(BF16)
    bv_ref[...] = _dot(xb, w_ref[:, 1792:2304])
    bg_ref[...] = _dot(xb, w_ref[:, 2304:2816])


def _proj_ab(x2, w_packed, cos, sin, t_len, tm):
    n = x2.shape[0]
    nt = t_len // tm
    row = lambda w: pl.BlockSpec((tm, w), lambda i: (i, 0))
    tab = pl.BlockSpec((tm, LANES), lambda i: (i % nt, 0))
    outs = [512, 512, 512, LANES, LANES, 512, 512]
    return pl.pallas_call(
        _proj_ab_kernel,
        grid=(n // tm,),
        in_specs=[row(D_MODEL), pl.BlockSpec((D_MODEL, AB_PACKED), lambda i: (0, 0)), tab, tab],
        out_specs=[row(w) for w in outs] + [row(LANES), row(LANES)],
        out_shape=[jax.ShapeDtypeStruct((n, w), F32) for w in outs] + [jax.ShapeDtypeStruct((n, LANES), BF16)] * 2,
        compiler_params=_params("parallel"),
        name="proj_ab",
    )(x2, w_packed, cos, sin)


def _rope_tables(past, t_len):
    half = HEAD_DIM // 2
    inv = ROPE_THETA ** (-jnp.arange(half, dtype=F32) / half)
    ang = (past + jnp.arange(t_len)).astype(F32)[:, None] * inv[None, :]
    c, s = jnp.cos(ang), jnp.sin(ang)
    return jnp.concatenate([c, c, c, c], axis=1), jnp.concatenate([-s, s, -s, s], axis=1)


def _dsa_kernel(aq_ref, iq_ref, aviw_ref, kik_ref, v_ref, tri_ref, o_ref,
                skey_ref, half_ref, iqs_ref, iwb_ref, qs_ref, p_ref, m_ref, mlane_ref, acc_ref, kmax_ref,
                *, past, qb, mr, kt, topk):
    i = pl.program_id(1)
    q0 = past + i * qb
    n_tiles = (q0 + qb + kt - 1) // kt
    row = lax.broadcasted_iota(jnp.int32, (qb, 1), 0)
    vis_end = ((q0 + row) // CHUNK + 1) * CHUNK
    lane_kt = lax.broadcasted_iota(jnp.int32, (1, kt), 1)

    @pl.when(i == 0)
    def _():
        def body(j, m):
            k = kik_ref[pl.ds(pl.multiple_of(j * kt, kt), kt), 0:64].astype(F32)
            return jnp.maximum(m, jnp.sum(k * k, axis=1, keepdims=True))
        norms = lax.fori_loop(0, kik_ref.shape[0] // kt, body, jnp.zeros((kt, 1), F32))
        kmax_ref[...] = jnp.max(norms, axis=0, keepdims=True)

    groups = qb // mr
    stack = lambda g, h: slice((g * A_HEADS + h) * mr, (g * A_HEADS + h + 1) * mr)
    for g in range(groups):
        for h in range(IDX_HEADS):
            iqs_ref[stack(g, h), :] = iq_ref[g * mr:(g + 1) * mr, h * 64:(h + 1) * 64].astype(BF16)
    for h in range(IDX_HEADS):
        iwb_ref[h] = jnp.broadcast_to(aviw_ref[:, 64 + h:65 + h], (qb, LANES))

    def score_tile(j, carry):
        off = pl.multiple_of(j * kt, kt)
        ik = kik_ref[pl.ds(off, kt), 64:128]
        for g in range(groups):
            grows = slice(g * mr, (g + 1) * mr)
            s = lax.dot_general(iqs_ref[g * IDX_HEADS * mr:(g + 1) * IDX_HEADS * mr, :], ik, NT_DIMS,
                                preferred_element_type=F32)
            parts = []
            for c in range(kt // LANES):
                a = jnp.zeros((mr, LANES), F32)
                for h in range(IDX_HEADS):
                    a = a + iwb_ref[h, grows, :] * jnp.maximum(s[h * mr:(h + 1) * mr, c * LANES:(c + 1) * LANES], 0.0)
                parts.append(a)
            acc = jnp.concatenate(parts, axis=1)
            acc = acc + 0.0
            acc = jnp.where(off + lane_kt < vis_end[grows, :], acc, -jnp.inf)
            bits = pltpu.bitcast(acc, jnp.int32)
            key = jnp.where(bits < 0, bits ^ 0x7FFFFFFF, bits)
            skey_ref[grows, pl.ds(off, kt)] = key
            half_ref[grows, pl.ds(off, kt)] = jnp.right_shift(key, 16).astype(jnp.int16)
        return carry

    lax.fori_loop(0, n_tiles, score_tile, 0)

    def count(pred_fn):
        def body(j, cnt):
            off = pl.multiple_of(j * kt, kt)
            hit = pred_fn(skey_ref[:, pl.ds(off, kt)])
            for c in range(kt // LANES):
                cnt = cnt + jnp.where(hit[:, c * LANES:(c + 1) * LANES], 1.0, 0.0)
            return cnt
        cnt = lax.fori_loop(0, n_tiles, body, jnp.zeros((qb, LANES), F32))
        return jnp.sum(cnt, axis=1, keepdims=True)

    def count_half_ge(cand):
        cand16 = jnp.broadcast_to(cand, (qb, LANES)).astype(jnp.int16)
        one = jnp.ones((qb, LANES), jnp.int16)
        nil = jnp.zeros((qb, LANES), jnp.int16)

        def body(j, cnt):
            off = pl.multiple_of(j * kt, kt)
            tile = half_ref[:, pl.ds(off, kt)]
            for c in range(kt // LANES):
                cnt = cnt + jnp.where(tile[:, c * LANES:(c + 1) * LANES] >= cand16, one, nil)
            return cnt
        cnt = lax.fori_loop(0, n_tiles, body, nil)
        return jnp.sum(cnt.astype(F32), axis=1, keepdims=True)

    def bisect16(cnt_min, extra):
        zero = jnp.zeros((qb, 1), jnp.int32)
        cnt = extra + count_half_ge(zero)
        ok = cnt >= topk
        start = (jnp.where(ok, zero, jnp.full((qb, 1), -(2 ** 15), jnp.int32)), jnp.where(ok, cnt, cnt_min))

        def bit_step(it, c):
            t, cnt_t = c
            cand = t | jnp.left_shift(jnp.int32(1), 14 - it)
            cnt = extra + count_half_ge(cand)
            ok = cnt >= topk
            return jnp.where(ok, cand, t), jnp.where(ok, cnt, cnt_t)
        return lax.fori_loop(0, 15, bit_step, start)

    visited = jnp.full((qb, 1), n_tiles * kt, jnp.int32).astype(F32)
    t_hi, cnt_hi = bisect16(visited, 0.0)
    top16 = 2 ** 15 - 1
    above = jnp.where(t_hi == top16, 0.0, count_half_ge(jnp.minimum(t_hi + 1, top16)))

    def low_tile(j, carry):
        off = pl.multiple_of(j * kt, kt)
        key = skey_ref[:, pl.ds(off, kt)]
        low = (key & 0xFFFF) - 2 ** 15
        half_ref[:, pl.ds(off, kt)] = jnp.where(jnp.right_shift(key, 16) == t_hi, low, -(2 ** 15)).astype(jnp.int16)
        return carry

    lax.fori_loop(0, n_tiles, low_tile, 0)
    t_lo, cnt_ge = bisect16(cnt_hi, above)
    thr = jnp.left_shift(t_hi, 16) | (t_lo + 2 ** 15)
    lane_ok = lambda off: off + lane_kt < vis_end

    def select_all_ties(j, carry):
        off = pl.multiple_of(j * kt, kt)
        sel = (skey_ref[:, pl.ds(off, kt)] >= thr) & lane_ok(off)
        skey_ref[:, pl.ds(off, kt)] = pltpu.bitcast(jnp.where(sel, 0.0, -jnp.inf), jnp.int32)
        return carry

    def select_ranked_ties(need):
        def body(j, eq_seen):
            off = pl.multiple_of(j * kt, kt)
            key = skey_ref[:, pl.ds(off, kt)]
            eq = key == thr
            rank = _dot(jnp.where(eq, 1.0, 0.0).astype(BF16), tri_ref[...]) + eq_seen
            sel = ((key > thr) | (eq & (rank <= need))) & lane_ok(off)
            skey_ref[:, pl.ds(off, kt)] = pltpu.bitcast(jnp.where(sel, 0.0, -jnp.inf), jnp.int32)
            return rank[:, kt - 1:kt]
        return body

    def exact_fit():
        lax.fori_loop(0, n_tiles, select_all_ties, 0)

    def surplus_ties():
        need = topk - count(lambda x: x > thr)
        lax.fori_loop(0, n_tiles, select_ranked_ties(need), jnp.zeros((qb, 1), F32))

    lax.cond(jnp.max(jnp.abs(cnt_ge - topk)) == 0.0, exact_fit, surplus_ties)

    for g in range(groups):
        for h in range(A_HEADS):
            qs_ref[stack(g, h), :] = (aq_ref[g * mr:(g + 1) * mr, h * 64:(h + 1) * 64] * (HEAD_DIM ** -0.5)).astype(BF16)

    for g in range(groups):
        grows = slice(g * mr, (g + 1) * mr)
        qs_rows = slice(g * A_HEADS * mr, (g + 1) * A_HEADS * mr)
        g_tiles = (q0 + (g + 1) * mr + kt - 1) // kt

        def max_tile(j, carry):
            off = pl.multiple_of(j * kt, kt)
            bias = pltpu.bitcast(skey_ref[grows, pl.ds(off, kt)], F32)
            k = kik_ref[pl.ds(off, kt), 0:64]
            s = lax.dot_general(qs_ref[qs_rows, :], k, NT_DIMS, preferred_element_type=F32)
            for h in range(A_HEADS):
                sh = s[h * mr:(h + 1) * mr, :] + bias
                mm = sh[:, 0:LANES]
                for c in range(1, kt // LANES):
                    mm = jnp.maximum(mm, sh[:, c * LANES:(c + 1) * LANES])
                mlane_ref[h] = jnp.maximum(mlane_ref[h], mm)
            return carry

        def pv_tile(j, carry):
            off = pl.multiple_of(j * kt, kt)
            bias = pltpu.bitcast(skey_ref[grows, pl.ds(off, kt)], F32)
            k = kik_ref[pl.ds(off, kt), 0:64]
            s = lax.dot_general(qs_ref[qs_rows, :], k, NT_DIMS, preferred_element_type=F32)
            for h in range(A_HEADS):
                rows = slice(h * mr, (h + 1) * mr)
                p_ref[rows, :] = jnp.exp((s[rows, :] + bias) - m_ref[rows, :]).astype(BF16)
            acc_ref[...] += _dot(p_ref[...], v_ref[pl.ds(off, kt), :])
            return carry

        q32 = qs_ref[qs_rows, :].astype(F32)
        m_ref[...] = jnp.sqrt(jnp.sum(q32 * q32, axis=1, keepdims=True) * kmax_ref[...]) * SHIFT_MARGIN
        acc_ref[...] = jnp.zeros(acc_ref.shape, F32)
        lax.fori_loop(0, g_tiles, pv_tile, 0)

        def exact_shift():
            mlane_ref[...] = jnp.full(mlane_ref.shape, -jnp.inf, F32)
            lax.fori_loop(0, g_tiles, max_tile, 0)
            for h in range(A_HEADS):
                m_ref[h * mr:(h + 1) * mr, :] = jnp.max(mlane_ref[h], axis=1, keepdims=True)
            acc_ref[...] = jnp.zeros(acc_ref.shape, F32)
            lax.fori_loop(0, g_tiles, pv_tile, 0)

        lax.cond(jnp.min(acc_ref[:, 64:65]) >= MIN_SOFTMAX_SUM, lambda: None, exact_shift)
        for h in range(A_HEADS):
            a = acc_ref[h * mr:(h + 1) * mr, :]
            o_ref[grows, h * 64:(h + 1) * 64] = (a / a[:, 64:65])[:, 0:64]


def _dsa(aq, iq, aviw, kik_all, v_all, bsz, t_len, past, kt):
    qb = min(DSA_QUERY_ROWS, t_len)
    mr = min(DSA_MATMUL_ROWS, qb)
    nq = t_len // qb
    lp = kik_all.shape[1]
    topk = min(TOPK_MAX, (past + t_len) // 4)
    tri = jnp.triu(jnp.ones((kt, kt), F32)).astype(BF16)
    qrow = lambda w: pl.BlockSpec((qb, w), lambda b, i: (b * nq + i, 0))
    keys = pl.BlockSpec((None, lp, LANES), lambda b, i: (b, 0, 0))
    return pl.pallas_call(
        functools.partial(_dsa_kernel, past=past, qb=qb, mr=mr, kt=kt, topk=topk),
        grid=(bsz, nq),
        in_specs=[qrow(512), qrow(512), qrow(LANES), keys, keys, pl.BlockSpec((kt, kt), lambda b, i: (0, 0))],
        out_specs=qrow(512),
        out_shape=jax.ShapeDtypeStruct((bsz * t_len, 512), F32),
        scratch_shapes=[
            pltpu.VMEM((qb, lp), jnp.int32),
            pltpu.VMEM((qb, lp), jnp.int16),
            pltpu.VMEM((IDX_HEADS * qb, 64), BF16),
            pltpu.VMEM((IDX_HEADS, qb, LANES), F32),
            pltpu.VMEM((A_HEADS * qb, 64), BF16),
            pltpu.VMEM((A_HEADS * mr, kt), BF16),
            pltpu.VMEM((A_HEADS * mr, 1), F32),
            pltpu.VMEM((A_HEADS, mr, LANES), F32),
            pltpu.VMEM((A_HEADS * mr, LANES), F32),
            pltpu.VMEM((1, 1), F32),
        ],
        compiler_params=_params("parallel", "arbitrary"),
        name="dsa",
    )(aq, iq, aviw, kik_all, v_all, tri)


def _retention_kernel(bqk_ref, bv_ref, bg_ref, s0_ref, gn_ref, o_ref, s_out_ref, s_ref, *, n_chunks):
    t = pl.program_id(0)

    @pl.when(t == 0)
    def _():
        s_ref[...] = s0_ref[...]

    n = CHUNK
    ri = lax.broadcasted_iota(jnp.int32, (n, n), 0).astype(F32)
    ci = lax.broadcasted_iota(jnp.int32, (n, n), 1).astype(F32)
    diff = ri - ci
    pos = lax.broadcasted_iota(jnp.int32, (n, 1), 0).astype(F32)
    decays = []
    for h in range(B_HEADS):
        log_g = math.log(1.0 - 2.0 ** (-5.0 - h))
        decays.append((jnp.where(diff >= 0, jnp.exp(jnp.maximum(diff, 0.0) * log_g), 0.0),
                       jnp.exp((pos + 1.0) * log_g), jnp.exp((n - 1.0 - pos) * log_g), math.exp(n * log_g)))
    for c in range(n_chunks):
        rows = slice(c * n, (c + 1) * n)
        for b in range(bqk_ref.shape[0]):
            for h in range(B_HEADS):
                intra, q_decay, k_decay, chunk_decay = decays[h]
                q = bqk_ref[b, rows, h * B_DK:(h + 1) * B_DK]
                k = bqk_ref[b, rows, 256 + h * B_DK:256 + (h + 1) * B_DK]
                v = bv_ref[b, rows, h * B_DV:(h + 1) * B_DV]
                s = s_ref[b, h]
                scores = _dot_f32x3(q, k, NT_DIMS) * intra
                o = _dot_f32x3(scores, v) + _dot_f32x3(q, s) * q_decay
                s_ref[b, h] = s * chunk_decay + _dot_f32x3(k * k_decay, v, TN_DIMS)
                mu = jnp.mean(o, axis=-1, keepdims=True)
                oc = o - mu
                var = jnp.mean(oc * oc, axis=-1, keepdims=True)
                g = bg_ref[b, rows, h * B_DV:(h + 1) * B_DV]
                o_ref[b, rows, h * B_DV:(h + 1) * B_DV] = (
                    oc * lax.rsqrt(var + LN_EPS) * gn_ref[:, h * B_DV:(h + 1) * B_DV] * (g * _sigmoid(g)))

    @pl.when(t == pl.num_programs(0) - 1)
    def _():
        s_out_ref[...] = s_ref[...]


def _retention(bqk, bv, bg, s0, b_gn, bsz, t_len, tt):
    row = pl.BlockSpec((bsz, tt, 512), lambda t: (0, t, 0))
    state = pl.BlockSpec((bsz, B_HEADS, B_DK, B_DV), lambda t: (0, 0, 0, 0))
    o_b, s_new = pl.pallas_call(
        functools.partial(_retention_kernel, n_chunks=tt // CHUNK),
        grid=(t_len // tt,),
        in_specs=[row, row, row, state, pl.BlockSpec((1, 512), lambda t: (0, 0))],
        out_specs=[row, state],
        out_shape=[jax.ShapeDtypeStruct((bsz, t_len, 512), F32),
                   jax.ShapeDtypeStruct((bsz, B_HEADS, B_DK, B_DV), F32)],
        scratch_shapes=[pltpu.VMEM((bsz, B_HEADS, B_DK, B_DV), F32)],
        compiler_params=_params("arbitrary"),
        name="retention",
    )(bqk.reshape(bsz, t_len, 512), bv.reshape(bsz, t_len, 512), bg.reshape(bsz, t_len, 512), s0, b_gn)
    return o_b.reshape(bsz * t_len, 512), s_new


def _out_ln_kernel(oa_ref, ob_ref, x_ref, w_ref, g_ref, b_ref, y_ref):
    y = _dot(oa_ref[...].astype(BF16), w_ref[0:512, :]) + _dot(ob_ref[...].astype(BF16), w_ref[512:1024, :])
    y_ref[...] = _layernorm_rows(ALPHA * x_ref[...] + y, g_ref[...], b_ref[...])


def _out_ln(oa, ob, x2, w_out, g, b, tm):
    n = x2.shape[0]
    row = lambda w: pl.BlockSpec((tm, w), lambda i: (i, 0))
    vec = pl.BlockSpec((1, D_MODEL), lambda i: (0, 0))
    return pl.pallas_call(
        _out_ln_kernel,
        grid=(n // tm,),
        in_specs=[row(512), row(512), row(D_MODEL), pl.BlockSpec((D_MODEL, D_MODEL), lambda i: (0, 0)), vec, vec],
        out_specs=row(D_MODEL),
        out_shape=jax.ShapeDtypeStruct((n, D_MODEL), F32),
        compiler_params=_params("parallel"),
        name="out_ln",
    )(oa, ob, x2, w_out, g, b)


def _out_ln_d_kernel(oc_ref, od_ref, r_ref, kr_ref, v_ref, gate_ref, x_ref, w_ref, hm_ref,
                     lnx_g_ref, lnx_b_ref, rk_ref, g_ref, b_ref, y_ref):
    hm = hm_ref[...]
    o = od_ref[...]
    mu = _dot_exact_rhs(o, hm) * (1.0 / HEAD_DIM)
    oc = o - mu
    var = _dot_exact_rhs(oc * oc, hm) * (1.0 / HEAD_DIM)
    normed = oc * lax.rsqrt(var + D_GN_EPS) * lnx_g_ref[...] + lnx_b_ref[...]
    v = v_ref[...]
    bonus = _dot_exact_rhs(r_ref[...] * kr_ref[...] * rk_ref[...], hm) * v
    od = (normed + bonus) * gate_ref[...]
    y = _dot(oc_ref[...].astype(BF16), w_ref[0:512, :]) + _dot(od.astype(BF16), w_ref[512:1024, :])
    y_ref[...] = _layernorm_rows(ALPHA * x_ref[...] + y, g_ref[...], b_ref[...])


def _head_mask():
    head = jnp.arange(D_W) // HEAD_DIM
    return (head[:, None] == head[None, :]).astype(BF16)


def _out_ln_d(oc, od, r, kr, v, gate, x2, w_out, lnx_g, lnx_b, r_k, g, b, tm):
    n = x2.shape[0]
    row = lambda w: pl.BlockSpec((tm, w), lambda i: (i, 0))
    vec = lambda w: pl.BlockSpec((1, w), lambda i: (0, 0))
    return pl.pallas_call(
        _out_ln_d_kernel,
        grid=(n // tm,),
        in_specs=[row(512)] * 6 + [row(D_MODEL), pl.BlockSpec((D_MODEL, D_MODEL), lambda i: (0, 0)),
                                   pl.BlockSpec((D_W, D_W), lambda i: (0, 0)),
                                   vec(D_W), vec(D_W), vec(D_W), vec(D_MODEL), vec(D_MODEL)],
        out_specs=row(D_MODEL),
        out_shape=jax.ShapeDtypeStruct((n, D_MODEL), F32),
        compiler_params=_params("parallel"),
        name="out_ln_d",
    )(oc, od, r, kr, v, gate, x2, w_out, _head_mask(), lnx_g, lnx_b, r_k, g, b)


MOE_EXPERTS_PER_STEP = 4


def _moe_kernel(x_ref, rw_ref, rb_ref, w13_ref, w2_ref, g_ref, b_ref, y_ref, acc_ref, gate_ref, xb_ref):
    e = pl.program_id(1)
    tm = x_ref.shape[0]
    lane = lax.broadcasted_iota(jnp.int32, (tm, N_EXPERTS), 1)

    lane_f = lane.astype(F32)

    def first_argmax(v):
        m = jnp.max(v, axis=1, keepdims=True)
        idx = jnp.min(jnp.where(v == m, lane_f, float(N_EXPERTS)), axis=1, keepdims=True)
        return m, idx.astype(jnp.int32)

    @pl.when(e == 0)
    def _():
        x = x_ref[...]
        xb_ref[...] = x.astype(BF16)
        aff = _sigmoid(_dot_hi(x, rw_ref[...]))
        biased = aff + rb_ref[...]
        best = jnp.zeros((tm, 1), jnp.int32)
        best_score = jnp.full((tm, 1), -jnp.inf, F32)
        for grp in range(N_GROUPS):
            vg = jnp.where(lane // EXPERTS_PER_GROUP == grp, biased, -jnp.inf)
            top1, idx1 = first_argmax(vg)
            top2, _ = first_argmax(jnp.where(lane == idx1, -jnp.inf, vg))
            score = top1 + top2
            better = score > best_score
            best = jnp.where(better, grp, best)
            best_score = jnp.where(better, score, best_score)
        masked = jnp.where(lane // EXPERTS_PER_GROUP == best, biased, -jnp.inf)
        _, idx1 = first_argmax(masked)
        _, idx2 = first_argmax(jnp.where(lane == idx1, -jnp.inf, masked))
        top_aff = jnp.where((lane == idx1) | (lane == idx2), aff, 0.0)
        gate_ref[...] = top_aff / jnp.sum(top_aff, axis=1, keepdims=True)
        acc_ref[...] = jnp.zeros(acc_ref.shape, F32)

    for k in range(MOE_EXPERTS_PER_STEP):
        h13 = _dot(xb_ref[...], w13_ref[k])
        h1 = h13[:, 0:D_EXPERT]
        h = (h1 * _sigmoid(h1)) * h13[:, D_EXPERT:2 * D_EXPERT]
        gate_e = jnp.sum(jnp.where(lane == e * MOE_EXPERTS_PER_STEP + k, gate_ref[...], 0.0), axis=1, keepdims=True)
        acc_ref[...] += gate_e * _dot(h.astype(BF16), w2_ref[k])

    @pl.when(e == pl.num_programs(1) - 1)
    def _():
        y_ref[...] = _layernorm_rows(ALPHA * x_ref[...] + acc_ref[...], g_ref[...], b_ref[...])


def _moe_ln(x2, router_w, router_b, w13, w2, g, b, tm):
    n = x2.shape[0]
    row = pl.BlockSpec((tm, D_MODEL), lambda i, e: (i, 0))
    vec = pl.BlockSpec((1, D_MODEL), lambda i, e: (0, 0))
    return pl.pallas_call(
        _moe_kernel,
        grid=(n // tm, N_EXPERTS // MOE_EXPERTS_PER_STEP),
        in_specs=[row,
                  pl.BlockSpec((D_MODEL, N_EXPERTS), lambda i, e: (0, 0)),
                  pl.BlockSpec((1, N_EXPERTS), lambda i, e: (0, 0)),
                  pl.BlockSpec((MOE_EXPERTS_PER_STEP, D_MODEL, 2 * D_EXPERT), lambda i, e: (e, 0, 0)),
                  pl.BlockSpec((MOE_EXPERTS_PER_STEP, D_EXPERT, D_MODEL), lambda i, e: (e, 0, 0)),
                  vec, vec],
        out_specs=row,
        out_shape=jax.ShapeDtypeStruct((n, D_MODEL), F32),
        scratch_shapes=[pltpu.VMEM((tm, D_MODEL), F32), pltpu.VMEM((tm, N_EXPERTS), F32),
                        pltpu.VMEM((tm, D_MODEL), BF16)],
        compiler_params=_params("parallel", "arbitrary"),
        name="moe_ln",
    )(x2, router_w, router_b, w13, w2, g, b)


def _proj_cd_kernel(x_ref, w_ref, cq_ref, ck_ref, cv_ref, pd_ref, ck16_ref, cv16_ref):
    xb = x_ref[...].astype(BF16)
    cq_ref[...] = _dot(xb, w_ref[:, 0:512])
    ck = _dot(xb, w_ref[:, 512:1024])
    ck_ref[...] = ck
    ck16_ref[...] = ck.astype(BF16)
    cv = _dot(xb, w_ref[:, 1024:1536])
    cv_ref[...] = cv
    cv16_ref[...] = cv.astype(BF16)
    pd_ref[...] = _dot(xb, w_ref[:, 1536:1536 + D_SHIFT_W])


def _proj_cd(x2, w_bf16, tm):
    n = x2.shape[0]
    row = lambda w: pl.BlockSpec((tm, w), lambda i: (i, 0))
    outs = [512, 512, 512, D_SHIFT_W]
    return pl.pallas_call(
        _proj_cd_kernel,
        grid=(n // tm,),
        in_specs=[row(D_MODEL), pl.BlockSpec((D_MODEL, 1536 + D_SHIFT_W), lambda i: (0, 0))],
        out_specs=[row(w) for w in outs] + [row(512), row(512)],
        out_shape=[jax.ShapeDtypeStruct((n, w), F32) for w in outs] + [jax.ShapeDtypeStruct((n, 512), BF16)] * 2,
        compiler_params=_params("parallel"),
        name="proj_cd",
    )(x2, w_bf16)


STICK_HEADS_PER_STEP = 2


def _stick_kernel(q_ref, k_ref, v_ref, ust_ref, o_ref, *, past, tq, kb):
    i = pl.program_id(2)
    q0 = past + i * tq
    n_kb = (q0 + tq + kb - 1) // kb
    qpos = q0 + lax.broadcasted_iota(jnp.int32, (tq, 1), 0)
    lane_kb = lax.broadcasted_iota(jnp.int32, (1, kb), 1)
    n_heads = STICK_HEADS_PER_STEP
    cols =[slice(hh * HEAD_DIM, (hh + 1) * HEAD_DIM) for hh in range(n_heads)]
    qs = [(q_ref[:, c] * (HEAD_DIM ** -0.5)).astype(BF16) for c in cols]

    def cond(c):
        j, carries, _ = c
        worst = carries[0]
        for carry in carries[1:]:
            worst = jnp.maximum(worst, carry)
        return jnp.logical_and(j >= 0, jnp.max(worst) > EXP_ZERO_BELOW)

    def body(c):
        j, carries, outs = c
        off = pl.multiple_of(j * kb, kb)
        strict = off + lane_kb < qpos
        new_carries, new_outs = [], []
        for hh in range(n_heads):
            k = k_ref[pl.ds(off, kb), cols[hh]]
            v = v_ref[pl.ds(off, kb), cols[hh]]
            z = lax.dot_general(qs[hh], k, NT_DIMS, preferred_element_type=F32)
            sp = _softplus(z)
            log_keep = jnp.where(strict, -sp, 0.0)
            later = carries[hh] + _dot_exact_rhs2(log_keep, ust_ref[...])
            a = jnp.where(strict, jnp.exp((z - sp) + later), 0.0)
            new_outs.append(outs[hh] + _dot(a.astype(BF16), v))
            new_carries.append(carries[hh] + jnp.sum(log_keep, axis=1, keepdims=True))
        return j - 1, tuple(new_carries), tuple(new_outs)

    init = (n_kb - 1, tuple(jnp.zeros((tq, 1), F32) for _ in cols), tuple(jnp.zeros((tq, HEAD_DIM), F32) for _ in cols))
    _, _, outs = lax.while_loop(cond, body, init)
    for hh in range(n_heads):
        o_ref[:, cols[hh]] = outs[hh]


def _stick(cq, k_all, v_all, bsz, t_len, past, kb):
    tq = min(kb, t_len)
    nq = t_len // tq
    lp = k_all.shape[1]
    ust = jnp.tril(jnp.ones((kb, kb), F32), -1).astype(BF16)
    width = STICK_HEADS_PER_STEP * HEAD_DIM
    qrow = pl.BlockSpec((tq, width), lambda b, hp, i: (b * nq + i, hp))
    keys = pl.BlockSpec((None, lp, width), lambda b, hp, i: (b, 0, hp))
    return pl.pallas_call(
        functools.partial(_stick_kernel, past=past, tq=tq, kb=kb),
        grid=(bsz, C_HEADS // STICK_HEADS_PER_STEP, nq),
        in_specs=[qrow, keys, keys, pl.BlockSpec((kb, kb), lambda b, hp, i: (0, 0))],
        out_specs=qrow,
        out_shape=jax.ShapeDtypeStruct((bsz * t_len, 512), F32),
        compiler_params=_params("parallel", "parallel", "arbitrary"),
        name="stick",
    )(cq, k_all, v_all, ust)


def _rwkv_pre_kernel(pd_ref, shift_ref, mu_ref, w0a0_ref, wa2_ref, gw2_ref, kk_ref_, ka_ref, hm_ref,
                     r_out, w_out, kr_out, v_out, kk_out, b_out, g_out, last_ref):
    t = pl.program_id(1)

    @pl.when(t == 0)
    def _():
        last_ref[...] = shift_ref[...]

    pd = pd_ref[...]
    tm = pd.shape[0]
    rolled = pltpu.roll(pd, 1, 0)
    first_row = lax.broadcasted_iota(jnp.int32, (tm, 1), 0) == 0
    prev = jnp.where(first_row, last_ref[...], rolled)
    last_ref[...] = pd[tm - 1:tm, :]
    pm = pd + (prev - pd) * mu_ref[...]
    r = pm[:, 0:512]
    k = pm[:, 512:1024]
    v = pm[:, 1024:1536]
    lwa = pm[:, 1536:1664]
    lg = pm[:, 1664:1792]
    lane = lax.broadcasted_iota(jnp.int32, lwa.shape, 1)
    lwa = jnp.where(lane < D_LORA_W, jnp.tanh(lwa), lwa)
    pre = w0a0_ref[...] + _dot_hi(lwa, wa2_ref[...])
    w_log = -_softplus(-pre[:, 0:512]) - 0.5
    decay = jnp.exp(-jnp.exp(w_log))
    a = _sigmoid(pre[:, 512:1024])
    g = _dot_hi(_sigmoid(lg), gw2_ref[...])
    kk = k * kk_ref_[...]
    kk = kk * lax.rsqrt(_dot_exact_rhs(kk * kk, hm_ref[...]) + 1e-12)
    r_out[...] = r
    w_out[...] = decay
    kr_out[...] = k * (1.0 + (a - 1.0) * ka_ref[...])
    v_out[...] = v
    kk_out[...] = kk
    b_out[...] = kk * a
    g_out[...] = g


def _rwkv_pre(pd, shift, mu, w0a0, wa2, gw2, k_k, k_a, bsz, t_len, tm):
    nt = t_len // tm
    row = lambda w: pl.BlockSpec((tm, w), lambda b, t: (b * nt + t, 0))
    const = lambda s: pl.BlockSpec(s, lambda b, t: (0,) * len(s))
    return pl.pallas_call(
        _rwkv_pre_kernel,
        grid=(bsz, nt),
        in_specs=[row(D_SHIFT_W), pl.BlockSpec((None, 1, D_SHIFT_W), lambda b, t: (b, 0, 0)),
                  const((1, D_SHIFT_W)), const((1, 2 * D_W)), const((LANES, 2 * D_W)), const((D_LORA_G, D_W)),
                  const((1, D_W)), const((1, D_W)), const((D_W, D_W))],
        out_specs=[row(D_W)] * 7,
        out_shape=[jax.ShapeDtypeStruct((bsz * t_len, D_W), F32)] * 7,
        scratch_shapes=[pltpu.VMEM((1, D_SHIFT_W), F32)],
        compiler_params=_params("parallel", "arbitrary"),
        name="rwkv_pre",
    )(pd, shift, mu, w0a0, wa2, gw2, k_k, k_a, _head_mask())


RWKV_KL = HEAD_DIM // 2


def _rwkv_scan_kernel(kk_ref, w_ref, b_ref, kr_ref, r_ref, v_ref, s0_ref, o_ref, s_out_ref, s_ref, *, tb):
    g = pl.program_id(0)

    @pl.when(g == 0)
    def _():
        s_ref[...] = s0_ref[...]

    def both_halves(x):
        return x + pltpu.roll(x, LANES // 2, 1)

    def step(t, carry):
        for vrows in (slice(0, HEAD_DIM // 2), slice(HEAD_DIM // 2, HEAD_DIM)):
            acc = [jnp.zeros((HEAD_DIM // 2, LANES), F32) for _ in range(2)]
            for kl in range(RWKV_KL):
                acc[kl % 2] = acc[kl % 2] + s_ref[kl, vrows, :] * kk_ref[t, kl:kl + 1, :]
            sa = both_halves(acc[0] + acc[1])
            v = v_ref[t, vrows, :]
            out = [jnp.zeros((HEAD_DIM // 2, LANES), F32) for _ in range(2)]
            for kl in range(RWKV_KL):
                row = slice(kl, kl + 1)
                s = s_ref[kl, vrows, :] * w_ref[t, row, :] - sa * b_ref[t, row, :] + v * kr_ref[t, row, :]
                s_ref[kl, vrows, :] = s
                out[kl % 2] = out[kl % 2] + s * r_ref[t, row, :]
            o_ref[t, vrows, :] = both_halves(out[0] + out[1])
        return carry

    lax.fori_loop(0, tb, step, 0)

    @pl.when(g == pl.num_programs(0) - 1)
    def _():
        s_out_ref[...] = s_ref[...]


def _rwkv_scan(kk, w, b, kr, r, v, s0, t_len, tb):
    keyed = pl.BlockSpec((tb, RWKV_KL, LANES), lambda g: (g, 0, 0))
    valued = pl.BlockSpec((tb, HEAD_DIM, LANES), lambda g: (g, 0, 0))
    state = pl.BlockSpec((RWKV_KL, HEAD_DIM, LANES), lambda g: (0, 0, 0))
    return pl.pallas_call(
        functools.partial(_rwkv_scan_kernel, tb=tb),
        grid=(t_len // tb,),
        in_specs=[keyed] * 5 + [valued, state],
        out_specs=[valued, state],
        out_shape=[jax.ShapeDtypeStruct((t_len, HEAD_DIM, LANES), F32),
                   jax.ShapeDtypeStruct((RWKV_KL, HEAD_DIM, LANES), F32)],
        scratch_shapes=[pltpu.VMEM((RWKV_KL, HEAD_DIM, LANES), F32)],
        compiler_params=_params("arbitrary"),
        name="rwkv_scan",
    )(kk, w, b, kr, r, v, s0)


def _to_scan_keyed(x2, bsz, t_len):
    y = x2.reshape(bsz, t_len, D_HEADS, 2, RWKV_KL).transpose(1, 4, 3, 0, 2)
    return y.reshape(t_len, RWKV_KL, 2 * bsz * D_HEADS)


def _to_scan_valued(x2, bsz, t_len):
    y = x2.reshape(bsz, t_len, D_HEADS, HEAD_DIM).transpose(1, 3, 0, 2).reshape(t_len, HEAD_DIM, bsz * D_HEADS)
    return jnp.concatenate([y, y], axis=-1)


def _from_scan_valued(y, bsz, t_len):
    y = y[:, :, 0:bsz * D_HEADS].reshape(t_len, HEAD_DIM, bsz, D_HEADS)
    return y.transpose(2, 0, 3, 1).reshape(bsz * t_len, D_W)


def _state_to_scan(s, bsz):
    y = s.reshape(bsz, D_HEADS, HEAD_DIM, 2, RWKV_KL).transpose(4, 2, 3, 0, 1)
    return y.reshape(RWKV_KL, HEAD_DIM, 2 * bsz * D_HEADS)


def _state_from_scan(y, bsz):
    return y.reshape(RWKV_KL, HEAD_DIM, 2, bsz, D_HEADS).transpose(3, 4, 1, 2, 0).reshape(
        bsz, D_HEADS, HEAD_DIM, HEAD_DIM)


def _pad_keys(x, mult):
    pad = (-x.shape[1]) % mult
    return x if pad == 0 else jnp.pad(x, ((0, 0), (0, pad), (0, 0)))


def _tile(n, cap):
    return min(n, cap)


def _run_group(x, past, p):
    bsz, t_len, _ = x.shape
    assert bsz * D_HEADS * 2 == LANES, "rwkv scan packs (key half, batch, head) into the lane axis"
    n = bsz * t_len
    x2 = x.reshape(n, D_MODEL)
    past_len = 0 if past is None else past[0].shape[2]
    tm = _tile(t_len, 512)

    cos, sin = _rope_tables(past_len, t_len)
    aq, iq, bqk, kik, aviw, bv, bg, kik16, v16 = _proj_ab(x2, p['w_in_ab'], cos, sin, t_len, tm)
    ak = kik[:, 0:64].reshape(1, bsz, t_len, 64)
    ik = kik[:, 64:128].reshape(1, bsz, t_len, 64)
    av = aviw[:, 0:64].reshape(1, bsz, t_len, 64)
    kik_all = kik16.reshape(bsz, t_len, LANES)
    v_all = v16.reshape(bsz, t_len, LANES)
    ones_col = (jnp.arange(LANES) == 64).astype(F32)
    if past is None:
        s_b = jnp.zeros((bsz, B_HEADS, B_DK, B_DV), F32)
    else:
        pk, pv, pik, sb = past[0][0], past[1][0], past[2][0], past[3][0]
        kik_all = jnp.concatenate([jnp.concatenate([pk, pik], axis=-1).astype(BF16), kik_all], axis=1)
        pv_slab = jnp.concatenate([pv, jnp.broadcast_to(ones_col[64:], pv.shape)], axis=-1)
        v_all = jnp.concatenate([pv_slab.astype(BF16), v_all], axis=1)
        s_b = sb
    kt = 1024
    o_a = _dsa(aq, iq, aviw, _pad_keys(kik_all, kt), _pad_keys(v_all, kt), bsz, t_len, past_len, kt)
    o_b, s_b_new = _retention(bqk, bv, bg, s_b, p['b_gn'], bsz, t_len, _tile(t_len, 2 * CHUNK))
    x2 = _out_ln(o_a, o_b, x2, p['w_out_ab'], p['ln_g'][0, 0][None], p['ln_b'][0, 0][None], tm)
    tm_moe = _tile(n, 1024)
    x2 = _moe_ln(x2, p['router_w'], p['router_b'], p['w13'][0], p['w2'][0],
                 p['ln_g'][0, 1][None], p['ln_b'][0, 1][None], tm_moe)

    cq, ck, cv, pd, ck16, cv16 = _proj_cd(x2, p['w_in_cd'], tm)
    ck_all = ck16.reshape(bsz, t_len, 512)
    cv_all = cv16.reshape(bsz, t_len, 512)
    if past is None:
        s_d = jnp.zeros((bsz, D_HEADS, HEAD_DIM, HEAD_DIM), F32)
        shift = jnp.zeros((bsz, 1, D_SHIFT_W), F32)
    else:
        ck_all = jnp.concatenate([past[4][0].reshape(bsz, past_len, 512).astype(BF16), ck_all], axis=1)
        cv_all = jnp.concatenate([past[5][0].reshape(bsz, past_len, 512).astype(BF16), cv_all], axis=1)
        s_d, shift = past[6][0], past[7][0]
    kb = 256
    o_c = _stick(cq, _pad_keys(ck_all, kb), _pad_keys(cv_all, kb), bsz, t_len, past_len, kb)
    r, w, kr, v, kk, b, g = _rwkv_pre(pd, shift, p['d_mu'], p['d_w0a0'], p['d_wa2'], p['d_g2'],
                                      p['d_k_k'], p['d_k_a'], bsz, t_len, tm)
    keyed = [_to_scan_keyed(u, bsz, t_len) for u in (kk, w, b, kr, r)]
    o_scan, s_scan = _rwkv_scan(*keyed, _to_scan_valued(v, bsz, t_len), _state_to_scan(s_d, bsz),
                                t_len, _tile(t_len, 64))
    o_d = _from_scan_valued(o_scan, bsz, t_len)
    s_d_new = _state_from_scan(s_scan, bsz)
    x2 = _out_ln_d(o_c, o_d, r, kr, v, g, x2, p['w_out_cd'], p['d_lnx_g'], p['d_lnx_b'], p['d_r_k'],
                   p['ln_g'][1, 0][None], p['ln_b'][1, 0][None], tm)
    x2 = _moe_ln(x2, p['router_w'], p['router_b'], p['w13'][1], p['w2'][1],
                 p['ln_g'][1, 1][None], p['ln_b'][1, 1][None], tm_moe)

    states = (ak, av, ik, s_b_new[None],
              ck.reshape(1, bsz, t_len, C_HEADS, HEAD_DIM), cv.reshape(1, bsz, t_len, C_HEADS, HEAD_DIM),
              s_d_new[None], pd.reshape(bsz, t_len, D_SHIFT_W)[:, -1:][None])
    return x2.reshape(bsz, t_len, D_MODEL), states


def kernel(x_prompt, x_sample, cache_a_k, cache_a_v, cache_a_idx_k, state_b, cache_c_k, cache_c_v, state_d, state_d_shift, w_in_ab, w_out_ab, b_gn, w_in_cd, w_out_cd, d_mu, d_w0, d_w2, d_a0, d_a2, d_g2, d_k_k, d_k_a, d_r_k, d_lnx_g, d_lnx_b, ln_g, ln_b, router_w, router_b, moe_w1, moe_w3, moe_w2):
    zeros_w = jnp.zeros((D_LORA_W, D_W), F32)
    p = {
        'w_in_ab': _pack_w_in_ab(w_in_ab[0]),
        'w_out_ab': w_out_ab[0].astype(BF16),
        'b_gn': b_gn,
        'w_in_cd': w_in_cd[0].astype(BF16),
        'w_out_cd': w_out_cd[0].astype(BF16),
        'd_mu': d_mu,
        'd_w0a0': jnp.concatenate([d_w0, d_a0], axis=1),
        'd_wa2': jnp.concatenate([jnp.concatenate([d_w2[0], zeros_w], axis=1),
                                  jnp.concatenate([zeros_w, d_a2[0]], axis=1)], axis=0),
        'd_g2': d_g2[0],
        'd_k_k': d_k_k, 'd_k_a': d_k_a, 'd_r_k': d_r_k, 'd_lnx_g': d_lnx_g, 'd_lnx_b': d_lnx_b,
        'ln_g': ln_g, 'ln_b': ln_b,
        'router_w': router_w, 'router_b': router_b[None],
        'w13': jnp.concatenate([moe_w1, moe_w3], axis=-1).astype(BF16),
        'w2': moe_w2.astype(BF16),
    }
    y_p, sp = _run_group(x_prompt, None, p)
    past = (cache_a_k, cache_a_v, cache_a_idx_k, state_b, cache_c_k, cache_c_v, state_d, state_d_shift)
    y_s, ss = _run_group(x_sample, past, p)
    return (y_p, y_s, sp[0], sp[1], sp[2], ss[0], ss[1], ss[2], sp[3], ss[3], sp[4], sp[5], ss[4], ss[5],
            sp[6], ss[6], sp[7], ss[7])
```

```python
import functools
import math

import jax
import jax.numpy as jnp
import numpy as np
from jax import lax
from jax.experimental import pallas as pl
from jax.experimental.pallas import tpu as pltpu

F32 = jnp.float32
BF16 = jnp.bfloat16

D_MODEL = 1024
CHUNK = 64
DSA_QUERY_ROWS = 128
DSA_MATMUL_ROWS = 128
ROPE_THETA = 10000.0
HEAD_DIM = 64
LN_EPS = 1e-5
A_HEADS = 8
IDX_HEADS = 8
TOPK_MAX = 256
B_HEADS = 4
B_DK = 64
B_DV = 128
C_HEADS = 8
D_HEADS = 8
D_LORA_W = 64
D_LORA_A = 64
D_LORA_G = 128
D_GN_EPS = 64e-5
N_EXPERTS = 16
N_GROUPS = 4
EXPERTS_PER_GROUP = 4
D_EXPERT = 256
DEPTH = 2
ALPHA = (2 * DEPTH) ** 0.25
D_W = D_HEADS * HEAD_DIM
D_SHIFT_W = 3 * D_W + D_LORA_W + D_LORA_A + D_LORA_G

LANES = 128
SUBLANES = 8
VMEM_LIMIT_BYTES = 56 * 1024 * 1024

SHIFT_MARGIN = 1.001
MIN_SOFTMAX_SUM = 1e-30
EXP_ZERO_BELOW = -104.0

NT_DIMS = (((1,), (1,)), ((), ()))
TN_DIMS = (((0,), (0,)), ((), ()))


def _params(*sem):
    return pltpu.CompilerParams(dimension_semantics=sem, vmem_limit_bytes=VMEM_LIMIT_BYTES)


def _dot(a, b):
    return jnp.dot(a, b, preferred_element_type=F32)


def _dot_hi(a, b):
    return jnp.dot(a, b, preferred_element_type=F32, precision=lax.Precision.HIGHEST)


def _dot_f32x3(a, b, dims=(((1,), (0,)), ((), ()))):
    a_hi = a.astype(BF16)
    b_hi = b.astype(BF16)
    a_lo = (a - a_hi.astype(F32)).astype(BF16)
    b_lo = (b - b_hi.astype(F32)).astype(BF16)
    dot = lambda x, y: lax.dot_general(x, y, dims, preferred_element_type=F32)
    return dot(a_hi, b_hi) + dot(a_hi, b_lo) + dot(a_lo, b_hi)


def _split3(x):
    h1 = x.astype(BF16)
    r1 = x - h1.astype(F32)
    h2 = r1.astype(BF16)
    r2 = r1 - h2.astype(F32)
    return h1, h2, r2.astype(BF16)


def _dot_exact_rhs(x, m01):
    h1, h2, h3 = _split3(x)
    return _dot(h1, m01) + _dot(h2, m01) + _dot(h3, m01)


def _dot_exact_rhs2(x, m01):
    h1 = x.astype(BF16)
    h2 = (x - h1.astype(F32)).astype(BF16)
    return _dot(h1, m01) + _dot(h2, m01)


def _layernorm_rows(x, g, b):
    mu = jnp.mean(x, axis=-1, keepdims=True)
    xc = x - mu
    var = jnp.mean(xc * xc, axis=-1, keepdims=True)
    return xc * lax.rsqrt(var + LN_EPS) * g + b


def _sigmoid(x):
    return 1.0 / (1.0 + jnp.exp(-x))


def _softplus(x):
    return jnp.maximum(x, 0.0) + jnp.log(1.0 + jnp.exp(-jnp.abs(x)))


def _rope_slab(x, cos, sin_signed):
    lane = lax.broadcasted_iota(jnp.int32, x.shape, 1)
    first_half = (lane % HEAD_DIM) < (HEAD_DIM // 2)
    swapped = jnp.where(first_half, pltpu.roll(x, LANES - HEAD_DIM // 2, 1), pltpu.roll(x, HEAD_DIM // 2, 1))
    return x * cos + swapped * sin_signed


AB_ROPED = 1664
AB_PACKED = 2816


def _pack_w_in_ab(w):
    aq, ak, av, iq, ik, iw, bq, bk, bv, bg = jnp.split(
        w, [512, 576, 640, 1152, 1216, 1224, 1480, 1736, 2248], axis=1)
    pad = jnp.zeros((w.shape[0], LANES - 64 - IDX_HEADS), w.dtype)
    return jnp.concatenate([aq, iq, bq, bk, ak, ik, av, iw, pad, bv, bg], axis=1).astype(BF16)


def _proj_ab_kernel(x_ref, w_ref, cos_ref, sin_ref, aq_ref, iq_ref, bqk_ref, kik_ref, aviw_ref, bv_ref, bg_ref,
                    kik16_ref, v16_ref):
    xb = x_ref[...].astype(BF16)
    cos = cos_ref[...]
    sin = sin_ref[...]

    def roped(col0, width, scale_from=None):
        y = _dot(xb, w_ref[:, col0:col0 + width])
        parts = []
        for c in range(width // LANES):
            slab = _rope_slab(y[:, c * LANES:(c + 1) * LANES], cos, sin)
            if scale_from is not None and c * LANES >= scale_from:
                slab = slab * (B_DK ** -0.5)
            parts.append(slab)
        return parts

    for c, slab in enumerate(roped(0, 512)):
        aq_ref[:, c * LANES:(c + 1) * LANES] = slab
    for c, slab in enumerate(roped(512, 512)):
        iq_ref[:, c * LANES:(c + 1) * LANES] = slab
    for c, slab in enumerate(roped(1024, 512, scale_from=256)):
        bqk_ref[:, c * LANES:(c + 1) * LANES] = slab
    kik = roped(1536, LANES)[0]
    kik_ref[...] = kik
    kik16_ref[...] = kik.astype(BF16)
    aviw = _dot(xb, w_ref[:, AB_ROPED:AB_ROPED + LANES])
    lane = lax.broadcasted_iota(jnp.int32, aviw.shape, 1)
    is_iw = (lane >= 64) & (lane < 64 + IDX_HEADS)
    aviw_ref[...] = jnp.where(is_iw, aviw * ((IDX_HEADS * HEAD_DIM) ** -0.5), aviw)
    v16_ref[...] = jnp.where(lane < 64, aviw, jnp.where(lane == 64, 1.0, 0.0)).astype(BF16)
    bv_ref[...] = _dot(xb, w_ref[:, 1792:2304])
    bg_ref[...] = _dot(xb, w_ref[:, 2304:2816])


def _proj_ab(x2, w_packed, cos, sin, t_len, tm):
    n = x2.shape[0]
    nt = t_len // tm
    row = lambda w: pl.BlockSpec((tm, w), lambda i: (i, 0))
    tab = pl.BlockSpec((tm, LANES), lambda i: (i % nt, 0))
    outs = [512, 512, 512, LANES, LANES, 512, 512]
    return pl.pallas_call(
        _proj_ab_kernel,
        grid=(n // tm,),
        in_specs=[row(D_MODEL), pl.BlockSpec((D_MODEL, AB_PACKED), lambda i: (0, 0)), tab, tab],
        out_specs=[row(w) for w in outs] + [row(LANES), row(LANES)],
        out_shape=[jax.ShapeDtypeStruct((n, w), F32) for w in outs] + [jax.ShapeDtypeStruct((n, LANES), BF16)] * 2,
        compiler_params=_params("parallel"),
        name="proj_ab",
    )(x2, w_packed, cos, sin)


def _rope_tables(past, t_len):
    half = HEAD_DIM // 2
    inv = ROPE_THETA ** (-jnp.arange(half, dtype=F32) / half)
    ang = (past + jnp.arange(t_len)).astype(F32)[:, None] * inv[None, :]
    c, s = jnp.cos(ang), jnp.sin(ang)
    return jnp.concatenate([c, c, c, c], axis=1), jnp.concatenate([-s, s, -s, s], axis=1)


def _dsa_kernel(aq_ref, iq_ref, aviw_ref, kik_ref, v_ref, tri_ref, o_ref,
                skey_ref, skey_t_ref, half_ref, iqs_ref, iwb_ref, qs_ref, p_ref, m_ref, mlane_ref, acc_ref, kmax_ref,
                *, past, qb, mr, kt, topk, transposed):
    i = pl.program_id(1)
    q0 = past + i * qb
    n_tiles = (q0 + qb + kt - 1) // kt
    row = lax.broadcasted_iota(jnp.int32, (qb, 1), 0)
    vis_end = ((q0 + row) // CHUNK + 1) * CHUNK
    lane_kt = lax.broadcasted_iota(jnp.int32, (1, kt), 1)

    @pl.when(i == 0)
    def _():
        def body(j, m):
            k = kik_ref[pl.ds(pl.multiple_of(j * kt, kt), kt), 0:64].astype(F32)
            return jnp.maximum(m, jnp.sum(k * k, axis=1, keepdims=True))
        norms = lax.fori_loop(0, kik_ref.shape[0] // kt, body, jnp.zeros((kt, 1), F32))
        kmax_ref[...] = jnp.max(norms, axis=0, keepdims=True)

    groups = qb // mr
    stack = lambda g, h: slice((g * A_HEADS + h) * mr, (g * A_HEADS + h + 1) * mr)
    for g in range(groups):
        for h in range(IDX_HEADS):
            iqs_ref[stack(g, h), :] = iq_ref[g * mr:(g + 1) * mr, h * 64:(h + 1) * 64].astype(BF16)
    for h in range(IDX_HEADS):
        iwb_ref[h] = jnp.broadcast_to(aviw_ref[:, 64 + h:65 + h], (qb, LANES))

    def score_tile(j, carry):
        off = pl.multiple_of(j * kt, kt)
        ik = kik_ref[pl.ds(off, kt), 64:128]
        for g in range(groups):
            grows = slice(g * mr, (g + 1) * mr)
            s = lax.dot_general(iqs_ref[g * IDX_HEADS * mr:(g + 1) * IDX_HEADS * mr, :], ik, NT_DIMS,
                                preferred_element_type=F32)
            parts = []
            for c in range(kt // LANES):
                a = jnp.zeros((mr, LANES), F32)
                for h in range(IDX_HEADS):
                    a = a + iwb_ref[h, grows, :] * jnp.maximum(s[h * mr:(h + 1) * mr, c * LANES:(c + 1) * LANES], 0.0)
                parts.append(a)
            acc = jnp.concatenate(parts, axis=1)
            acc = acc + 0.0
            acc = jnp.where(off + lane_kt < vis_end[grows, :], acc, -jnp.inf)
            bits = pltpu.bitcast(acc, jnp.int32)
            key = jnp.where(bits < 0, bits ^ 0x7FFFFFFF, bits)
            skey_ref[grows, pl.ds(off, kt)] = key
            if transposed:
                for c in range(kt // LANES):
                    chunk = pltpu.bitcast(key[:, c * LANES:(c + 1) * LANES], F32).T
                    key_t = pltpu.bitcast(chunk, jnp.int32)
                    rows = pl.ds(off + c * LANES, LANES)
                    skey_t_ref[rows, :] = key_t
                    half_ref[rows, :] = jnp.right_shift(key_t, 16).astype(jnp.int16)
            else:
                half_ref[grows, pl.ds(off, kt)] = jnp.right_shift(key, 16).astype(jnp.int16)
        return carry

    lax.fori_loop(0, n_tiles, score_tile, 0)

    def count(pred_fn):
        def body(j, cnt):
            off = pl.multiple_of(j * kt, kt)
            hit = pred_fn(skey_ref[:, pl.ds(off, kt)])
            for c in range(kt // LANES):
                cnt = cnt + jnp.where(hit[:, c * LANES:(c + 1) * LANES], 1.0, 0.0)
            return cnt
        cnt = lax.fori_loop(0, n_tiles, body, jnp.zeros((qb, LANES), F32))
        return jnp.sum(cnt, axis=1, keepdims=True)

    vec = (1, LANES) if transposed else (qb, 1)
    pack = 2 * SUBLANES

    def count_half_ge(cand):
        if transposed:
            cand16 = jnp.broadcast_to(cand, (pack, LANES)).astype(jnp.int16)
            one = jnp.ones((pack, LANES), jnp.int16)
            nil = jnp.zeros((pack, LANES), jnp.int16)

            def body(j, cnts):
                tile = half_ref[pl.ds(pl.multiple_of(j * kt, kt), kt), :]
                cnts = list(cnts)
                for c in range(kt // pack):
                    cnts[c % 4] = cnts[c % 4] + jnp.where(tile[c * pack:(c + 1) * pack, :] >= cand16, one, nil)
                return tuple(cnts)
            cnts = lax.fori_loop(0, n_tiles, body, (nil, nil, nil, nil))
            cnt = (cnts[0] + cnts[1]) + (cnts[2] + cnts[3])
            return jnp.sum(cnt.astype(F32), axis=0, keepdims=True)
        cand16 = jnp.broadcast_to(cand, (qb, LANES)).astype(jnp.int16)
        one = jnp.ones((qb, LANES), jnp.int16)
        nil = jnp.zeros((qb, LANES), jnp.int16)

        def body(j, cnt):
            off = pl.multiple_of(j * kt, kt)
            tile = half_ref[:, pl.ds(off, kt)]
            for c in range(kt // LANES):
                cnt = cnt + jnp.where(tile[:, c * LANES:(c + 1) * LANES] >= cand16, one, nil)
            return cnt
        cnt = lax.fori_loop(0, n_tiles, body, nil)
        return jnp.sum(cnt.astype(F32), axis=1, keepdims=True)

    def bisect16(cnt_min, extra):
        zero = jnp.zeros(vec, jnp.int32)
        cnt = extra + count_half_ge(zero)
        ok = cnt >= topk
        start = (jnp.where(ok, zero, jnp.full(vec, -(2 ** 15), jnp.int32)), jnp.where(ok, cnt, cnt_min))

        def bit_step(it, c):
            t, cnt_t = c
            cand = t | jnp.left_shift(jnp.int32(1), 14 - it)
            cnt = extra + count_half_ge(cand)
            ok = cnt >= topk
            return jnp.where(ok, cand, t), jnp.where(ok, cnt, cnt_t)
        return lax.fori_loop(0, 15, bit_step, start)

    visited = jnp.full(vec, n_tiles * kt, jnp.int32).astype(F32)
    t_hi, cnt_hi = bisect16(visited, 0.0)
    top16 = 2 ** 15 - 1
    above = jnp.where(t_hi == top16, 0.0, count_half_ge(jnp.minimum(t_hi + 1, top16)))

    def low_tile(j, carry):
        off = pl.multiple_of(j * kt, kt)
        at = (pl.ds(off, kt), slice(None)) if transposed else (slice(None), pl.ds(off, kt))
        key = skey_t_ref[at] if transposed else skey_ref[at]
        low = (key & 0xFFFF) - 2 ** 15
        half_ref[at] = jnp.where(jnp.right_shift(key, 16) == t_hi, low, -(2 ** 15)).astype(jnp.int16)
        return carry

    lax.fori_loop(0, n_tiles, low_tile, 0)
    t_lo, cnt_ge = bisect16(cnt_hi, above)
    thr = jnp.left_shift(t_hi, 16) | (t_lo + 2 ** 15)
    if transposed:
        square = pltpu.bitcast(jnp.broadcast_to(thr, (LANES, LANES)), F32).T
        thr = pltpu.bitcast(square, jnp.int32)[:, 0:1]
    lane_ok = lambda off: off + lane_kt < vis_end

    def select_all_ties(j, carry):
        off = pl.multiple_of(j * kt, kt)
        sel = (skey_ref[:, pl.ds(off, kt)] >= thr) & lane_ok(off)
        skey_ref[:, pl.ds(off, kt)] = pltpu.bitcast(jnp.where(sel, 0.0, -jnp.inf), jnp.int32)
        return carry

    def select_ranked_ties(need):
        def body(j, eq_seen):
            off = pl.multiple_of(j * kt, kt)
            key = skey_ref[:, pl.ds(off, kt)]
            eq = key == thr
            rank = _dot(jnp.where(eq, 1.0, 0.0).astype(BF16), tri_ref[...]) + eq_seen
            sel = ((key > thr) | (eq & (rank <= need))) & lane_ok(off)
            skey_ref[:, pl.ds(off, kt)] = pltpu.bitcast(jnp.where(sel, 0.0, -jnp.inf), jnp.int32)
            return rank[:, kt - 1:kt]
        return body

    def exact_fit():
        lax.fori_loop(0, n_tiles, select_all_ties, 0)

    def surplus_ties():
        need = topk - count(lambda x: x > thr)
        lax.fori_loop(0, n_tiles, select_ranked_ties(need), jnp.zeros((qb, 1), F32))

    lax.cond(jnp.max(jnp.abs(cnt_ge - topk)) == 0.0, exact_fit, surplus_ties)

    for g in range(groups):
        for h in range(A_HEADS):
            qs_ref[stack(g, h), :] = (aq_ref[g * mr:(g + 1) * mr, h * 64:(h + 1) * 64] * (HEAD_DIM ** -0.5)).astype(BF16)

    for g in range(groups):
        grows = slice(g * mr, (g + 1) * mr)
        qs_rows = slice(g * A_HEADS * mr, (g + 1) * A_HEADS * mr)
        g_tiles = (q0 + (g + 1) * mr + kt - 1) // kt

        def max_tile(j, carry):
            off = pl.multiple_of(j * kt, kt)
            bias = pltpu.bitcast(skey_ref[grows, pl.ds(off, kt)], F32)
            k = kik_ref[pl.ds(off, kt), 0:64]
            s = lax.dot_general(qs_ref[qs_rows, :], k, NT_DIMS, preferred_element_type=F32)
            for h in range(A_HEADS):
                sh = s[h * mr:(h + 1) * mr, :] + bias
                mm = sh[:, 0:LANES]
                for c in range(1, kt // LANES):
                    mm = jnp.maximum(mm, sh[:, c * LANES:(c + 1) * LANES])
                mlane_ref[h] = jnp.maximum(mlane_ref[h], mm)
            return carry

        def pv_tile(j, carry):
            off = pl.multiple_of(j * kt, kt)
            bias = pltpu.bitcast(skey_ref[grows, pl.ds(off, kt)], F32)
            k = kik_ref[pl.ds(off, kt), 0:64]
            s = lax.dot_general(qs_ref[qs_rows, :], k, NT_DIMS, preferred_element_type=F32)
            for h in range(A_HEADS):
                rows = slice(h * mr, (h + 1) * mr)
                p_ref[rows, :] = jnp.exp((s[rows, :] + bias) - m_ref[rows, :]).astype(BF16)
            acc_ref[...] += _dot(p_ref[...], v_ref[pl.ds(off, kt), :])
            return carry

        q32 = qs_ref[qs_rows, :].astype(F32)
        m_ref[...] = jnp.sqrt(jnp.sum(q32 * q32, axis=1, keepdims=True) * kmax_ref[...]) * SHIFT_MARGIN
        acc_ref[...] = jnp.zeros(acc_ref.shape, F32)
        lax.fori_loop(0, g_tiles, pv_tile, 0)

        def exact_shift():
            mlane_ref[...] = jnp.full(mlane_ref.shape, -jnp.inf, F32)
            lax.fori_loop(0, g_tiles, max_tile, 0)
            for h in range(A_HEADS):
                m_ref[h * mr:(h + 1) * mr, :] = jnp.max(mlane_ref[h], axis=1, keepdims=True)
            acc_ref[...] = jnp.zeros(acc_ref.shape, F32)
            lax.fori_loop(0, g_tiles, pv_tile, 0)

        lax.cond(jnp.min(acc_ref[:, 64:65]) >= MIN_SOFTMAX_SUM, lambda: None, exact_shift)
        for h in range(A_HEADS):
            a = acc_ref[h * mr:(h + 1) * mr, :]
            o_ref[grows, h * 64:(h + 1) * 64] = (a / a[:, 64:65])[:, 0:64]


def _dsa(aq, iq, aviw, kik_all, v_all, bsz, t_len, past, kt):
    qb = min(DSA_QUERY_ROWS, t_len)
    mr = min(DSA_MATMUL_ROWS, qb)
    transposed = qb == LANES and mr == qb
    nq = t_len // qb
    lp = kik_all.shape[1]
    topk = min(TOPK_MAX, (past + t_len) // 4)
    tri = jnp.triu(jnp.ones((kt, kt), F32)).astype(BF16)
    qrow = lambda w: pl.BlockSpec((qb, w), lambda b, i: (b * nq + i, 0))
    keys = pl.BlockSpec((None, lp, LANES), lambda b, i: (b, 0, 0))
    return pl.pallas_call(
        functools.partial(_dsa_kernel, past=past, qb=qb, mr=mr, kt=kt, topk=topk, transposed=transposed),
        grid=(bsz, nq),
        in_specs=[qrow(512), qrow(512), qrow(LANES), keys, keys, pl.BlockSpec((kt, kt), lambda b, i: (0, 0))],
        out_specs=qrow(512),
        out_shape=jax.ShapeDtypeStruct((bsz * t_len, 512), F32),
        scratch_shapes=[
            pltpu.VMEM((qb, lp), jnp.int32),
            pltpu.VMEM((lp, LANES) if transposed else (SUBLANES, LANES), jnp.int32),
            pltpu.VMEM((lp, LANES) if transposed else (qb, lp), jnp.int16),
            pltpu.VMEM((IDX_HEADS * qb, 64), BF16),
            pltpu.VMEM((IDX_HEADS, qb, LANES), F32),
            pltpu.VMEM((A_HEADS * qb, 64), BF16),
            pltpu.VMEM((A_HEADS * mr, kt), BF16),
            pltpu.VMEM((A_HEADS * mr, 1), F32),
            pltpu.VMEM((A_HEADS, mr, LANES), F32),
            pltpu.VMEM((A_HEADS * mr, LANES), F32),
            pltpu.VMEM((1, 1), F32),
        ],
        compiler_params=_params("parallel", "arbitrary"),
        name="dsa",
    )(aq, iq, aviw, kik_all, v_all, tri)


def _retention_kernel(bqk_ref, bv_ref, bg_ref, s0_ref, gn_ref, o_ref, s_out_ref, s_ref, *, n_chunks):
    t = pl.program_id(0)

    @pl.when(t == 0)
    def _():
        s_ref[...] = s0_ref[...]

    n = CHUNK
    ri = lax.broadcasted_iota(jnp.int32, (n, n), 0).astype(F32)
    ci = lax.broadcasted_iota(jnp.int32, (n, n), 1).astype(F32)
    diff = ri - ci
    pos = lax.broadcasted_iota(jnp.int32, (n, 1), 0).astype(F32)
    decays = []
    for h in range(B_HEADS):
        log_g = math.log(1.0 - 2.0 ** (-5.0 - h))
        decays.append((jnp.where(diff >= 0, jnp.exp(jnp.maximum(diff, 0.0) * log_g), 0.0),
                       jnp.exp((pos + 1.0) * log_g), jnp.exp((n - 1.0 - pos) * log_g), math.exp(n * log_g)))
    for c in range(n_chunks):
        rows = slice(c * n, (c + 1) * n)
        for b in range(bqk_ref.shape[0]):
            for h in range(B_HEADS):
                intra, q_decay, k_decay, chunk_decay = decays[h]
                q = bqk_ref[b, rows, h * B_DK:(h + 1) * B_DK]
                k = bqk_ref[b, rows, 256 + h * B_DK:256 + (h + 1) * B_DK]
                v = bv_ref[b, rows, h * B_DV:(h + 1) * B_DV]
                s = s_ref[b, h]
                scores = _dot_f32x3(q, k, NT_DIMS) * intra
                o = _dot_f32x3(scores, v) + _dot_f32x3(q, s) * q_decay
                s_ref[b, h] = s * chunk_decay + _dot_f32x3(k * k_decay, v, TN_DIMS)
                mu = jnp.mean(o, axis=-1, keepdims=True)
                oc = o - mu
                var = jnp.mean(oc * oc, axis=-1, keepdims=True)
                g = bg_ref[b, rows, h * B_DV:(h + 1) * B_DV]
                o_ref[b, rows, h * B_DV:(h + 1) * B_DV] = (
                    oc * lax.rsqrt(var + LN_EPS) * gn_ref[:, h * B_DV:(h + 1) * B_DV] * (g * _sigmoid(g)))

    @pl.when(t == pl.num_programs(0) - 1)
    def _():
        s_out_ref[...] = s_ref[...]


def _retention(bqk, bv, bg, s0, b_gn, bsz, t_len, tt):
    row = pl.BlockSpec((bsz, tt, 512), lambda t: (0, t, 0))
    state = pl.BlockSpec((bsz, B_HEADS, B_DK, B_DV), lambda t: (0, 0, 0, 0))
    o_b, s_new = pl.pallas_call(
        functools.partial(_retention_kernel, n_chunks=tt // CHUNK),
        grid=(t_len // tt,),
        in_specs=[row, row, row, state, pl.BlockSpec((1, 512), lambda t: (0, 0))],
        out_specs=[row, state],
        out_shape=[jax.ShapeDtypeStruct((bsz, t_len, 512), F32),
                   jax.ShapeDtypeStruct((bsz, B_HEADS, B_DK, B_DV), F32)],
        scratch_shapes=[pltpu.VMEM((bsz, B_HEADS, B_DK, B_DV), F32)],
        compiler_params=_params("arbitrary"),
        name="retention",
    )(bqk.reshape(bsz, t_len, 512), bv.reshape(bsz, t_len, 512), bg.reshape(bsz, t_len, 512), s0, b_gn)
    return o_b.reshape(bsz * t_len, 512), s_new


def _out_ln_kernel(oa_ref, ob_ref, x_ref, w_ref, g_ref, b_ref, y_ref):
    y = _dot(oa_ref[...].astype(BF16), w_ref[0:512, :]) + _dot(ob_ref[...].astype(BF16), w_ref[512:1024, :])
    y_ref[...] = _layernorm_rows(ALPHA * x_ref[...] + y, g_ref[...], b_ref[...])


def _out_ln(oa, ob, x2, w_out, g, b, tm):
    n = x2.shape[0]
    row = lambda w: pl.BlockSpec((tm, w), lambda i: (i, 0))
    vec = pl.BlockSpec((1, D_MODEL), lambda i: (0, 0))
    return pl.pallas_call(
        _out_ln_kernel,
        grid=(n // tm,),
        in_specs=[row(512), row(512), row(D_MODEL), pl.BlockSpec((D_MODEL, D_MODEL), lambda i: (0, 0)), vec, vec],
        out_specs=row(D_MODEL),
        out_shape=jax.ShapeDtypeStruct((n, D_MODEL), F32),
        compiler_params=_params("parallel"),
        name="out_ln",
    )(oa, ob, x2, w_out, g, b)


def _out_ln_d_kernel(oc_ref, od_ref, r_ref, kr_ref, v_ref, gate_ref, x_ref, w_ref, hm_ref,
                     lnx_g_ref, lnx_b_ref, rk_ref, g_ref, b_ref, y_ref):
    hm = hm_ref[...]
    o = od_ref[...]
    mu = _dot_exact_rhs(o, hm) * (1.0 / HEAD_DIM)
    oc = o - mu
    var = _dot_exact_rhs(oc * oc, hm) * (1.0 / HEAD_DIM)
    normed = oc * lax.rsqrt(var + D_GN_EPS) * lnx_g_ref[...] + lnx_b_ref[...]
    v = v_ref[...]
    bonus = _dot_exact_rhs(r_ref[...] * kr_ref[...] * rk_ref[...], hm) * v
    od = (normed + bonus) * gate_ref[...]
    y = _dot(oc_ref[...].astype(BF16), w_ref[0:512, :]) + _dot(od.astype(BF16), w_ref[512:1024, :])
    y_ref[...] = _layernorm_rows(ALPHA * x_ref[...] + y, g_ref[...], b_ref[...])


def _head_mask():
    head = jnp.arange(D_W) // HEAD_DIM
    return (head[:, None] == head[None, :]).astype(BF16)


def _out_ln_d(oc, od, r, kr, v, gate, x2, w_out, lnx_g, lnx_b, r_k, g, b, tm):
    n = x2.shape[0]
    row = lambda w: pl.BlockSpec((tm, w), lambda i: (i, 0))
    vec = lambda w: pl.BlockSpec((1, w), lambda i: (0, 0))
    return pl.pallas_call(
        _out_ln_d_kernel,
        grid=(n // tm,),
        in_specs=[row(512)] * 6 + [row(D_MODEL), pl.BlockSpec((D_MODEL, D_MODEL), lambda i: (0, 0)),
                                   pl.BlockSpec((D_W, D_W), lambda i: (0, 0)),
                                   vec(D_W), vec(D_W), vec(D_W), vec(D_MODEL), vec(D_MODEL)],
        out_specs=row(D_MODEL),
        out_shape=jax.ShapeDtypeStruct((n, D_MODEL), F32),
        compiler_params=_params("parallel"),
        name="out_ln_d",
    )(oc, od, r, kr, v, gate, x2, w_out, _head_mask(), lnx_g, lnx_b, r_k, g, b)


MOE_EXPERTS_PER_STEP = 4


def _moe_kernel(x_ref, rw_ref, rb_ref, w13_ref, w2_ref, g_ref, b_ref, y_ref, acc_ref, gate_ref, xb_ref):
    e = pl.program_id(1)
    tm = x_ref.shape[0]
    lane = lax.broadcasted_iota(jnp.int32, (tm, N_EXPERTS), 1)

    lane_f = lane.astype(F32)

    def first_argmax(v):
        m = jnp.max(v, axis=1, keepdims=True)
        idx = jnp.min(jnp.where(v == m, lane_f, float(N_EXPERTS)), axis=1, keepdims=True)
        return m, idx.astype(jnp.int32)

    @pl.when(e == 0)
    def _():
        x = x_ref[...]
        xb_ref[...] = x.astype(BF16)
        aff = _sigmoid(_dot_hi(x, rw_ref[...]))
        biased = aff + rb_ref[...]
        best = jnp.zeros((tm, 1), jnp.int32)
        best_score = jnp.full((tm, 1), -jnp.inf, F32)
        for grp in range(N_GROUPS):
            vg = jnp.where(lane // EXPERTS_PER_GROUP == grp, biased, -jnp.inf)
            top1, idx1 = first_argmax(vg)
            top2, _ = first_argmax(jnp.where(lane == idx1, -jnp.inf, vg))
            score = top1 + top2
            better = score > best_score
            best = jnp.where(better, grp, best)
            best_score = jnp.where(better, score, best_score)
        masked = jnp.where(lane // EXPERTS_PER_GROUP == best, biased, -jnp.inf)
        _, idx1 = first_argmax(masked)
        _, idx2 = first_argmax(jnp.where(lane == idx1, -jnp.inf, masked))
        top_aff = jnp.where((lane == idx1) | (lane == idx2), aff, 0.0)
        gate_ref[...] = top_aff / jnp.sum(top_aff, axis=1, keepdims=True)
        acc_ref[...] = jnp.zeros(acc_ref.shape, F32)

    for k in range(MOE_EXPERTS_PER_STEP):
        h13 = _dot(xb_ref[...], w13_ref[k])
        h1 = h13[:, 0:D_EXPERT]
        h = (h1 * _sigmoid(h1)) * h13[:, D_EXPERT:2 * D_EXPERT]
        gate_e = jnp.sum(jnp.where(lane == e * MOE_EXPERTS_PER_STEP + k, gate_ref[...], 0.0), axis=1, keepdims=True)
        acc_ref[...] += gate_e * _dot(h.astype(BF16), w2_ref[k])

    @pl.when(e == pl.num_programs(1) - 1)
    def _():
        y_ref[...] = _layernorm_rows(ALPHA * x_ref[...] + acc_ref[...], g_ref[...], b_ref[...])


def _moe_ln(x2, router_w, router_b, w13, w2, g, b, tm):
    n = x2.shape[0]
    row = pl.BlockSpec((tm, D_MODEL), lambda i, e: (i, 0))
    vec = pl.BlockSpec((1, D_MODEL), lambda i, e: (0, 0))
    return pl.pallas_call(
        _moe_kernel,
        grid=(n // tm, N_EXPERTS // MOE_EXPERTS_PER_STEP),
        in_specs=[row,
                  pl.BlockSpec((D_MODEL, N_EXPERTS), lambda i, e: (0, 0)),
                  pl.BlockSpec((1, N_EXPERTS), lambda i, e: (0, 0)),
                  pl.BlockSpec((MOE_EXPERTS_PER_STEP, D_MODEL, 2 * D_EXPERT), lambda i, e: (e, 0, 0)),
                  pl.BlockSpec((MOE_EXPERTS_PER_STEP, D_EXPERT, D_MODEL), lambda i, e: (e, 0, 0)),
                  vec, vec],
        out_specs=row,
        out_shape=jax.ShapeDtypeStruct((n, D_MODEL), F32),
        scratch_shapes=[pltpu.VMEM((tm, D_MODEL), F32), pltpu.VMEM((tm, N_EXPERTS), F32),
                        pltpu.VMEM((tm, D_MODEL), BF16)],
        compiler_params=_params("parallel", "arbitrary"),
        name="moe_ln",
    )(x2, router_w, router_b, w13, w2, g, b)


def _proj_cd_kernel(x_ref, w_ref, cq_ref, ck_ref, cv_ref, pd_ref, ck16_ref, cv16_ref):
    xb = x_ref[...].astype(BF16)
    cq_ref[...] = _dot(xb, w_ref[:, 0:512])
    ck = _dot(xb, w_ref[:, 512:1024])
    ck_ref[...] = ck
    ck16_ref[...] = ck.astype(BF16)
    cv = _dot(xb, w_ref[:, 1024:1536])
    cv_ref[...] = cv
    cv16_ref[...] = cv.astype(BF16)
    pd_ref[...] = _dot(xb, w_ref[:, 1536:1536 + D_SHIFT_W])


def _proj_cd(x2, w_bf16, tm):
    n = x2.shape[0]
    row = lambda w: pl.BlockSpec((tm, w), lambda i: (i, 0))
    outs = [512, 512, 512, D_SHIFT_W]
    return pl.pallas_call(
        _proj_cd_kernel,
        grid=(n // tm,),
        in_specs=[row(D_MODEL), pl.BlockSpec((D_MODEL, 1536 + D_SHIFT_W), lambda i: (0, 0))],
        out_specs=[row(w) for w in outs] + [row(512), row(512)],
        out_shape=[jax.ShapeDtypeStruct((n, w), F32) for w in outs] + [jax.ShapeDtypeStruct((n, 512), BF16)] * 2,
        compiler_params=_params("parallel"),
        name="proj_cd",
    )(x2, w_bf16)


STICK_HEADS_PER_STEP = 2


def _stick_kernel(q_ref, k_ref, v_ref, ust_ref, o_ref, *, past, tq, kb):
    i = pl.program_id(2)
    q0 = past + i * tq
    n_kb = (q0 + tq + kb - 1) // kb
    qpos = q0 + lax.broadcasted_iota(jnp.int32, (tq, 1), 0)
    lane_kb = lax.broadcasted_iota(jnp.int32, (1, kb), 1)
    n_heads = STICK_HEADS_PER_STEP
    cols =[slice(hh * HEAD_DIM, (hh + 1) * HEAD_DIM) for hh in range(n_heads)]
    qs = [(q_ref[:, c] * (HEAD_DIM ** -0.5)).astype(BF16) for c in cols]

    def cond(c):
        j, carries, _ = c
        worst = carries[0]
        for carry in carries[1:]:
            worst = jnp.maximum(worst, carry)
        return jnp.logical_and(j >= 0, jnp.max(worst) > EXP_ZERO_BELOW)

    def body(c):
        j, carries, outs = c
        off = pl.multiple_of(j * kb, kb)
        strict = off + lane_kb < qpos
        new_carries, new_outs = [], []
        for hh in range(n_heads):
            k = k_ref[pl.ds(off, kb), cols[hh]]
            v = v_ref[pl.ds(off, kb), cols[hh]]
            z = lax.dot_general(qs[hh], k, NT_DIMS, preferred_element_type=F32)
            sp = _softplus(z)
            log_keep = jnp.where(strict, -sp, 0.0)
            later = carries[hh] + _dot_exact_rhs2(log_keep, ust_ref[...])
            a = jnp.where(strict, jnp.exp((z - sp) + later), 0.0)
            new_outs.append(outs[hh] + _dot(a.astype(BF16), v))
            new_carries.append(carries[hh] + jnp.sum(log_keep, axis=1, keepdims=True))
        return j - 1, tuple(new_carries), tuple(new_outs)

    init = (n_kb - 1, tuple(jnp.zeros((tq, 1), F32) for _ in cols), tuple(jnp.zeros((tq, HEAD_DIM), F32) for _ in cols))
    _, _, outs = lax.while_loop(cond, body, init)
    for hh in range(n_heads):
        o_ref[:, cols[hh]] = outs[hh]


def _stick(cq, k_all, v_all, bsz, t_len, past, kb):
    tq = min(kb, t_len)
    nq = t_len // tq
    lp = k_all.shape[1]
    ust = jnp.tril(jnp.ones((kb, kb), F32), -1).astype(BF16)
    width = STICK_HEADS_PER_STEP * HEAD_DIM
    qrow = pl.BlockSpec((tq, width), lambda b, hp, i: (b * nq + i, hp))
    keys = pl.BlockSpec((None, lp, width), lambda b, hp, i: (b, 0, hp))
    return pl.pallas_call(
        functools.partial(_stick_kernel, past=past, tq=tq, kb=kb),
        grid=(bsz, C_HEADS // STICK_HEADS_PER_STEP, nq),
        in_specs=[qrow, keys, keys, pl.BlockSpec((kb, kb), lambda b, hp, i: (0, 0))],
        out_specs=qrow,
        out_shape=jax.ShapeDtypeStruct((bsz * t_len, 512), F32),
        compiler_params=_params("parallel", "parallel", "arbitrary"),
        name="stick",
    )(cq, k_all, v_all, ust)


def _rwkv_pre_kernel(pd_ref, shift_ref, mu_ref, w0a0_ref, wa2_ref, gw2_ref, kk_ref_, ka_ref, hm_ref,
                     r_out, w_out, kr_out, v_out, kk_out, b_out, g_out, last_ref):
    t = pl.program_id(1)

    @pl.when(t == 0)
    def _():
        last_ref[...] = shift_ref[...]

    pd = pd_ref[...]
    tm = pd.shape[0]
    rolled = pltpu.roll(pd, 1, 0)
    first_row = lax.broadcasted_iota(jnp.int32, (tm, 1), 0) == 0
    prev = jnp.where(first_row, last_ref[...], rolled)
    last_ref[...] = pd[tm - 1:tm, :]
    pm = pd + (prev - pd) * mu_ref[...]
    r = pm[:, 0:512]
    k = pm[:, 512:1024]
    v = pm[:, 1024:1536]
    lwa = pm[:, 1536:1664]
    lg = pm[:, 1664:1792]
    lane = lax.broadcasted_iota(jnp.int32, lwa.shape, 1)
    lwa = jnp.where(lane < D_LORA_W, jnp.tanh(lwa), lwa)
    pre = w0a0_ref[...] + _dot_hi(lwa, wa2_ref[...])
    w_log = -_softplus(-pre[:, 0:512]) - 0.5
    decay = jnp.exp(-jnp.exp(w_log))
    a = _sigmoid(pre[:, 512:1024])
    g = _dot_hi(_sigmoid(lg), gw2_ref[...])
    kk = k * kk_ref_[...]
    kk = kk * lax.rsqrt(_dot_exact_rhs(kk * kk, hm_ref[...]) + 1e-12)
    r_out[...] = r
    w_out[...] = decay
    kr_out[...] = k * (1.0 + (a - 1.0) * ka_ref[...])
    v_out[...] = v
    kk_out[...] = kk
    b_out[...] = kk * a
    g_out[...] = g


def _rwkv_pre(pd, shift, mu, w0a0, wa2, gw2, k_k, k_a, bsz, t_len, tm):
    nt = t_len // tm
    row = lambda w: pl.BlockSpec((tm, w), lambda b, t: (b * nt + t, 0))
    const = lambda s: pl.BlockSpec(s, lambda b, t: (0,) * len(s))
    return pl.pallas_call(
        _rwkv_pre_kernel,
        grid=(bsz, nt),
        in_specs=[row(D_SHIFT_W), pl.BlockSpec((None, 1, D_SHIFT_W), lambda b, t: (b, 0, 0)),
                  const((1, D_SHIFT_W)), const((1, 2 * D_W)), const((LANES, 2 * D_W)), const((D_LORA_G, D_W)),
                  const((1, D_W)), const((1, D_W)), const((D_W, D_W))],
        out_specs=[row(D_W)] * 7,
        out_shape=[jax.ShapeDtypeStruct((bsz * t_len, D_W), F32)] * 7,
        scratch_shapes=[pltpu.VMEM((1, D_SHIFT_W), F32)],
        compiler_params=_params("parallel", "arbitrary"),
        name="rwkv_pre",
    )(pd, shift, mu, w0a0, wa2, gw2, k_k, k_a, _head_mask())


RWKV_KL = HEAD_DIM // 2


def _rwkv_scan_kernel(kk_ref, w_ref, b_ref, kr_ref, r_ref, v_ref, s0_ref, o_ref, s_out_ref, s_ref, *, tb):
    g = pl.program_id(0)

    @pl.when(g == 0)
    def _():
        s_ref[...] = s0_ref[...]

    def both_halves(x):
        return x + pltpu.roll(x, LANES // 2, 1)

    def step(t, carry):
        for vrows in (slice(0, HEAD_DIM // 2), slice(HEAD_DIM // 2, HEAD_DIM)):
            acc = [jnp.zeros((HEAD_DIM // 2, LANES), F32) for _ in range(2)]
            for kl in range(RWKV_KL):
                acc[kl % 2] = acc[kl % 2] + s_ref[kl, vrows, :] * kk_ref[t, kl:kl + 1, :]
            sa = both_halves(acc[0] + acc[1])
            v = v_ref[t, vrows, :]
            out = [jnp.zeros((HEAD_DIM // 2, LANES), F32) for _ in range(2)]
            for kl in range(RWKV_KL):
                row = slice(kl, kl + 1)
                s = s_ref[kl, vrows, :] * w_ref[t, row, :] - sa * b_ref[t, row, :] + v * kr_ref[t, row, :]
                s_ref[kl, vrows, :] = s
                out[kl % 2] = out[kl % 2] + s * r_ref[t, row, :]
            o_ref[t, vrows, :] = both_halves(out[0] + out[1])
        return carry

    lax.fori_loop(0, tb, step, 0)

    @pl.when(g == pl.num_programs(0) - 1)
    def _():
        s_out_ref[...] = s_ref[...]


def _rwkv_scan(kk, w, b, kr, r, v, s0, t_len, tb):
    keyed = pl.BlockSpec((tb, RWKV_KL, LANES), lambda g: (g, 0, 0))
    valued = pl.BlockSpec((tb, HEAD_DIM, LANES), lambda g: (g, 0, 0))
    state = pl.BlockSpec((RWKV_KL, HEAD_DIM, LANES), lambda g: (0, 0, 0))
    return pl.pallas_call(
        functools.partial(_rwkv_scan_kernel, tb=tb),
        grid=(t_len // tb,),
        in_specs=[keyed] * 5 + [valued, state],
        out_specs=[valued, state],
        out_shape=[jax.ShapeDtypeStruct((t_len, HEAD_DIM, LANES), F32),
                   jax.ShapeDtypeStruct((RWKV_KL, HEAD_DIM, LANES), F32)],
        scratch_shapes=[pltpu.VMEM((RWKV_KL, HEAD_DIM, LANES), F32)],
        compiler_params=_params("arbitrary"),
        name="rwkv_scan",
    )(kk, w, b, kr, r, v, s0)


def _to_scan_keyed(x2, bsz, t_len):
    y = x2.reshape(bsz, t_len, D_HEADS, 2, RWKV_KL).transpose(1, 4, 3, 0, 2)
    return y.reshape(t_len, RWKV_KL, 2 * bsz * D_HEADS)


def _to_scan_valued(x2, bsz, t_len):
    y = x2.reshape(bsz, t_len, D_HEADS, HEAD_DIM).transpose(1, 3, 0, 2).reshape(t_len, HEAD_DIM, bsz * D_HEADS)
    return jnp.concatenate([y, y], axis=-1)


def _from_scan_valued(y, bsz, t_len):
    y = y[:, :, 0:bsz * D_HEADS].reshape(t_len, HEAD_DIM, bsz, D_HEADS)
    return y.transpose(2, 0, 3, 1).reshape(bsz * t_len, D_W)


def _state_to_scan(s, bsz):
    y = s.reshape(bsz, D_HEADS, HEAD_DIM, 2, RWKV_KL).transpose(4, 2, 3, 0, 1)
    return y.reshape(RWKV_KL, HEAD_DIM, 2 * bsz * D_HEADS)


def _state_from_scan(y, bsz):
    return y.reshape(RWKV_KL, HEAD_DIM, 2, bsz, D_HEADS).transpose(3, 4, 1, 2, 0).reshape(
        bsz, D_HEADS, HEAD_DIM, HEAD_DIM)


def _pad_keys(x, mult):
    pad = (-x.shape[1]) % mult
    return x if pad == 0 else jnp.pad(x, ((0, 0), (0, pad), (0, 0)))


def _tile(n, cap):
    return min(n, cap)


def _run_group(x, past, p):
    bsz, t_len, _ = x.shape
    assert bsz * D_HEADS * 2 == LANES, "rwkv scan packs (key half, batch, head) into the lane axis"
    n = bsz * t_len
    x2 = x.reshape(n, D_MODEL)
    past_len = 0 if past is None else past[0].shape[2]
    tm = _tile(t_len, 512)

    cos, sin = _rope_tables(past_len, t_len)
    aq, iq, bqk, kik, aviw, bv, bg, kik16, v16 = _proj_ab(x2, p['w_in_ab'], cos, sin, t_len, tm)
    ak = kik[:, 0:64].reshape(1, bsz, t_len, 64)
    ik = kik[:, 64:128].reshape(1, bsz, t_len, 64)
    av = aviw[:, 0:64].reshape(1, bsz, t_len, 64)
    kik_all = kik16.reshape(bsz, t_len, LANES)
    v_all = v16.reshape(bsz, t_len, LANES)
    ones_col = (jnp.arange(LANES) == 64).astype(F32)
    if past is None:
        s_b = jnp.zeros((bsz, B_HEADS, B_DK, B_DV), F32)
    else:
        pk, pv, pik, sb = past[0][0], past[1][0], past[2][0], past[3][0]
        kik_all = jnp.concatenate([jnp.concatenate([pk, pik], axis=-1).astype(BF16), kik_all], axis=1)
        pv_slab = jnp.concatenate([pv, jnp.broadcast_to(ones_col[64:], pv.shape)], axis=-1)
        v_all = jnp.concatenate([pv_slab.astype(BF16), v_all], axis=1)
        s_b = sb
    kt = 1024
    o_a = _dsa(aq, iq, aviw, _pad_keys(kik_all, kt), _pad_keys(v_all, kt), bsz, t_len, past_len, kt)
    o_b, s_b_new = _retention(bqk, bv, bg, s_b, p['b_gn'], bsz, t_len, _tile(t_len, 2 * CHUNK))
    x2 = _out_ln(o_a, o_b, x2, p['w_out_ab'], p['ln_g'][0, 0][None], p['ln_b'][0, 0][None], tm)
    tm_moe = _tile(n, 1024)
    x2 = _moe_ln(x2, p['router_w'], p['router_b'], p['w13'][0], p['w2'][0],
                 p['ln_g'][0, 1][None], p['ln_b'][0, 1][None], tm_moe)

    cq, ck, cv, pd, ck16, cv16 = _proj_cd(x2, p['w_in_cd'], tm)
    ck_all = ck16.reshape(bsz, t_len, 512)
    cv_all = cv16.reshape(bsz, t_len, 512)
    if past is None:
        s_d = jnp.zeros((bsz, D_HEADS, HEAD_DIM, HEAD_DIM), F32)
        shift = jnp.zeros((bsz, 1, D_SHIFT_W), F32)
    else:
        ck_all = jnp.concatenate([past[4][0].reshape(bsz, past_len, 512).astype(BF16), ck_all], axis=1)
        cv_all = jnp.concatenate([past[5][0].reshape(bsz, past_len, 512).astype(BF16), cv_all], axis=1)
        s_d, shift = past[6][0], past[7][0]
    kb = 256
    o_c = _stick(cq, _pad_keys(ck_all, kb), _pad_keys(cv_all, kb), bsz, t_len, past_len, kb)
    r, w, kr, v, kk, b, g = _rwkv_pre(pd, shift, p['d_mu'], p['d_w0a0'], p['d_wa2'], p['d_g2'],
                                      p['d_k_k'], p['d_k_a'], bsz, t_len, tm)
    keyed = [_to_scan_keyed(u, bsz, t_len) for u in (kk, w, b, kr, r)]
    o_scan, s_scan = _rwkv_scan(*keyed, _to_scan_valued(v, bsz, t_len), _state_to_scan(s_d, bsz),
                                t_len, _tile(t_len, 64))
    o_d = _from_scan_valued(o_scan, bsz, t_len)
    s_d_new = _state_from_scan(s_scan, bsz)
    x2 = _out_ln_d(o_c, o_d, r, kr, v, g, x2, p['w_out_cd'], p['d_lnx_g'], p['d_lnx_b'], p['d_r_k'],
                   p['ln_g'][1, 0][None], p['ln_b'][1, 0][None], tm)
    x2 = _moe_ln(x2, p['router_w'], p['router_b'], p['w13'][1], p['w2'][1],
                 p['ln_g'][1, 1][None], p['ln_b'][1, 1][None], tm_moe)

    states = (ak, av, ik, s_b_new[None],
              ck.reshape(1, bsz, t_len, C_HEADS, HEAD_DIM), cv.reshape(1, bsz, t_len, C_HEADS, HEAD_DIM),
              s_d_new[None], pd.reshape(bsz, t_len, D_SHIFT_W)[:, -1:][None])
    return x2.reshape(bsz, t_len, D_MODEL), states


def kernel(x_prompt, x_sample, cache_a_k, cache_a_v, cache_a_idx_k, state_b, cache_c_k, cache_c_v, state_d, state_d_shift, w_in_ab, w_out_ab, b_gn, w_in_cd, w_out_cd, d_mu, d_w0, d_w2, d_a0, d_a2, d_g2, d_k_k, d_k_a, d_r_k, d_lnx_g, d_lnx_b, ln_g, ln_b, router_w, router_b, moe_w1, moe_w3, moe_w2):
    zeros_w = jnp.zeros((D_LORA_W, D_W), F32)
    p = {
        'w_in_ab': _pack_w_in_ab(w_in_ab[0]),
        'w_out_ab': w_out_ab[0].astype(BF16),
        'b_gn': b_gn,
        'w_in_cd': w_in_cd[0].astype(BF16),
        'w_out_cd': w_out_cd[0].astype(BF16),
        'd_mu': d_mu,
        'd_w0a0': jnp.concatenate([d_w0, d_a0], axis=1),
        'd_wa2': jnp.concatenate([jnp.concatenate([d_w2[0], zeros_w], axis=1),
                                  jnp.concatenate([zeros_w, d_a2[0]], axis=1)], axis=0),
        'd_g2': d_g2[0],
        'd_k_k': d_k_k, 'd_k_a': d_k_a, 'd_r_k': d_r_k, 'd_lnx_g': d_lnx_g, 'd_lnx_b': d_lnx_b,
        'ln_g': ln_g, 'ln_b': ln_b,
        'router_w': router_w, 'router_b': router_b[None],
        'w13': jnp.concatenate([moe_w1, moe_w3], axis=-1).astype(BF16),
        'w2': moe_w2.astype(BF16),
    }
    y_p, sp = _run_group(x_prompt, None, p)
    past = (cache_a_k, cache_a_v, cache_a_idx_k, state_b, cache_c_k, cache_c_v, state_d, state_d_shift)
    y_s, ss = _run_group(x_sample, past, p)
    return (y_p, y_s, sp[0], sp[1], sp[2], ss[0], ss[1], ss[2], sp[3], ss[3], sp[4], sp[5], ss[4], ss[5],
            sp[6], ss[6], sp[7], ss[7])
```

```python
import functools
import math

import jax
import jax.numpy as jnp
import numpy as np
from jax import lax
from jax.experimental import pallas as pl
from jax.experimental.pallas import tpu as pltpu

F32 = jnp.float32
BF16 = jnp.bfloat16

D_MODEL = 1024
CHUNK = 64
DSA_QUERY_ROWS = 128
DSA_MATMUL_ROWS = 128
ROPE_THETA = 10000.0
HEAD_DIM = 64
LN_EPS = 1e-5
A_HEADS = 8
IDX_HEADS = 8
TOPK_MAX = 256
B_HEADS = 4
B_DK = 64
B_DV = 128
C_HEADS = 8
D_HEADS = 8
D_LORA_W = 64
D_LORA_A = 64
D_LORA_G = 128
D_GN_EPS = 64e-5
N_EXPERTS = 16
N_GROUPS = 4
EXPERTS_PER_GROUP = 4
D_EXPERT = 256
DEPTH = 2
ALPHA = (2 * DEPTH) ** 0.25
D_W = D_HEADS * HEAD_DIM
D_SHIFT_W = 3 * D_W + D_LORA_W + D_LORA_A + D_LORA_G

LANES = 128
SUBLANES = 8
VMEM_LIMIT_BYTES = 56 * 1024 * 1024

SHIFT_MARGIN = 1.001
MIN_SOFTMAX_SUM = 1e-30
EXP_ZERO_BELOW = -104.0

NT_DIMS = (((1,), (1,)), ((), ()))
TN_DIMS = (((0,), (0,)), ((), ()))


def _params(*sem):
    return pltpu.CompilerParams(dimension_semantics=sem, vmem_limit_bytes=VMEM_LIMIT_BYTES)


def _dot(a, b):
    return jnp.dot(a, b, preferred_element_type=F32)


def _dot_hi(a, b):
    return jnp.dot(a, b, preferred_element_type=F32, precision=lax.Precision.HIGHEST)


def _dot_f32x3(a, b, dims=(((1,), (0,)), ((), ()))):
    a_hi = a.astype(BF16)
    b_hi = b.astype(BF16)
    a_lo = (a - a_hi.astype(F32)).astype(BF16)
    b_lo = (b - b_hi.astype(F32)).astype(BF16)
    dot = lambda x, y: lax.dot_general(x, y, dims, preferred_element_type=F32)
    return dot(a_hi, b_hi) + dot(a_hi, b_lo) + dot(a_lo, b_hi)


def _split3(x):
    h1 = x.astype(BF16)
    r1 = x - h1.astype(F32)
    h2 = r1.astype(BF16)
    r2 = r1 - h2.astype(F32)
    return h1, h2, r2.astype(BF16)


def _dot_exact_rhs(x, m01):
    h1, h2, h3 = _split3(x)
    return _dot(h1, m01) + _dot(h2, m01) + _dot(h3, m01)


def _dot_exact_rhs2(x, m01):
    h1 = x.astype(BF16)
    h2 = (x - h1.astype(F32)).astype(BF16)
    return _dot(h1, m01) + _dot(h2, m01)


def _layernorm_rows(x, g, b):
    mu = jnp.mean(x, axis=-1, keepdims=True)
    xc = x - mu
    var = jnp.mean(xc * xc, axis=-1, keepdims=True)
    return xc * lax.rsqrt(var + LN_EPS) * g + b


def _sigmoid(x):
    return 1.0 / (1.0 + jnp.exp(-x))


def _softplus(x):
    return jnp.maximum(x, 0.0) + jnp.log(1.0 + jnp.exp(-jnp.abs(x)))


def _rope_slab(x, cos, sin_signed):
    lane = lax.broadcasted_iota(jnp.int32, x.shape, 1)
    first_half = (lane % HEAD_DIM) < (HEAD_DIM // 2)
    swapped = jnp.where(first_half, pltpu.roll(x, LANES - HEAD_DIM // 2, 1), pltpu.roll(x, HEAD_DIM // 2, 1))
    return x * cos + swapped * sin_signed


AB_ROPED = 1664
AB_PACKED = 2816


def _pack_w_in_ab(w):
    aq, ak, av, iq, ik, iw, bq, bk, bv, bg = jnp.split(
        w, [512, 576, 640, 1152, 1216, 1224, 1480, 1736, 2248], axis=1)
    pad = jnp.zeros((w.shape[0], LANES - 64 - IDX_HEADS), w.dtype)
    return jnp.concatenate([aq, iq, bq, bk, ak, ik, av, iw, pad, bv, bg], axis=1).astype(BF16)


def _proj_ab_kernel(x_ref, w_ref, cos_ref, sin_ref, aq_ref, iq_ref, bqk_ref, kik_ref, aviw_ref, bv_ref, bg_ref,
                    kik16_ref, v16_ref):
    xb = x_ref[...].astype(BF16)
    cos = cos_ref[...]
    sin = sin_ref[...]

    def roped(col0, width, scale_from=None):
        y = _dot(xb, w_ref[:, col0:col0 + width])
        parts = []
        for c in range(width // LANES):
            slab = _rope_slab(y[:, c * LANES:(c + 1) * LANES], cos, sin)
            if scale_from is not None and c * LANES >= scale_from:
                slab = slab * (B_DK ** -0.5)
            parts.append(slab)
        return parts

    for c, slab in enumerate(roped(0, 512)):
        aq_ref[:, c * LANES:(c + 1) * LANES] = slab
    for c, slab in enumerate(roped(512, 512)):
        iq_ref[:, c * LANES:(c + 1) * LANES] = slab
    for c, slab in enumerate(roped(1024, 512, scale_from=256)):
        bqk_ref[:, c * LANES:(c + 1) * LANES] = slab
    kik = roped(1536, LANES)[0]
    kik_ref[...] = kik
    kik16_ref[...] = kik.astype(BF16)
    aviw = _dot(xb, w_ref[:, AB_ROPED:AB_ROPED + LANES])
    lane = lax.broadcasted_iota(jnp.int32, aviw.shape, 1)
    is_iw = (lane >= 64) & (lane < 64 + IDX_HEADS)
    aviw_ref[...] = jnp.where(is_iw, aviw * ((IDX_HEADS * HEAD_DIM) ** -0.5), aviw)
    v16_ref[...] = jnp.where(lane < 64, aviw, jnp.where(lane == 64, 1.0, 0.0)).astype(BF16)
    bv_ref[...] = _dot(xb, w_ref[:, 1792:2304])
    bg_ref[...] = _dot(xb, w_ref[:, 2304:2816])


def _proj_ab(x2, w_packed, cos, sin, t_len, tm):
    n = x2.shape[0]
    nt = t_len // tm
    row = lambda w: pl.BlockSpec((tm, w), lambda i: (i, 0))
    tab = pl.BlockSpec((tm, LANES), lambda i: (i % nt, 0))
    outs = [512, 512, 512, LANES, LANES, 512, 512]
    return pl.pallas_call(
        _proj_ab_kernel,
        grid=(n // tm,),
        in_specs=[row(D_MODEL), pl.BlockSpec((D_MODEL, AB_PACKED), lambda i: (0, 0)), tab, tab],
        out_specs=[row(w) for w in outs] + [row(LANES), row(LANES)],
        out_shape=[jax.ShapeDtypeStruct((n, w), F32) for w in outs] + [jax.ShapeDtypeStruct((n, LANES), BF16)] * 2,
        compiler_params=_params("parallel"),
        name="proj_ab",
    )(x2, w_packed, cos, sin)


def _rope_tables(past, t_len):
    half = HEAD_DIM // 2
    inv = ROPE_THETA ** (-jnp.arange(half, dtype=F32) / half)
    ang = (past + jnp.arange(t_len)).astype(F32)[:, None] * inv[None, :]
    c, s = jnp.cos(ang), jnp.sin(ang)
    return jnp.concatenate([c, c, c, c], axis=1), jnp.concatenate([-s, s, -s, s], axis=1)


def _dsa_kernel(aq_ref, iq_ref, aviw_ref, kik_ref, v_ref, tri_ref, o_ref,
                skey_ref, skey_t_ref, half_ref, iqs_ref, iwb_ref, qs_ref, p_ref, m_ref, mlane_ref, acc_ref, kmax_ref,
                *, past, qb, mr, kt, topk, transposed):
    i = pl.program_id(1)
    q0 = past + i * qb
    n_tiles = (q0 + qb + kt - 1) // kt
    row = lax.broadcasted_iota(jnp.int32, (qb, 1), 0)
    vis_end = ((q0 + row) // CHUNK + 1) * CHUNK
    lane_kt = lax.broadcasted_iota(jnp.int32, (1, kt), 1)

    @pl.when(i == 0)
    def _():
        def body(j, m):
            k = kik_ref[pl.ds(pl.multiple_of(j * kt, kt), kt), 0:64].astype(F32)
            return jnp.maximum(m, jnp.sum(k * k, axis=1, keepdims=True))
        norms = lax.fori_loop(0, kik_ref.shape[0] // kt, body, jnp.zeros((kt, 1), F32))
        kmax_ref[...] = jnp.max(norms, axis=0, keepdims=True)

    groups = qb // mr
    stack = lambda g, h: slice((g * A_HEADS + h) * mr, (g * A_HEADS + h + 1) * mr)
    for g in range(groups):
        for h in range(IDX_HEADS):
            iqs_ref[stack(g, h), :] = iq_ref[g * mr:(g + 1) * mr, h * 64:(h + 1) * 64].astype(BF16)
    for h in range(IDX_HEADS):
        iwb_ref[h] = jnp.broadcast_to(aviw_ref[:, 64 + h:65 + h], (qb, LANES))

    def score_tile(j, carry):
        off = pl.multiple_of(j * kt, kt)
        ik = kik_ref[pl.ds(off, kt), 64:128]
        for g in range(groups):
            grows = slice(g * mr, (g + 1) * mr)
            s = lax.dot_general(iqs_ref[g * IDX_HEADS * mr:(g + 1) * IDX_HEADS * mr, :], ik, NT_DIMS,
                                preferred_element_type=F32)
            parts = []
            for c in range(kt // LANES):
                a = jnp.zeros((mr, LANES), F32)
                for h in range(IDX_HEADS):
                    a = a + iwb_ref[h, grows, :] * jnp.maximum(s[h * mr:(h + 1) * mr, c * LANES:(c + 1) * LANES], 0.0)
                parts.append(a)
            acc = jnp.concatenate(parts, axis=1)
            acc = acc + 0.0
            acc = jnp.where(off + lane_kt < vis_end[grows, :], acc, -jnp.inf)
            bits = pltpu.bitcast(acc, jnp.int32)
            key = jnp.where(bits < 0, bits ^ 0x7FFFFFFF, bits)
            skey_ref[grows, pl.ds(off, kt)] = key
            if transposed:
                for c in range(kt // LANES):
                    chunk = pltpu.bitcast(key[:, c * LANES:(c + 1) * LANES], F32).T
                    key_t = pltpu.bitcast(chunk, jnp.int32)
                    rows = pl.ds(off + c * LANES, LANES)
                    skey_t_ref[rows, :] = key_t
                    half_ref[rows, :] = jnp.right_shift(key_t, 16).astype(jnp.int16)
            else:
                half_ref[grows, pl.ds(off, kt)] = jnp.right_shift(key, 16).astype(jnp.int16)
        return carry

    lax.fori_loop(0, n_tiles, score_tile, 0)

    def count(pred_fn):
        def body(j, cnt):
            off = pl.multiple_of(j * kt, kt)
            hit = pred_fn(skey_ref[:, pl.ds(off, kt)])
            for c in range(kt // LANES):
                cnt = cnt + jnp.where(hit[:, c * LANES:(c + 1) * LANES], 1.0, 0.0)
            return cnt
        cnt = lax.fori_loop(0, n_tiles, body, jnp.zeros((qb, LANES), F32))
        return jnp.sum(cnt, axis=1, keepdims=True)

    vec = (1, LANES) if transposed else (qb, 1)
    pack = 2 * SUBLANES

    def count_half_ge(cand):
        if transposed:
            cand16 = jnp.broadcast_to(cand, (pack, LANES)).astype(jnp.int16)
            one = jnp.ones((pack, LANES), jnp.int16)
            nil = jnp.zeros((pack, LANES), jnp.int16)

            def body(j, cnts):
                tile = half_ref[pl.ds(pl.multiple_of(j * kt, kt), kt), :]
                cnts = list(cnts)
                for c in range(kt // pack):
                    cnts[c % 4] = cnts[c % 4] + jnp.where(tile[c * pack:(c + 1) * pack, :] >= cand16, one, nil)
                return tuple(cnts)
            cnts = lax.fori_loop(0, n_tiles, body, (nil, nil, nil, nil))
            cnt = (cnts[0] + cnts[1]) + (cnts[2] + cnts[3])
            return jnp.sum(cnt.astype(F32), axis=0, keepdims=True)
        cand16 = jnp.broadcast_to(cand, (qb, LANES)).astype(jnp.int16)
        one = jnp.ones((qb, LANES), jnp.int16)
        nil = jnp.zeros((qb, LANES), jnp.int16)

        def body(j, cnt):
            off = pl.multiple_of(j * kt, kt)
            tile = half_ref[:, pl.ds(off, kt)]
            for c in range(kt // LANES):
                cnt = cnt + jnp.where(tile[:, c * LANES:(c + 1) * LANES] >= cand16, one, nil)
            return cnt
        cnt = lax.fori_loop(0, n_tiles, body, nil)
        return jnp.sum(cnt.astype(F32), axis=1, keepdims=True)

    def bisect16(cnt_min, extra):
        zero = jnp.zeros(vec, jnp.int32)
        cnt = extra + count_half_ge(zero)
        ok = cnt >= topk
        start = (jnp.where(ok, zero, jnp.full(vec, -(2 ** 15), jnp.int32)), jnp.where(ok, cnt, cnt_min))

        def bit_step(it, c):
            t, cnt_t = c
            cand = t | jnp.left_shift(jnp.int32(1), 14 - it)
            cnt = extra + count_half_ge(cand)
            ok = cnt >= topk
            return jnp.where(ok, cand, t), jnp.where(ok, cnt, cnt_t)
        return lax.fori_loop(0, 15, bit_step, start)

    visited = jnp.full(vec, n_tiles * kt, jnp.int32).astype(F32)
    t_hi, cnt_hi = bisect16(visited, 0.0)
    top16 = 2 ** 15 - 1
    above = jnp.where(t_hi == top16, 0.0, count_half_ge(jnp.minimum(t_hi + 1, top16)))

    def low_tile(j, carry):
        off = pl.multiple_of(j * kt, kt)
        at = (pl.ds(off, kt), slice(None)) if transposed else (slice(None), pl.ds(off, kt))
        key = skey_t_ref[at] if transposed else skey_ref[at]
        low = (key & 0xFFFF) - 2 ** 15
        half_ref[at] = jnp.where(jnp.right_shift(key, 16) == t_hi, low, -(2 ** 15)).astype(jnp.int16)
        return carry

    lax.fori_loop(0, n_tiles, low_tile, 0)
    t_lo, cnt_ge = bisect16(cnt_hi, above)
    thr = jnp.left_shift(t_hi, 16) | (t_lo + 2 ** 15)
    if transposed:
        square = pltpu.bitcast(jnp.broadcast_to(thr, (LANES, LANES)), F32).T
        thr = pltpu.bitcast(square, jnp.int32)[:, 0:1]
    lane_ok = lambda off: off + lane_kt < vis_end

    def select_all_ties(j, carry):
        off = pl.multiple_of(j * kt, kt)
        sel = (skey_ref[:, pl.ds(off, kt)] >= thr) & lane_ok(off)
        skey_ref[:, pl.ds(off, kt)] = pltpu.bitcast(jnp.where(sel, 0.0, -jnp.inf), jnp.int32)
        return carry

    def select_ranked_ties(need):
        def body(j, eq_seen):
            off = pl.multiple_of(j * kt, kt)
            key = skey_ref[:, pl.ds(off, kt)]
            eq = key == thr
            rank = _dot(jnp.where(eq, 1.0, 0.0).astype(BF16), tri_ref[...]) + eq_seen
            sel = ((key > thr) | (eq & (rank <= need))) & lane_ok(off)
            skey_ref[:, pl.ds(off, kt)] = pltpu.bitcast(jnp.where(sel, 0.0, -jnp.inf), jnp.int32)
            return rank[:, kt - 1:kt]
        return body

    def exact_fit():
        lax.fori_loop(0, n_tiles, select_all_ties, 0)

    def surplus_ties():
        need = topk - count(lambda x: x > thr)
        lax.fori_loop(0, n_tiles, select_ranked_ties(need), jnp.zeros((qb, 1), F32))

    lax.cond(jnp.max(jnp.abs(cnt_ge - topk)) == 0.0, exact_fit, surplus_ties)

    for g in range(groups):
        for h in range(A_HEADS):
            qs_ref[stack(g, h), :] = (aq_ref[g * mr:(g + 1) * mr, h * 64:(h + 1) * 64] * (HEAD_DIM ** -0.5)).astype(BF16)

    for g in range(groups):
        grows = slice(g * mr, (g + 1) * mr)
        qs_rows = slice(g * A_HEADS * mr, (g + 1) * A_HEADS * mr)
        g_tiles = (q0 + (g + 1) * mr + kt - 1) // kt

        def max_tile(j, carry):
            off = pl.multiple_of(j * kt, kt)
            bias = pltpu.bitcast(skey_ref[grows, pl.ds(off, kt)], F32)
            k = kik_ref[pl.ds(off, kt), 0:64]
            s = lax.dot_general(qs_ref[qs_rows, :], k, NT_DIMS, preferred_element_type=F32)
            for h in range(A_HEADS):
                sh = s[h * mr:(h + 1) * mr, :] + bias
                mm = sh[:, 0:LANES]
                for c in range(1, kt // LANES):
                    mm = jnp.maximum(mm, sh[:, c * LANES:(c + 1) * LANES])
                mlane_ref[h] = jnp.maximum(mlane_ref[h], mm)
            return carry

        def pv_tile(j, carry):
            off = pl.multiple_of(j * kt, kt)
            bias = pltpu.bitcast(skey_ref[grows, pl.ds(off, kt)], F32)
            k = kik_ref[pl.ds(off, kt), 0:64]
            s = lax.dot_general(qs_ref[qs_rows, :], k, NT_DIMS, preferred_element_type=F32)
            for h in range(A_HEADS):
                rows = slice(h * mr, (h + 1) * mr)
                p_ref[rows, :] = jnp.exp((s[rows, :] + bias) - m_ref[rows, :]).astype(BF16)
            acc_ref[...] += _dot(p_ref[...], v_ref[pl.ds(off, kt), :])
            return carry

        q32 = qs_ref[qs_rows, :].astype(F32)
        m_ref[...] = jnp.sqrt(jnp.sum(q32 * q32, axis=1, keepdims=True) * kmax_ref[...]) * SHIFT_MARGIN
        acc_ref[...] = jnp.zeros(acc_ref.shape, F32)
        lax.fori_loop(0, g_tiles, pv_tile, 0)

        def exact_shift():
            mlane_ref[...] = jnp.full(mlane_ref.shape, -jnp.inf, F32)
            lax.fori_loop(0, g_tiles, max_tile, 0)
            for h in range(A_HEADS):
                m_ref[h * mr:(h + 1) * mr, :] = jnp.max(mlane_ref[h], axis=1, keepdims=True)
            acc_ref[...] = jnp.zeros(acc_ref.shape, F32)
            lax.fori_loop(0, g_tiles, pv_tile, 0)

        lax.cond(jnp.min(acc_ref[:, 64:65]) >= MIN_SOFTMAX_SUM, lambda: None, exact_shift)
        for h in range(A_HEADS):
            a = acc_ref[h * mr:(h + 1) * mr, :]
            o_ref[grows, h * 64:(h + 1) * 64] = (a / a[:, 64:65])[:, 0:64]


def _dsa(aq, iq, aviw, kik_all, v_all, bsz, t_len, past, kt):
    qb = min(DSA_QUERY_ROWS, t_len)
    mr = min(DSA_MATMUL_ROWS, qb)
    transposed = qb == LANES and mr == qb
    nq = t_len // qb
    lp = kik_all.shape[1]
    topk = min(TOPK_MAX, (past + t_len) // 4)
    tri = jnp.triu(jnp.ones((kt, kt), F32)).astype(BF16)
    qrow = lambda w: pl.BlockSpec((qb, w), lambda b, i: (b * nq + i, 0))
    keys = pl.BlockSpec((None, lp, LANES), lambda b, i: (b, 0, 0))
    return pl.pallas_call(
        functools.partial(_dsa_kernel, past=past, qb=qb, mr=mr, kt=kt, topk=topk, transposed=transposed),
        grid=(bsz, nq),
        in_specs=[qrow(512), qrow(512), qrow(LANES), keys, keys, pl.BlockSpec((kt, kt), lambda b, i: (0, 0))],
        out_specs=qrow(512),
        out_shape=jax.ShapeDtypeStruct((bsz * t_len, 512), F32),
        scratch_shapes=[
            pltpu.VMEM((qb, lp), jnp.int32),
            pltpu.VMEM((lp, LANES) if transposed else (SUBLANES, LANES), jnp.int32),
            pltpu.VMEM((lp, LANES) if transposed else (qb, lp), jnp.int16),
            pltpu.VMEM((IDX_HEADS * qb, 64), BF16),
            pltpu.VMEM((IDX_HEADS, qb, LANES), F32),
            pltpu.VMEM((A_HEADS * qb, 64), BF16),
            pltpu.VMEM((A_HEADS * mr, kt), BF16),
            pltpu.VMEM((A_HEADS * mr, 1), F32),
            pltpu.VMEM((A_HEADS, mr, LANES), F32),
            pltpu.VMEM((A_HEADS * mr, LANES), F32),
            pltpu.VMEM((1, 1), F32),
        ],
        compiler_params=_params("parallel", "arbitrary"),
        name="dsa",
    )(aq, iq, aviw, kik_all, v_all, tri)


def _retention_kernel(bqk_ref, bv_ref, bg_ref, s0_ref, gn_ref, o_ref, s_out_ref, s_ref, *, n_chunks):
    t = pl.program_id(0)

    @pl.when(t == 0)
    def _():
        s_ref[...] = s0_ref[...]

    n = CHUNK
    ri = lax.broadcasted_iota(jnp.int32, (n, n), 0).astype(F32)
    ci = lax.broadcasted_iota(jnp.int32, (n, n), 1).astype(F32)
    diff = ri - ci
    pos = lax.broadcasted_iota(jnp.int32, (n, 1), 0).astype(F32)
    decays = []
    for h in range(B_HEADS):
        log_g = math.log(1.0 - 2.0 ** (-5.0 - h))
        decays.append((jnp.where(diff >= 0, jnp.exp(jnp.maximum(diff, 0.0) * log_g), 0.0),
                       jnp.exp((pos + 1.0) * log_g), jnp.exp((n - 1.0 - pos) * log_g), math.exp(n * log_g)))
    for c in range(n_chunks):
        rows = slice(c * n, (c + 1) * n)
        for b in range(bqk_ref.shape[0]):
            for h in range(B_HEADS):
                intra, q_decay, k_decay, chunk_decay = decays[h]
                q = bqk_ref[b, rows, h * B_DK:(h + 1) * B_DK]
                k = bqk_ref[b, rows, 256 + h * B_DK:256 + (h + 1) * B_DK]
                v = bv_ref[b, rows, h * B_DV:(h + 1) * B_DV]
                s = s_ref[b, h]
                scores = _dot_f32x3(q, k, NT_DIMS) * intra
                o = _dot_f32x3(scores, v) + _dot_f32x3(q, s) * q_decay
                s_ref[b, h] = s * chunk_decay + _dot_f32x3(k * k_decay, v, TN_DIMS)
                mu = jnp.mean(o, axis=-1, keepdims=True)
                oc = o - mu
                var = jnp.mean(oc * oc, axis=-1, keepdims=True)
                g = bg_ref[b, rows, h * B_DV:(h + 1) * B_DV]
                o_ref[b, rows, h * B_DV:(h + 1) * B_DV] = (
                    oc * lax.rsqrt(var + LN_EPS) * gn_ref[:, h * B_DV:(h + 1) * B_DV] * (g * _sigmoid(g)))

    @pl.when(t == pl.num_programs(0) - 1)
    def _():
        s_out_ref[...] = s_ref[...]


def _retention(bqk, bv, bg, s0, b_gn, bsz, t_len, tt):
    row = pl.BlockSpec((bsz, tt, 512), lambda t: (0, t, 0))
    state = pl.BlockSpec((bsz, B_HEADS, B_DK, B_DV), lambda t: (0, 0, 0, 0))
    o_b, s_new = pl.pallas_call(
        functools.partial(_retention_kernel, n_chunks=tt // CHUNK),
        grid=(t_len // tt,),
        in_specs=[row, row, row, state, pl.BlockSpec((1, 512), lambda t: (0, 0))],
        out_specs=[row, state],
        out_shape=[jax.ShapeDtypeStruct((bsz, t_len, 512), F32),
                   jax.ShapeDtypeStruct((bsz, B_HEADS, B_DK, B_DV), F32)],
        scratch_shapes=[pltpu.VMEM((bsz, B_HEADS, B_DK, B_DV), F32)],
        compiler_params=_params("arbitrary"),
        name="retention",
    )(bqk.reshape(bsz, t_len, 512), bv.reshape(bsz, t_len, 512), bg.reshape(bsz, t_len, 512), s0, b_gn)
    return o_b.reshape(bsz * t_len, 512), s_new


def _out_ln_kernel(oa_ref, ob_ref, x_ref, w_ref, g_ref, b_ref, y_ref):
    y = _dot(oa_ref[...].astype(BF16), w_ref[0:512, :]) + _dot(ob_ref[...].astype(BF16), w_ref[512:1024, :])
    y_ref[...] = _layernorm_rows(ALPHA * x_ref[...] + y, g_ref[...], b_ref[...])


def _out_ln(oa, ob, x2, w_out, g, b, tm):
    n = x2.shape[0]
    row = lambda w: pl.BlockSpec((tm, w), lambda i: (i, 0))
    vec = pl.BlockSpec((1, D_MODEL), lambda i: (0, 0))
    return pl.pallas_call(
        _out_ln_kernel,
        grid=(n // tm,),
        in_specs=[row(512), row(512), row(D_MODEL), pl.BlockSpec((D_MODEL, D_MODEL), lambda i: (0, 0)), vec, vec],
        out_specs=row(D_MODEL),
        out_shape=jax.ShapeDtypeStruct((n, D_MODEL), F32),
        compiler_params=_params("parallel"),
        name="out_ln",
    )(oa, ob, x2, w_out, g, b)


def _out_ln_d_kernel(oc_ref, od_ref, r_ref, kr_ref, v_ref, gate_ref, x_ref, w_ref, hm_ref,
                     lnx_g_ref, lnx_b_ref, rk_ref, g_ref, b_ref, y_ref):
    hm = hm_ref[...]
    o = od_ref[...]
    mu = _dot_exact_rhs(o, hm) * (1.0 / HEAD_DIM)
    oc = o - mu
    var = _dot_exact_rhs2(oc * oc, hm) * (1.0 / HEAD_DIM)
    normed = oc * lax.rsqrt(var + D_GN_EPS) * lnx_g_ref[...] + lnx_b_ref[...]
    v = v_ref[...]
    bonus = _dot_exact_rhs2(r_ref[...] * kr_ref[...] * rk_ref[...], hm) * v
    od = (normed + bonus) * gate_ref[...]
    y = _dot(oc_ref[...].astype(BF16), w_ref[0:512, :]) + _dot(od.astype(BF16), w_ref[512:1024, :])
    y_ref[...] = _layernorm_rows(ALPHA * x_ref[...] + y, g_ref[...], b_ref[...])


def _head_mask():
    head = jnp.arange(D_W) // HEAD_DIM
    return (head[:, None] == head[None, :]).astype(BF16)


def _out_ln_d(oc, od, r, kr, v, gate, x2, w_out, lnx_g, lnx_b, r_k, g, b, tm):
    n = x2.shape[0]
    row = lambda w: pl.BlockSpec((tm, w), lambda i: (i, 0))
    vec = lambda w: pl.BlockSpec((1, w), lambda i: (0, 0))
    return pl.pallas_call(
        _out_ln_d_kernel,
        grid=(n // tm,),
        in_specs=[row(512)] * 6 + [row(D_MODEL), pl.BlockSpec((D_MODEL, D_MODEL), lambda i: (0, 0)),
                                   pl.BlockSpec((D_W, D_W), lambda i: (0, 0)),
                                   vec(D_W), vec(D_W), vec(D_W), vec(D_MODEL), vec(D_MODEL)],
        out_specs=row(D_MODEL),
        out_shape=jax.ShapeDtypeStruct((n, D_MODEL), F32),
        compiler_params=_params("parallel"),
        name="out_ln_d",
    )(oc, od, r, kr, v, gate, x2, w_out, _head_mask(), lnx_g, lnx_b, r_k, g, b)


MOE_EXPERTS_PER_STEP = 4


def _moe_kernel(x_ref, rw_ref, rb_ref, w13_ref, w2_ref, g_ref, b_ref, y_ref, acc_ref, gate_ref, xb_ref):
    e = pl.program_id(1)
    tm = x_ref.shape[0]
    lane = lax.broadcasted_iota(jnp.int32, (tm, N_EXPERTS), 1)

    lane_f = lane.astype(F32)

    def first_argmax(v):
        m = jnp.max(v, axis=1, keepdims=True)
        idx = jnp.min(jnp.where(v == m, lane_f, float(N_EXPERTS)), axis=1, keepdims=True)
        return m, idx.astype(jnp.int32)

    @pl.when(e == 0)
    def _():
        x = x_ref[...]
        xb_ref[...] = x.astype(BF16)
        aff = _sigmoid(_dot_hi(x, rw_ref[...]))
        biased = aff + rb_ref[...]
        best = jnp.zeros((tm, 1), jnp.int32)
        best_score = jnp.full((tm, 1), -jnp.inf, F32)
        for grp in range(N_GROUPS):
            vg = jnp.where(lane // EXPERTS_PER_GROUP == grp, biased, -jnp.inf)
            top1, idx1 = first_argmax(vg)
            top2, _ = first_argmax(jnp.where(lane == idx1, -jnp.inf, vg))
            score = top1 + top2
            better = score > best_score
            best = jnp.where(better, grp, best)
            best_score = jnp.where(better, score, best_score)
        masked = jnp.where(lane // EXPERTS_PER_GROUP == best, biased, -jnp.inf)
        _, idx1 = first_argmax(masked)
        _, idx2 = first_argmax(jnp.where(lane == idx1, -jnp.inf, masked))
        top_aff = jnp.where((lane == idx1) | (lane == idx2), aff, 0.0)
        gate_ref[...] = top_aff / jnp.sum(top_aff, axis=1, keepdims=True)
        acc_ref[...] = jnp.zeros(acc_ref.shape, F32)

    for k in range(MOE_EXPERTS_PER_STEP):
        h13 = _dot(xb_ref[...], w13_ref[k])
        h1 = h13[:, 0:D_EXPERT]
        h = (h1 * _sigmoid(h1)) * h13[:, D_EXPERT:2 * D_EXPERT]
        gate_e = jnp.sum(jnp.where(lane == e * MOE_EXPERTS_PER_STEP + k, gate_ref[...], 0.0), axis=1, keepdims=True)
        acc_ref[...] += gate_e * _dot(h.astype(BF16), w2_ref[k])

    @pl.when(e == pl.num_programs(1) - 1)
    def _():
        y_ref[...] = _layernorm_rows(ALPHA * x_ref[...] + acc_ref[...], g_ref[...], b_ref[...])


def _moe_ln(x2, router_w, router_b, w13, w2, g, b, tm):
    n = x2.shape[0]
    row = pl.BlockSpec((tm, D_MODEL), lambda i, e: (i, 0))
    vec = pl.BlockSpec((1, D_MODEL), lambda i, e: (0, 0))
    return pl.pallas_call(
        _moe_kernel,
        grid=(n // tm, N_EXPERTS // MOE_EXPERTS_PER_STEP),
        in_specs=[row,
                  pl.BlockSpec((D_MODEL, N_EXPERTS), lambda i, e: (0, 0)),
                  pl.BlockSpec((1, N_EXPERTS), lambda i, e: (0, 0)),
                  pl.BlockSpec((MOE_EXPERTS_PER_STEP, D_MODEL, 2 * D_EXPERT), lambda i, e: (e, 0, 0)),
                  pl.BlockSpec((MOE_EXPERTS_PER_STEP, D_EXPERT, D_MODEL), lambda i, e: (e, 0, 0)),
                  vec, vec],
        out_specs=row,
        out_shape=jax.ShapeDtypeStruct((n, D_MODEL), F32),
        scratch_shapes=[pltpu.VMEM((tm, D_MODEL), F32), pltpu.VMEM((tm, N_EXPERTS), F32),
                        pltpu.VMEM((tm, D_MODEL), BF16)],
        compiler_params=_params("parallel", "arbitrary"),
        name="moe_ln",
    )(x2, router_w, router_b, w13, w2, g, b)


def _proj_cd_kernel(x_ref, w_ref, cq_ref, ck_ref, cv_ref, pd_ref, ck16_ref, cv16_ref):
    xb = x_ref[...].astype(BF16)
    cq_ref[...] = _dot(xb, w_ref[:, 0:512])
    ck = _dot(xb, w_ref[:, 512:1024])
    ck_ref[...] = ck
    ck16_ref[...] = ck.astype(BF16)
    cv = _dot(xb, w_ref[:, 1024:1536])
    cv_ref[...] = cv
    cv16_ref[...] = cv.astype(BF16)
    pd_ref[...] = _dot(xb, w_ref[:, 1536:1536 + D_SHIFT_W])


def _proj_cd(x2, w_bf16, tm):
    n = x2.shape[0]
    row = lambda w: pl.BlockSpec((tm, w), lambda i: (i, 0))
    outs = [512, 512, 512, D_SHIFT_W]
    return pl.pallas_call(
        _proj_cd_kernel,
        grid=(n // tm,),
        in_specs=[row(D_MODEL), pl.BlockSpec((D_MODEL, 1536 + D_SHIFT_W), lambda i: (0, 0))],
        out_specs=[row(w) for w in outs] + [row(512), row(512)],
        out_shape=[jax.ShapeDtypeStruct((n, w), F32) for w in outs] + [jax.ShapeDtypeStruct((n, 512), BF16)] * 2,
        compiler_params=_params("parallel"),
        name="proj_cd",
    )(x2, w_bf16)


STICK_HEADS_PER_STEP = 2


def _stick_kernel(q_ref, k_ref, v_ref, ust_ref, o_ref, *, past, tq, kb):
    i = pl.program_id(2)
    q0 = past + i * tq
    n_kb = (q0 + tq + kb - 1) // kb
    qpos = q0 + lax.broadcasted_iota(jnp.int32, (tq, 1), 0)
    lane_kb = lax.broadcasted_iota(jnp.int32, (1, kb), 1)
    n_heads = STICK_HEADS_PER_STEP
    cols =[slice(hh * HEAD_DIM, (hh + 1) * HEAD_DIM) for hh in range(n_heads)]
    qs = [(q_ref[:, c] * (HEAD_DIM ** -0.5)).astype(BF16) for c in cols]

    def cond(c):
        j, carries, _ = c
        worst = carries[0]
        for carry in carries[1:]:
            worst = jnp.maximum(worst, carry)
        return jnp.logical_and(j >= 0, jnp.max(worst) > EXP_ZERO_BELOW)

    def body(c):
        j, carries, outs = c
        off = pl.multiple_of(j * kb, kb)
        strict = off + lane_kb < qpos
        new_carries, new_outs = [], []
        for hh in range(n_heads):
            k = k_ref[pl.ds(off, kb), cols[hh]]
            v = v_ref[pl.ds(off, kb), cols[hh]]
            z = lax.dot_general(qs[hh], k, NT_DIMS, preferred_element_type=F32)
            sp = _softplus(z)
            log_keep = jnp.where(strict, -sp, 0.0)
            later = carries[hh] + _dot_exact_rhs2(log_keep, ust_ref[...])
            a = jnp.where(strict, jnp.exp((z - sp) + later), 0.0)
            new_outs.append(outs[hh] + _dot(a.astype(BF16), v))
            new_carries.append(carries[hh] + jnp.sum(log_keep, axis=1, keepdims=True))
        return j - 1, tuple(new_carries), tuple(new_outs)

    init = (n_kb - 1, tuple(jnp.zeros((tq, 1), F32) for _ in cols), tuple(jnp.zeros((tq, HEAD_DIM), F32) for _ in cols))
    _, _, outs = lax.while_loop(cond, body, init)
    for hh in range(n_heads):
        o_ref[:, cols[hh]] = outs[hh]


def _stick(cq, k_all, v_all, bsz, t_len, past, kb):
    tq = min(kb, t_len)
    nq = t_len // tq
    lp = k_all.shape[1]
    ust = jnp.tril(jnp.ones((kb, kb), F32), -1).astype(BF16)
    width = STICK_HEADS_PER_STEP * HEAD_DIM
    qrow = pl.BlockSpec((tq, width), lambda b, hp, i: (b * nq + i, hp))
    keys = pl.BlockSpec((None, lp, width), lambda b, hp, i: (b, 0, hp))
    return pl.pallas_call(
        functools.partial(_stick_kernel, past=past, tq=tq, kb=kb),
        grid=(bsz, C_HEADS // STICK_HEADS_PER_STEP, nq),
        in_specs=[qrow, keys, keys, pl.BlockSpec((kb, kb), lambda b, hp, i: (0, 0))],
        out_specs=qrow,
        out_shape=jax.ShapeDtypeStruct((bsz * t_len, 512), F32),
        compiler_params=_params("parallel", "parallel", "arbitrary"),
        name="stick",
    )(cq, k_all, v_all, ust)


def _rwkv_pre_kernel(pd_ref, shift_ref, mu_ref, w0a0_ref, wa2_ref, gw2_ref, kk_ref_, ka_ref, hm_ref,
                     r_out, w_out, kr_out, v_out, kk_out, b_out, g_out, last_ref):
    t = pl.program_id(1)

    @pl.when(t == 0)
    def _():
        last_ref[...] = shift_ref[...]

    pd = pd_ref[...]
    tm = pd.shape[0]
    rolled = pltpu.roll(pd, 1, 0)
    first_row = lax.broadcasted_iota(jnp.int32, (tm, 1), 0) == 0
    prev = jnp.where(first_row, last_ref[...], rolled)
    last_ref[...] = pd[tm - 1:tm, :]
    pm = pd + (prev - pd) * mu_ref[...]
    r = pm[:, 0:512]
    k = pm[:, 512:1024]
    v = pm[:, 1024:1536]
    lwa = pm[:, 1536:1664]
    lg = pm[:, 1664:1792]
    lane = lax.broadcasted_iota(jnp.int32, lwa.shape, 1)
    lwa = jnp.where(lane < D_LORA_W, jnp.tanh(lwa), lwa)
    pre = w0a0_ref[...] + _dot_f32x3(lwa, wa2_ref[...])
    w_log = -_softplus(-pre[:, 0:512]) - 0.5
    decay = jnp.exp(-jnp.exp(w_log))
    a = _sigmoid(pre[:, 512:1024])
    g = _dot_f32x3(_sigmoid(lg), gw2_ref[...])
    kk = k * kk_ref_[...]
    kk = kk * lax.rsqrt(_dot_exact_rhs(kk * kk, hm_ref[...]) + 1e-12)
    r_out[...] = r
    w_out[...] = decay
    kr_out[...] = k * (1.0 + (a - 1.0) * ka_ref[...])
    v_out[...] = v
    kk_out[...] = kk
    b_out[...] = kk * a
    g_out[...] = g


def _rwkv_pre(pd, shift, mu, w0a0, wa2, gw2, k_k, k_a, bsz, t_len, tm):
    nt = t_len // tm
    row = lambda w: pl.BlockSpec((tm, w), lambda b, t: (b * nt + t, 0))
    const = lambda s: pl.BlockSpec(s, lambda b, t: (0,) * len(s))
    return pl.pallas_call(
        _rwkv_pre_kernel,
        grid=(bsz, nt),
        in_specs=[row(D_SHIFT_W), pl.BlockSpec((None, 1, D_SHIFT_W), lambda b, t: (b, 0, 0)),
                  const((1, D_SHIFT_W)), const((1, 2 * D_W)), const((LANES, 2 * D_W)), const((D_LORA_G, D_W)),
                  const((1, D_W)), const((1, D_W)), const((D_W, D_W))],
        out_specs=[row(D_W)] * 7,
        out_shape=[jax.ShapeDtypeStruct((bsz * t_len, D_W), F32)] * 7,
        scratch_shapes=[pltpu.VMEM((1, D_SHIFT_W), F32)],
        compiler_params=_params("parallel", "arbitrary"),
        name="rwkv_pre",
    )(pd, shift, mu, w0a0, wa2, gw2, k_k, k_a, _head_mask())


RWKV_KL = HEAD_DIM // 2
RWKV_VALUE_GROUPS = 2


def _rwkv_scan_kernel(kk_ref, w_ref, b_ref, kr_ref, r_ref, v_ref, s0_ref, o_ref, s_out_ref, s_ref, *, tb):
    g = pl.program_id(0)

    @pl.when(g == 0)
    def _():
        s_ref[...] = s0_ref[...]

    def both_halves(x):
        return x + pltpu.roll(x, LANES // 2, 1)

    def step(t, carry):
        nv = HEAD_DIM // RWKV_VALUE_GROUPS
        for vg in range(RWKV_VALUE_GROUPS):
            vrows = slice(vg * nv, (vg + 1) * nv)
            acc = [jnp.zeros((nv, LANES), F32) for _ in range(2)]
            for kl in range(RWKV_KL):
                acc[kl % 2] = acc[kl % 2] + s_ref[kl, vrows, :] * kk_ref[t, kl:kl + 1, :]
            sa = both_halves(acc[0] + acc[1])
            v = v_ref[t, vrows, :]
            out = [jnp.zeros((nv, LANES), F32) for _ in range(2)]
            for kl in range(RWKV_KL):
                row = slice(kl, kl + 1)
                s = s_ref[kl, vrows, :] * w_ref[t, row, :] - sa * b_ref[t, row, :] + v * kr_ref[t, row, :]
                s_ref[kl, vrows, :] = s
                out[kl % 2] = out[kl % 2] + s * r_ref[t, row, :]
            o_ref[t, vrows, :] = both_halves(out[0] + out[1])
        return carry

    lax.fori_loop(0, tb, step, 0)

    @pl.when(g == pl.num_programs(0) - 1)
    def _():
        s_out_ref[...] = s_ref[...]


def _rwkv_scan(kk, w, b, kr, r, v, s0, t_len, tb):
    keyed = pl.BlockSpec((tb, RWKV_KL, LANES), lambda g: (g, 0, 0))
    valued = pl.BlockSpec((tb, HEAD_DIM, LANES), lambda g: (g, 0, 0))
    state = pl.BlockSpec((RWKV_KL, HEAD_DIM, LANES), lambda g: (0, 0, 0))
    return pl.pallas_call(
        functools.partial(_rwkv_scan_kernel, tb=tb),
        grid=(t_len // tb,),
        in_specs=[keyed] * 5 + [valued, state],
        out_specs=[valued, state],
        out_shape=[jax.ShapeDtypeStruct((t_len, HEAD_DIM, LANES), F32),
                   jax.ShapeDtypeStruct((RWKV_KL, HEAD_DIM, LANES), F32)],
        scratch_shapes=[pltpu.VMEM((RWKV_KL, HEAD_DIM, LANES), F32)],
        compiler_params=_params("arbitrary"),
        name="rwkv_scan",
    )(kk, w, b, kr, r, v, s0)


def _to_scan_keyed(x2, bsz, t_len):
    y = x2.reshape(bsz, t_len, D_HEADS, 2, RWKV_KL).transpose(1, 4, 3, 0, 2)
    return y.reshape(t_len, RWKV_KL, 2 * bsz * D_HEADS)


def _to_scan_valued(x2, bsz, t_len):
    y = x2.reshape(bsz, t_len, D_HEADS, HEAD_DIM).transpose(1, 3, 0, 2).reshape(t_len, HEAD_DIM, bsz * D_HEADS)
    return jnp.concatenate([y, y], axis=-1)


def _from_scan_valued(y, bsz, t_len):
    y = y[:, :, 0:bsz * D_HEADS].reshape(t_len, HEAD_DIM, bsz, D_HEADS)
    return y.transpose(2, 0, 3, 1).reshape(bsz * t_len, D_W)


def _state_to_scan(s, bsz):
    y = s.reshape(bsz, D_HEADS, HEAD_DIM, 2, RWKV_KL).transpose(4, 2, 3, 0, 1)
    return y.reshape(RWKV_KL, HEAD_DIM, 2 * bsz * D_HEADS)


def _state_from_scan(y, bsz):
    return y.reshape(RWKV_KL, HEAD_DIM, 2, bsz, D_HEADS).transpose(3, 4, 1, 2, 0).reshape(
        bsz, D_HEADS, HEAD_DIM, HEAD_DIM)


def _pad_keys(x, mult):
    pad = (-x.shape[1]) % mult
    return x if pad == 0 else jnp.pad(x, ((0, 0), (0, pad), (0, 0)))


def _tile(n, cap):
    return min(n, cap)


def _run_group(x, past, p):
    bsz, t_len, _ = x.shape
    assert bsz * D_HEADS * 2 == LANES, "rwkv scan packs (key half, batch, head) into the lane axis"
    n = bsz * t_len
    x2 = x.reshape(n, D_MODEL)
    past_len = 0 if past is None else past[0].shape[2]
    tm = _tile(t_len, 512)

    cos, sin = _rope_tables(past_len, t_len)
    aq, iq, bqk, kik, aviw, bv, bg, kik16, v16 = _proj_ab(x2, p['w_in_ab'], cos, sin, t_len, tm)
    ak = kik[:, 0:64].reshape(1, bsz, t_len, 64)
    ik = kik[:, 64:128].reshape(1, bsz, t_len, 64)
    av = aviw[:, 0:64].reshape(1, bsz, t_len, 64)
    kik_all = kik16.reshape(bsz, t_len, LANES)
    v_all = v16.reshape(bsz, t_len, LANES)
    ones_col = (jnp.arange(LANES) == 64).astype(F32)
    if past is None:
        s_b = jnp.zeros((bsz, B_HEADS, B_DK, B_DV), F32)
    else:
        pk, pv, pik, sb = past[0][0], past[1][0], past[2][0], past[3][0]
        kik_all = jnp.concatenate([jnp.concatenate([pk, pik], axis=-1).astype(BF16), kik_all], axis=1)
        pv_slab = jnp.concatenate([pv, jnp.broadcast_to(ones_col[64:], pv.shape)], axis=-1)
        v_all = jnp.concatenate([pv_slab.astype(BF16), v_all], axis=1)
        s_b = sb
    kt = 1024
    o_a = _dsa(aq, iq, aviw, _pad_keys(kik_all, kt), _pad_keys(v_all, kt), bsz, t_len, past_len, kt)
    o_b, s_b_new = _retention(bqk, bv, bg, s_b, p['b_gn'], bsz, t_len, _tile(t_len, 2 * CHUNK))
    x2 = _out_ln(o_a, o_b, x2, p['w_out_ab'], p['ln_g'][0, 0][None], p['ln_b'][0, 0][None], tm)
    tm_moe = _tile(n, 1024)
    x2 = _moe_ln(x2, p['router_w'], p['router_b'], p['w13'][0], p['w2'][0],
                 p['ln_g'][0, 1][None], p['ln_b'][0, 1][None], tm_moe)

    cq, ck, cv, pd, ck16, cv16 = _proj_cd(x2, p['w_in_cd'], tm)
    ck_all = ck16.reshape(bsz, t_len, 512)
    cv_all = cv16.reshape(bsz, t_len, 512)
    if past is None:
        s_d = jnp.zeros((bsz, D_HEADS, HEAD_DIM, HEAD_DIM), F32)
        shift = jnp.zeros((bsz, 1, D_SHIFT_W), F32)
    else:
        ck_all = jnp.concatenate([past[4][0].reshape(bsz, past_len, 512).astype(BF16), ck_all], axis=1)
        cv_all = jnp.concatenate([past[5][0].reshape(bsz, past_len, 512).astype(BF16), cv_all], axis=1)
        s_d, shift = past[6][0], past[7][0]
    kb = 256
    o_c = _stick(cq, _pad_keys(ck_all, kb), _pad_keys(cv_all, kb), bsz, t_len, past_len, kb)
    r, w, kr, v, kk, b, g = _rwkv_pre(pd, shift, p['d_mu'], p['d_w0a0'], p['d_wa2'], p['d_g2'],
                                      p['d_k_k'], p['d_k_a'], bsz, t_len, tm)
    keyed = [_to_scan_keyed(u, bsz, t_len) for u in (kk, w, b, kr, r)]
    o_scan, s_scan = _rwkv_scan(*keyed, _to_scan_valued(v, bsz, t_len), _state_to_scan(s_d, bsz),
                                t_len, _tile(t_len, 64))
    o_d = _from_scan_valued(o_scan, bsz, t_len)
    s_d_new = _state_from_scan(s_scan, bsz)
    x2 = _out_ln_d(o_c, o_d, r, kr, v, g, x2, p['w_out_cd'], p['d_lnx_g'], p['d_lnx_b'], p['d_r_k'],
                   p['ln_g'][1, 0][None], p['ln_b'][1, 0][None], tm)
    x2 = _moe_ln(x2, p['router_w'], p['router_b'], p['w13'][1], p['w2'][1],
                 p['ln_g'][1, 1][None], p['ln_b'][1, 1][None], tm_moe)

    states = (ak, av, ik, s_b_new[None],
              ck.reshape(1, bsz, t_len, C_HEADS, HEAD_DIM), cv.reshape(1, bsz, t_len, C_HEADS, HEAD_DIM),
              s_d_new[None], pd.reshape(bsz, t_len, D_SHIFT_W)[:, -1:][None])
    return x2.reshape(bsz, t_len, D_MODEL), states


def kernel(x_prompt, x_sample, cache_a_k, cache_a_v, cache_a_idx_k, state_b, cache_c_k, cache_c_v, state_d, state_d_shift, w_in_ab, w_out_ab, b_gn, w_in_cd, w_out_cd, d_mu, d_w0, d_w2, d_a0, d_a2, d_g2, d_k_k, d_k_a, d_r_k, d_lnx_g, d_lnx_b, ln_g, ln_b, router_w, router_b, moe_w1, moe_w3, moe_w2):
    zeros_w = jnp.zeros((D_LORA_W, D_W), F32)
    p = {
        'w_in_ab': _pack_w_in_ab(w_in_ab[0]),
        'w_out_ab': w_out_ab[0].astype(BF16),
        'b_gn': b_gn,
        'w_in_cd': w_in_cd[0].astype(BF16),
        'w_out_cd': w_out_cd[0].astype(BF16),
        'd_mu': d_mu,
        'd_w0a0': jnp.concatenate([d_w0, d_a0], axis=1),
        'd_wa2': jnp.concatenate([jnp.concatenate([d_w2[0], zeros_w], axis=1),
                                  jnp.concatenate([zeros_w, d_a2[0]], axis=1)], axis=0),
        'd_g2': d_g2[0],
        'd_k_k': d_k_k, 'd_k_a': d_k_a, 'd_r_k': d_r_k, 'd_lnx_g': d_lnx_g, 'd_lnx_b': d_lnx_b,
        'ln_g': ln_g, 'ln_b': ln_b,
        'router_w': router_w, 'router_b': router_b[None],
        'w13': jnp.concatenate([moe_w1, moe_w3], axis=-1).astype(BF16),
        'w2': moe_w2.astype(BF16),
    }
    y_p, sp = _run_group(x_prompt, None, p)
    past = (cache_a_k, cache_a_v, cache_a_idx_k, state_b, cache_c_k, cache_c_v, state_d, state_d_shift)
    y_s, ss = _run_group(x_sample, past, p)
    return (y_p, y_s, sp[0], sp[1], sp[2], ss[0], ss[1], ss[2], sp[3], ss[3], sp[4], sp[5], ss[4], ss[5],
            sp[6], ss[6], sp[7], ss[7])
```

```python
import functools
import math

import jax
import jax.numpy as jnp
import numpy as np
from jax import lax
from jax.experimental import pallas as pl
from jax.experimental.pallas import tpu as pltpu

F32 = jnp.float32
BF16 = jnp.bfloat16

D_MODEL = 1024
CHUNK = 64
DSA_QUERY_ROWS = 128
DSA_MATMUL_ROWS = 128
ROPE_THETA = 10000.0
HEAD_DIM = 64
LN_EPS = 1e-5
A_HEADS = 8
IDX_HEADS = 8
TOPK_MAX = 256
B_HEADS = 4
B_DK = 64
B_DV = 128
C_HEADS = 8
D_HEADS = 8
D_LORA_W = 64
D_LORA_A = 64
D_LORA_G = 128
D_GN_EPS = 64e-5
N_EXPERTS = 16
N_GROUPS = 4
EXPERTS_PER_GROUP = 4
D_EXPERT = 256
DEPTH = 2
ALPHA = (2 * DEPTH) ** 0.25
D_W = D_HEADS * HEAD_DIM
D_SHIFT_W = 3 * D_W + D_LORA_W + D_LORA_A + D_LORA_G

LANES = 128
SUBLANES = 8
VMEM_LIMIT_BYTES = 56 * 1024 * 1024

SHIFT_MARGIN = 1.001
MIN_SOFTMAX_SUM = 1e-30
EXP_ZERO_BELOW = -104.0

NT_DIMS = (((1,), (1,)), ((), ()))
TN_DIMS = (((0,), (0,)), ((), ()))


def _params(*sem):
    return pltpu.CompilerParams(dimension_semantics=sem, vmem_limit_bytes=VMEM_LIMIT_BYTES)


def _dot(a, b):
    return jnp.dot(a, b, preferred_element_type=F32)


def _dot_hi(a, b):
    return jnp.dot(a, b, preferred_element_type=F32, precision=lax.Precision.HIGHEST)


def _dot_f32x3(a, b, dims=(((1,), (0,)), ((), ()))):
    a_hi = a.astype(BF16)
    b_hi = b.astype(BF16)
    a_lo = (a - a_hi.astype(F32)).astype(BF16)
    b_lo = (b - b_hi.astype(F32)).astype(BF16)
    dot = lambda x, y: lax.dot_general(x, y, dims, preferred_element_type=F32)
    return dot(a_hi, b_hi) + dot(a_hi, b_lo) + dot(a_lo, b_hi)


def _split3(x):
    h1 = x.astype(BF16)
    r1 = x - h1.astype(F32)
    h2 = r1.astype(BF16)
    r2 = r1 - h2.astype(F32)
    return h1, h2, r2.astype(BF16)


def _dot_exact_rhs(x, m01):
    h1, h2, h3 = _split3(x)
    return _dot(h1, m01) + _dot(h2, m01) + _dot(h3, m01)


def _dot_exact_rhs2(x, m01):
    h1 = x.astype(BF16)
    h2 = (x - h1.astype(F32)).astype(BF16)
    return _dot(h1, m01) + _dot(h2, m01)


def _layernorm_rows(x, g, b):
    mu = jnp.mean(x, axis=-1, keepdims=True)
    xc = x - mu
    var = jnp.mean(xc * xc, axis=-1, keepdims=True)
    return xc * lax.rsqrt(var + LN_EPS) * g + b


def _sigmoid(x):
    return 1.0 / (1.0 + jnp.exp(-x))


def _softplus(x):
    return jnp.maximum(x, 0.0) + jnp.log(1.0 + jnp.exp(-jnp.abs(x)))


def _rope_slab(x, cos, sin_signed):
    lane = lax.broadcasted_iota(jnp.int32, x.shape, 1)
    first_half = (lane % HEAD_DIM) < (HEAD_DIM // 2)
    swapped = jnp.where(first_half, pltpu.roll(x, LANES - HEAD_DIM // 2, 1), pltpu.roll(x, HEAD_DIM // 2, 1))
    return x * cos + swapped * sin_signed


AB_ROPED = 1664
AB_PACKED = 2816


def _pack_w_in_ab(w):
    aq, ak, av, iq, ik, iw, bq, bk, bv, bg = jnp.split(
        w, [512, 576, 640, 1152, 1216, 1224, 1480, 1736, 2248], axis=1)
    pad = jnp.zeros((w.shape[0], LANES - 64 - IDX_HEADS), w.dtype)
    return jnp.concatenate([aq, iq, bq, bk, ak, ik, av, iw, pad, bv, bg], axis=1).astype(BF16)


def _proj_ab_kernel(x_ref, w_ref, cos_ref, sin_ref, aq_ref, iq_ref, bqk_ref, kik_ref, aviw_ref, bv_ref, bg_ref,
                    kik16_ref, v16_ref):
    xb = x_ref[...].astype(BF16)
    cos = cos_ref[...]
    sin = sin_ref[...]

    def roped(col0, width, scale_from=None):
        y = _dot(xb, w_ref[:, col0:col0 + width])
        parts = []
        for c in range(width // LANES):
            slab = _rope_slab(y[:, c * LANES:(c + 1) * LANES], cos, sin)
            if scale_from is not None and c * LANES >= scale_from:
                slab = slab * (B_DK ** -0.5)
            parts.append(slab)
        return parts

    for c, slab in enumerate(roped(0, 512)):
        aq_ref[:, c * LANES:(c + 1) * LANES] = slab
    for c, slab in enumerate(roped(512, 512)):
        iq_ref[:, c * LANES:(c + 1) * LANES] = slab
    for c, slab in enumerate(roped(1024, 512, scale_from=256)):
        bqk_ref[:, c * LANES:(c + 1) * LANES] = slab
    kik = roped(1536, LANES)[0]
    kik_ref[...] = kik
    kik16_ref[...] = kik.astype(BF16)
    aviw = _dot(xb, w_ref[:, AB_ROPED:AB_ROPED + LANES])
    lane = lax.broadcasted_iota(jnp.int32, aviw.shape, 1)
    is_iw = (lane >= 64) & (lane < 64 + IDX_HEADS)
    aviw_ref[...] = jnp.where(is_iw, aviw * ((IDX_HEADS * HEAD_DIM) ** -0.5), aviw)
    v16_ref[...] = jnp.where(lane < 64, aviw, jnp.where(lane == 64, 1.0, 0.0)).astype(BF16)
    bv_ref[...] = _dot(xb, w_ref[:, 1792:2304])
    bg_ref[...] = _dot(xb, w_ref[:, 2304:2816])


def _proj_ab(x2, w_packed, cos, sin, t_len, tm):
    n = x2.shape[0]
    nt = t_len // tm
    row = lambda w: pl.BlockSpec((tm, w), lambda i: (i, 0))
    tab = pl.BlockSpec((tm, LANES), lambda i: (i % nt, 0))
    outs = [512, 512, 512, LANES, LANES, 512, 512]
    return pl.pallas_call(
        _proj_ab_kernel,
        grid=(n // tm,),
        in_specs=[row(D_MODEL), pl.BlockSpec((D_MODEL, AB_PACKED), lambda i: (0, 0)), tab, tab],
        out_specs=[row(w) for w in outs] + [row(LANES), row(LANES)],
        out_shape=[jax.ShapeDtypeStruct((n, w), F32) for w in outs] + [jax.ShapeDtypeStruct((n, LANES), BF16)] * 2,
        compiler_params=_params("parallel"),
        name="proj_ab",
    )(x2, w_packed, cos, sin)


def _rope_tables(past, t_len):
    half = HEAD_DIM // 2
    inv = ROPE_THETA ** (-jnp.arange(half, dtype=F32) / half)
    ang = (past + jnp.arange(t_len)).astype(F32)[:, None] * inv[None, :]
    c, s = jnp.cos(ang), jnp.sin(ang)
    return jnp.concatenate([c, c, c, c], axis=1), jnp.concatenate([-s, s, -s, s], axis=1)


def _dsa_kernel(aq_ref, iq_ref, aviw_ref, kik_ref, v_ref, tri_ref, o_ref,
                skey_ref, skey_t_ref, half_ref, iqs_ref, iwb_ref, qs_ref, p_ref, m_ref, mlane_ref, acc_ref, kmax_ref,
                *, past, qb, mr, kt, topk, transposed):
    i = pl.program_id(1)
    q0 = past + i * qb
    n_tiles = (q0 + qb + kt - 1) // kt
    row = lax.broadcasted_iota(jnp.int32, (qb, 1), 0)
    vis_end = ((q0 + row) // CHUNK + 1) * CHUNK
    lane_kt = lax.broadcasted_iota(jnp.int32, (1, kt), 1)

    @pl.when(i == 0)
    def _():
        def body(j, m):
            k = kik_ref[pl.ds(pl.multiple_of(j * kt, kt), kt), 0:64].astype(F32)
            return jnp.maximum(m, jnp.sum(k * k, axis=1, keepdims=True))
        norms = lax.fori_loop(0, kik_ref.shape[0] // kt, body, jnp.zeros((kt, 1), F32))
        kmax_ref[...] = jnp.max(norms, axis=0, keepdims=True)

    groups = qb // mr
    stack = lambda g, h: slice((g * A_HEADS + h) * mr, (g * A_HEADS + h + 1) * mr)
    for g in range(groups):
        for h in range(IDX_HEADS):
            iqs_ref[stack(g, h), :] = iq_ref[g * mr:(g + 1) * mr, h * 64:(h + 1) * 64].astype(BF16)
    for h in range(IDX_HEADS):
        iwb_ref[h] = jnp.broadcast_to(aviw_ref[:, 64 + h:65 + h], (qb, LANES))

    def score_tile(j, carry):
        off = pl.multiple_of(j * kt, kt)
        ik = kik_ref[pl.ds(off, kt), 64:128]
        for g in range(groups):
            grows = slice(g * mr, (g + 1) * mr)
            s = lax.dot_general(iqs_ref[g * IDX_HEADS * mr:(g + 1) * IDX_HEADS * mr, :], ik, NT_DIMS,
                                preferred_element_type=F32)
            parts = []
            for c in range(kt // LANES):
                a = jnp.zeros((mr, LANES), F32)
                for h in range(IDX_HEADS):
                    a = a + iwb_ref[h, grows, :] * jnp.maximum(s[h * mr:(h + 1) * mr, c * LANES:(c + 1) * LANES], 0.0)
                parts.append(a)
            acc = jnp.concatenate(parts, axis=1)
            acc = acc + 0.0
            acc = jnp.where(off + lane_kt < vis_end[grows, :], acc, -jnp.inf)
            bits = pltpu.bitcast(acc, jnp.int32)
            key = jnp.where(bits < 0, bits ^ 0x7FFFFFFF, bits)
            skey_ref[grows, pl.ds(off, kt)] = key
            if transposed:
                for c in range(kt // LANES):
                    chunk = pltpu.bitcast(key[:, c * LANES:(c + 1) * LANES], F32).T
                    key_t = pltpu.bitcast(chunk, jnp.int32)
                    rows = pl.ds(off + c * LANES, LANES)
                    skey_t_ref[rows, :] = key_t
                    half_ref[rows, :] = jnp.right_shift(key_t, 16).astype(jnp.int16)
            else:
                half_ref[grows, pl.ds(off, kt)] = jnp.right_shift(key, 16).astype(jnp.int16)
        return carry

    lax.fori_loop(0, n_tiles, score_tile, 0)

    def count(pred_fn):
        def body(j, cnt):
            off = pl.multiple_of(j * kt, kt)
            hit = pred_fn(skey_ref[:, pl.ds(off, kt)])
            for c in range(kt // LANES):
                cnt = cnt + jnp.where(hit[:, c * LANES:(c + 1) * LANES], 1.0, 0.0)
            return cnt
        cnt = lax.fori_loop(0, n_tiles, body, jnp.zeros((qb, LANES), F32))
        return jnp.sum(cnt, axis=1, keepdims=True)

    vec = (1, LANES) if transposed else (qb, 1)
    pack = 2 * SUBLANES

    def count_half_ge(cand):
        if transposed:
            cand16 = jnp.broadcast_to(cand, (pack, LANES)).astype(jnp.int16)
            one = jnp.ones((pack, LANES), jnp.int16)
            nil = jnp.zeros((pack, LANES), jnp.int16)

            def body(j, cnts):
                tile = half_ref[pl.ds(pl.multiple_of(j * kt, kt), kt), :]
                cnts = list(cnts)
                for c in range(kt // pack):
                    cnts[c % 4] = cnts[c % 4] + jnp.where(tile[c * pack:(c + 1) * pack, :] >= cand16, one, nil)
                return tuple(cnts)
            cnts = lax.fori_loop(0, n_tiles, body, (nil, nil, nil, nil))
            cnt = (cnts[0] + cnts[1]) + (cnts[2] + cnts[3])
            return jnp.sum(cnt.astype(F32), axis=0, keepdims=True)
        cand16 = jnp.broadcast_to(cand, (qb, LANES)).astype(jnp.int16)
        one = jnp.ones((qb, LANES), jnp.int16)
        nil = jnp.zeros((qb, LANES), jnp.int16)

        def body(j, cnt):
            off = pl.multiple_of(j * kt, kt)
            tile = half_ref[:, pl.ds(off, kt)]
            for c in range(kt // LANES):
                cnt = cnt + jnp.where(tile[:, c * LANES:(c + 1) * LANES] >= cand16, one, nil)
            return cnt
        cnt = lax.fori_loop(0, n_tiles, body, nil)
        return jnp.sum(cnt.astype(F32), axis=1, keepdims=True)

    def bisect16(cnt_min, extra):
        zero = jnp.zeros(vec, jnp.int32)
        cnt = extra + count_half_ge(zero)
        ok = cnt >= topk
        start = (jnp.where(ok, zero, jnp.full(vec, -(2 ** 15), jnp.int32)), jnp.where(ok, cnt, cnt_min))

        def bit_step(it, c):
            t, cnt_t = c
            cand = t | jnp.left_shift(jnp.int32(1), 14 - it)
            cnt = extra + count_half_ge(cand)
            ok = cnt >= topk
            return jnp.where(ok, cand, t), jnp.where(ok, cnt, cnt_t)
        return lax.fori_loop(0, 15, bit_step, start)

    visited = jnp.full(vec, n_tiles * kt, jnp.int32).astype(F32)
    t_hi, cnt_hi = bisect16(visited, 0.0)
    top16 = 2 ** 15 - 1
    above = jnp.where(t_hi == top16, 0.0, count_half_ge(jnp.minimum(t_hi + 1, top16)))

    def low_tile(j, carry):
        off = pl.multiple_of(j * kt, kt)
        at = (pl.ds(off, kt), slice(None)) if transposed else (slice(None), pl.ds(off, kt))
        key = skey_t_ref[at] if transposed else skey_ref[at]
        low = (key & 0xFFFF) - 2 ** 15
        half_ref[at] = jnp.where(jnp.right_shift(key, 16) == t_hi, low, -(2 ** 15)).astype(jnp.int16)
        return carry

    lax.fori_loop(0, n_tiles, low_tile, 0)
    t_lo, cnt_ge = bisect16(cnt_hi, above)
    thr = jnp.left_shift(t_hi, 16) | (t_lo + 2 ** 15)
    if transposed:
        square = pltpu.bitcast(jnp.broadcast_to(thr, (LANES, LANES)), F32).T
        thr = pltpu.bitcast(square, jnp.int32)[:, 0:1]
    lane_ok = lambda off: off + lane_kt < vis_end

    def select_all_ties(j, carry):
        off = pl.multiple_of(j * kt, kt)
        sel = (skey_ref[:, pl.ds(off, kt)] >= thr) & lane_ok(off)
        skey_ref[:, pl.ds(off, kt)] = pltpu.bitcast(jnp.where(sel, 0.0, -jnp.inf), jnp.int32)
        return carry

    def select_ranked_ties(need):
        def body(j, eq_seen):
            off = pl.multiple_of(j * kt, kt)
            key = skey_ref[:, pl.ds(off, kt)]
            eq = key == thr
            rank = _dot(jnp.where(eq, 1.0, 0.0).astype(BF16), tri_ref[...]) + eq_seen
            sel = ((key > thr) | (eq & (rank <= need))) & lane_ok(off)
            skey_ref[:, pl.ds(off, kt)] = pltpu.bitcast(jnp.where(sel, 0.0, -jnp.inf), jnp.int32)
            return rank[:, kt - 1:kt]
        return body

    def exact_fit():
        lax.fori_loop(0, n_tiles, select_all_ties, 0)

    def surplus_ties():
        need = topk - count(lambda x: x > thr)
        lax.fori_loop(0, n_tiles, select_ranked_ties(need), jnp.zeros((qb, 1), F32))

    lax.cond(jnp.max(jnp.abs(cnt_ge - topk)) == 0.0, exact_fit, surplus_ties)

    for g in range(groups):
        for h in range(A_HEADS):
            qs_ref[stack(g, h), :] = (aq_ref[g * mr:(g + 1) * mr, h * 64:(h + 1) * 64] * (HEAD_DIM ** -0.5)).astype(BF16)

    for g in range(groups):
        grows = slice(g * mr, (g + 1) * mr)
        qs_rows = slice(g * A_HEADS * mr, (g + 1) * A_HEADS * mr)
        g_tiles = (q0 + (g + 1) * mr + kt - 1) // kt

        def max_tile(j, carry):
            off = pl.multiple_of(j * kt, kt)
            bias = pltpu.bitcast(skey_ref[grows, pl.ds(off, kt)], F32)
            k = kik_ref[pl.ds(off, kt), 0:64]
            s = lax.dot_general(qs_ref[qs_rows, :], k, NT_DIMS, preferred_element_type=F32)
            for h in range(A_HEADS):
                sh = s[h * mr:(h + 1) * mr, :] + bias
                mm = sh[:, 0:LANES]
                for c in range(1, kt // LANES):
                    mm = jnp.maximum(mm, sh[:, c * LANES:(c + 1) * LANES])
                mlane_ref[h] = jnp.maximum(mlane_ref[h], mm)
            return carry

        def pv_tile(j, carry):
            off = pl.multiple_of(j * kt, kt)
            bias = pltpu.bitcast(skey_ref[grows, pl.ds(off, kt)], F32)
            k = kik_ref[pl.ds(off, kt), 0:64]
            s = lax.dot_general(qs_ref[qs_rows, :], k, NT_DIMS, preferred_element_type=F32)
            for h in range(A_HEADS):
                rows = slice(h * mr, (h + 1) * mr)
                p_ref[rows, :] = jnp.exp((s[rows, :] + bias) - m_ref[rows, :]).astype(BF16)
            acc_ref[...] += _dot(p_ref[...], v_ref[pl.ds(off, kt), :])
            return carry

        q32 = qs_ref[qs_rows, :].astype(F32)
        m_ref[...] = jnp.sqrt(jnp.sum(q32 * q32, axis=1, keepdims=True) * kmax_ref[...]) * SHIFT_MARGIN
        acc_ref[...] = jnp.zeros(acc_ref.shape, F32)
        lax.fori_loop(0, g_tiles, pv_tile, 0)

        def exact_shift():
            mlane_ref[...] = jnp.full(mlane_ref.shape, -jnp.inf, F32)
            lax.fori_loop(0, g_tiles, max_tile, 0)
            for h in range(A_HEADS):
                m_ref[h * mr:(h + 1) * mr, :] = jnp.max(mlane_ref[h], axis=1, keepdims=True)
            acc_ref[...] = jnp.zeros(acc_ref.shape, F32)
            lax.fori_loop(0, g_tiles, pv_tile, 0)

        lax.cond(jnp.min(acc_ref[:, 64:65]) >= MIN_SOFTMAX_SUM, lambda: None, exact_shift)
        for h in range(A_HEADS):
            a = acc_ref[h * mr:(h + 1) * mr, :]
            o_ref[grows, h * 64:(h + 1) * 64] = (a / a[:, 64:65])[:, 0:64]


def _dsa(aq, iq, aviw, kik_all, v_all, bsz, t_len, past, kt):
    qb = min(DSA_QUERY_ROWS, t_len)
    mr = min(DSA_MATMUL_ROWS, qb)
    transposed = qb == LANES and mr == qb
    nq = t_len // qb
    lp = kik_all.shape[1]
    topk = min(TOPK_MAX, (past + t_len) // 4)
    tri = jnp.triu(jnp.ones((kt, kt), F32)).astype(BF16)
    qrow = lambda w: pl.BlockSpec((qb, w), lambda b, i: (b * nq + i, 0))
    keys = pl.BlockSpec((None, lp, LANES), lambda b, i: (b, 0, 0))
    return pl.pallas_call(
        functools.partial(_dsa_kernel, past=past, qb=qb, mr=mr, kt=kt, topk=topk, transposed=transposed),
        grid=(bsz, nq),
        in_specs=[qrow(512), qrow(512), qrow(LANES), keys, keys, pl.BlockSpec((kt, kt), lambda b, i: (0, 0))],
        out_specs=qrow(512),
        out_shape=jax.ShapeDtypeStruct((bsz * t_len, 512), F32),
        scratch_shapes=[
            pltpu.VMEM((qb, lp), jnp.int32),
            pltpu.VMEM((lp, LANES) if transposed else (SUBLANES, LANES), jnp.int32),
            pltpu.VMEM((lp, LANES) if transposed else (qb, lp), jnp.int16),
            pltpu.VMEM((IDX_HEADS * qb, 64), BF16),
            pltpu.VMEM((IDX_HEADS, qb, LANES), F32),
            pltpu.VMEM((A_HEADS * qb, 64), BF16),
            pltpu.VMEM((A_HEADS * mr, kt), BF16),
            pltpu.VMEM((A_HEADS * mr, 1), F32),
            pltpu.VMEM((A_HEADS, mr, LANES), F32),
            pltpu.VMEM((A_HEADS * mr, LANES), F32),
            pltpu.VMEM((1, 1), F32),
        ],
        compiler_params=_params("parallel", "arbitrary"),
        name="dsa",
    )(aq, iq, aviw, kik_all, v_all, tri)


def _retention_kernel(bqk_ref, bv_ref, bg_ref, s0_ref, gn_ref, o_ref, s_out_ref, s_ref, *, n_chunks):
    t = pl.program_id(0)

    @pl.when(t == 0)
    def _():
        s_ref[...] = s0_ref[...]

    n = CHUNK
    ri = lax.broadcasted_iota(jnp.int32, (n, n), 0).astype(F32)
    ci = lax.broadcasted_iota(jnp.int32, (n, n), 1).astype(F32)
    diff = ri - ci
    pos = lax.broadcasted_iota(jnp.int32, (n, 1), 0).astype(F32)
    decays = []
    for h in range(B_HEADS):
        log_g = math.log(1.0 - 2.0 ** (-5.0 - h))
        decays.append((jnp.where(diff >= 0, jnp.exp(jnp.maximum(diff, 0.0) * log_g), 0.0),
                       jnp.exp((pos + 1.0) * log_g), jnp.exp((n - 1.0 - pos) * log_g), math.exp(n * log_g)))
    for c in range(n_chunks):
        rows = slice(c * n, (c + 1) * n)
        for b in range(bqk_ref.shape[0]):
            for h in range(B_HEADS):
                intra, q_decay, k_decay, chunk_decay = decays[h]
                q = bqk_ref[b, rows, h * B_DK:(h + 1) * B_DK]
                k = bqk_ref[b, rows, 256 + h * B_DK:256 + (h + 1) * B_DK]
                v = bv_ref[b, rows, h * B_DV:(h + 1) * B_DV]
                s = s_ref[b, h]
                scores = _dot_f32x3(q, k, NT_DIMS) * intra
                o = _dot_f32x3(scores, v) + _dot_f32x3(q, s) * q_decay
                s_ref[b, h] = s * chunk_decay + _dot_f32x3(k * k_decay, v, TN_DIMS)
                mu = jnp.mean(o, axis=-1, keepdims=True)
                oc = o - mu
                var = jnp.mean(oc * oc, axis=-1, keepdims=True)
                g = bg_ref[b, rows, h * B_DV:(h + 1) * B_DV]
                o_ref[b, rows, h * B_DV:(h + 1) * B_DV] = (
                    oc * lax.rsqrt(var + LN_EPS) * gn_ref[:, h * B_DV:(h + 1) * B_DV] * (g * _sigmoid(g)))

    @pl.when(t == pl.num_programs(0) - 1)
    def _():
        s_out_ref[...] = s_ref[...]


def _retention(bqk, bv, bg, s0, b_gn, bsz, t_len, tt):
    row = pl.BlockSpec((bsz, tt, 512), lambda t: (0, t, 0))
    state = pl.BlockSpec((bsz, B_HEADS, B_DK, B_DV), lambda t: (0, 0, 0, 0))
    o_b, s_new = pl.pallas_call(
        functools.partial(_retention_kernel, n_chunks=tt // CHUNK),
        grid=(t_len // tt,),
        in_specs=[row, row, row, state, pl.BlockSpec((1, 512), lambda t: (0, 0))],
        out_specs=[row, state],
        out_shape=[jax.ShapeDtypeStruct((bsz, t_len, 512), F32),
                   jax.ShapeDtypeStruct((bsz, B_HEADS, B_DK, B_DV), F32)],
        scratch_shapes=[pltpu.VMEM((bsz, B_HEADS, B_DK, B_DV), F32)],
        compiler_params=_params("arbitrary"),
        name="retention",
    )(bqk.reshape(bsz, t_len, 512), bv.reshape(bsz, t_len, 512), bg.reshape(bsz, t_len, 512), s0, b_gn)
    return o_b.reshape(bsz * t_len, 512), s_new


def _out_ln_kernel(oa_ref, ob_ref, x_ref, w_ref, g_ref, b_ref, y_ref):
    y = _dot(oa_ref[...].astype(BF16), w_ref[0:512, :]) + _dot(ob_ref[...].astype(BF16), w_ref[512:1024, :])
    y_ref[...] = _layernorm_rows(ALPHA * x_ref[...] + y, g_ref[...], b_ref[...])


def _out_ln(oa, ob, x2, w_out, g, b, tm):
    n = x2.shape[0]
    row = lambda w: pl.BlockSpec((tm, w), lambda i: (i, 0))
    vec = pl.BlockSpec((1, D_MODEL), lambda i: (0, 0))
    return pl.pallas_call(
        _out_ln_kernel,
        grid=(n // tm,),
        in_specs=[row(512), row(512), row(D_MODEL), pl.BlockSpec((D_MODEL, D_MODEL), lambda i: (0, 0)), vec, vec],
        out_specs=row(D_MODEL),
        out_shape=jax.ShapeDtypeStruct((n, D_MODEL), F32),
        compiler_params=_params("parallel"),
        name="out_ln",
    )(oa, ob, x2, w_out, g, b)


def _out_ln_d_kernel(oc_ref, od_ref, r_ref, kr_ref, v_ref, gate_ref, x_ref, w_ref, hm_ref,
                     lnx_g_ref, lnx_b_ref, rk_ref, g_ref, b_ref, y_ref):
    hm = hm_ref[...]
    o = od_ref[...]
    mu = _dot_exact_rhs(o, hm) * (1.0 / HEAD_DIM)
    oc = o - mu
    var = _dot_exact_rhs2(oc * oc, hm) * (1.0 / HEAD_DIM)
    normed = oc * lax.rsqrt(var + D_GN_EPS) * lnx_g_ref[...] + lnx_b_ref[...]
    v = v_ref[...]
    bonus = _dot_exact_rhs2(r_ref[...] * kr_ref[...] * rk_ref[...], hm) * v
    od = (normed + bonus) * gate_ref[...]
    y = _dot(oc_ref[...].astype(BF16), w_ref[0:512, :]) + _dot(od.astype(BF16), w_ref[512:1024, :])
    y_ref[...] = _layernorm_rows(ALPHA * x_ref[...] + y, g_ref[...], b_ref[...])


def _head_mask():
    head = jnp.arange(D_W) // HEAD_DIM
    return (head[:, None] == head[None, :]).astype(BF16)


def _out_ln_d(oc, od, r, kr, v, gate, x2, w_out, lnx_g, lnx_b, r_k, g, b, tm):
    n = x2.shape[0]
    row = lambda w: pl.BlockSpec((tm, w), lambda i: (i, 0))
    vec = lambda w: pl.BlockSpec((1, w), lambda i: (0, 0))
    return pl.pallas_call(
        _out_ln_d_kernel,
        grid=(n // tm,),
        in_specs=[row(512)] * 6 + [row(D_MODEL), pl.BlockSpec((D_MODEL, D_MODEL), lambda i: (0, 0)),
                                   pl.BlockSpec((D_W, D_W), lambda i: (0, 0)),
                                   vec(D_W), vec(D_W), vec(D_W), vec(D_MODEL), vec(D_MODEL)],
        out_specs=row(D_MODEL),
        out_shape=jax.ShapeDtypeStruct((n, D_MODEL), F32),
        compiler_params=_params("parallel"),
        name="out_ln_d",
    )(oc, od, r, kr, v, gate, x2, w_out, _head_mask(), lnx_g, lnx_b, r_k, g, b)


MOE_EXPERTS_PER_STEP = 4


def _moe_kernel(x_ref, rw_ref, rb_ref, w13_ref, w2_ref, g_ref, b_ref, y_ref, acc_ref, gate_ref, xb_ref):
    e = pl.program_id(1)
    tm = x_ref.shape[0]
    lane = lax.broadcasted_iota(jnp.int32, (tm, N_EXPERTS), 1)

    lane_f = lane.astype(F32)

    def first_argmax(v):
        m = jnp.max(v, axis=1, keepdims=True)
        idx = jnp.min(jnp.where(v == m, lane_f, float(N_EXPERTS)), axis=1, keepdims=True)
        return m, idx.astype(jnp.int32)

    @pl.when(e == 0)
    def _():
        x = x_ref[...]
        xb_ref[...] = x.astype(BF16)
        aff = _sigmoid(_dot_hi(x, rw_ref[...]))
        biased = aff + rb_ref[...]
        best = jnp.zeros((tm, 1), jnp.int32)
        best_score = jnp.full((tm, 1), -jnp.inf, F32)
        for grp in range(N_GROUPS):
            vg = jnp.where(lane // EXPERTS_PER_GROUP == grp, biased, -jnp.inf)
            top1, idx1 = first_argmax(vg)
            top2, _ = first_argmax(jnp.where(lane == idx1, -jnp.inf, vg))
            score = top1 + top2
            better = score > best_score
            best = jnp.where(better, grp, best)
            best_score = jnp.where(better, score, best_score)
        masked = jnp.where(lane // EXPERTS_PER_GROUP == best, biased, -jnp.inf)
        _, idx1 = first_argmax(masked)
        _, idx2 = first_argmax(jnp.where(lane == idx1, -jnp.inf, masked))
        top_aff = jnp.where((lane == idx1) | (lane == idx2), aff, 0.0)
        gate_ref[...] = top_aff / jnp.sum(top_aff, axis=1, keepdims=True)
        acc_ref[...] = jnp.zeros(acc_ref.shape, F32)

    for k in range(MOE_EXPERTS_PER_STEP):
        h13 = _dot(xb_ref[...], w13_ref[k])
        h1 = h13[:, 0:D_EXPERT]
        h = (h1 * _sigmoid(h1)) * h13[:, D_EXPERT:2 * D_EXPERT]
        gate_e = jnp.sum(jnp.where(lane == e * MOE_EXPERTS_PER_STEP + k, gate_ref[...], 0.0), axis=1, keepdims=True)
        acc_ref[...] += gate_e * _dot(h.astype(BF16), w2_ref[k])

    @pl.when(e == pl.num_programs(1) - 1)
    def _():
        y_ref[...] = _layernorm_rows(ALPHA * x_ref[...] + acc_ref[...], g_ref[...], b_ref[...])


def _moe_ln(x2, router_w, router_b, w13, w2, g, b, tm):
    n = x2.shape[0]
    row = pl.BlockSpec((tm, D_MODEL), lambda i, e: (i, 0))
    vec = pl.BlockSpec((1, D_MODEL), lambda i, e: (0, 0))
    return pl.pallas_call(
        _moe_kernel,
        grid=(n // tm, N_EXPERTS // MOE_EXPERTS_PER_STEP),
        in_specs=[row,
                  pl.BlockSpec((D_MODEL, N_EXPERTS), lambda i, e: (0, 0)),
                  pl.BlockSpec((1, N_EXPERTS), lambda i, e: (0, 0)),
                  pl.BlockSpec((MOE_EXPERTS_PER_STEP, D_MODEL, 2 * D_EXPERT), lambda i, e: (e, 0, 0)),
                  pl.BlockSpec((MOE_EXPERTS_PER_STEP, D_EXPERT, D_MODEL), lambda i, e: (e, 0, 0)),
                  vec, vec],
        out_specs=row,
        out_shape=jax.ShapeDtypeStruct((n, D_MODEL), F32),
        scratch_shapes=[pltpu.VMEM((tm, D_MODEL), F32), pltpu.VMEM((tm, N_EXPERTS), F32),
                        pltpu.VMEM((tm, D_MODEL), BF16)],
        compiler_params=_params("parallel", "arbitrary"),
        name="moe_ln",
    )(x2, router_w, router_b, w13, w2, g, b)


def _proj_cd_kernel(x_ref, w_ref, cq_ref, ck_ref, cv_ref, pd_ref, ck16_ref, cv16_ref):
    xb = x_ref[...].astype(BF16)
    cq_ref[...] = _dot(xb, w_ref[:, 0:512])
    ck = _dot(xb, w_ref[:, 512:1024])
    ck_ref[...] = ck
    ck16_ref[...] = ck.astype(BF16)
    cv = _dot(xb, w_ref[:, 1024:1536])
    cv_ref[...] = cv
    cv16_ref[...] = cv.astype(BF16)
    pd_ref[...] = _dot(xb, w_ref[:, 1536:1536 + D_SHIFT_W])


def _proj_cd(x2, w_bf16, tm):
    n = x2.shape[0]
    row = lambda w: pl.BlockSpec((tm, w), lambda i: (i, 0))
    outs = [512, 512, 512, D_SHIFT_W]
    return pl.pallas_call(
        _proj_cd_kernel,
        grid=(n // tm,),
        in_specs=[row(D_MODEL), pl.BlockSpec((D_MODEL, 1536 + D_SHIFT_W), lambda i: (0, 0))],
        out_specs=[row(w) for w in outs] + [row(512), row(512)],
        out_shape=[jax.ShapeDtypeStruct((n, w), F32) for w in outs] + [jax.ShapeDtypeStruct((n, 512), BF16)] * 2,
        compiler_params=_params("parallel"),
        name="proj_cd",
    )(x2, w_bf16)


STICK_HEADS_PER_STEP = 2


def _stick_kernel(q_ref, k_ref, v_ref, ust_ref, o_ref, *, past, tq, kb):
    i = pl.program_id(2)
    q0 = past + i * tq
    n_kb = (q0 + tq + kb - 1) // kb
    qpos = q0 + lax.broadcasted_iota(jnp.int32, (tq, 1), 0)
    lane_kb = lax.broadcasted_iota(jnp.int32, (1, kb), 1)
    n_heads = STICK_HEADS_PER_STEP
    cols =[slice(hh * HEAD_DIM, (hh + 1) * HEAD_DIM) for hh in range(n_heads)]
    qs = [(q_ref[:, c] * (HEAD_DIM ** -0.5)).astype(BF16) for c in cols]

    def cond(c):
        j, carries, _ = c
        worst = carries[0]
        for carry in carries[1:]:
            worst = jnp.maximum(worst, carry)
        return jnp.logical_and(j >= 0, jnp.max(worst) > EXP_ZERO_BELOW)

    def body(c):
        j, carries, outs = c
        off = pl.multiple_of(j * kb, kb)
        strict = off + lane_kb < qpos
        new_carries, new_outs = [], []
        for hh in range(n_heads):
            k = k_ref[pl.ds(off, kb), cols[hh]]
            v = v_ref[pl.ds(off, kb), cols[hh]]
            z = lax.dot_general(qs[hh], k, NT_DIMS, preferred_element_type=F32)
            sp = _softplus(z)
            log_keep = jnp.where(strict, -sp, 0.0)
            later = carries[hh] + _dot_exact_rhs2(log_keep, ust_ref[...])
            a = jnp.where(strict, jnp.exp((z - sp) + later), 0.0)
            new_outs.append(outs[hh] + _dot(a.astype(BF16), v))
            new_carries.append(carries[hh] + jnp.sum(log_keep, axis=1, keepdims=True))
        return j - 1, tuple(new_carries), tuple(new_outs)

    init = (n_kb - 1, tuple(jnp.zeros((tq, 1), F32) for _ in cols), tuple(jnp.zeros((tq, HEAD_DIM), F32) for _ in cols))
    _, _, outs = lax.while_loop(cond, body, init)
    for hh in range(n_heads):
        o_ref[:, cols[hh]] = outs[hh]


def _stick(cq, k_all, v_all, bsz, t_len, past, kb):
    tq = min(kb, t_len)
    nq = t_len // tq
    lp = k_all.shape[1]
    ust = jnp.tril(jnp.ones((kb, kb), F32), -1).astype(BF16)
    width = STICK_HEADS_PER_STEP * HEAD_DIM
    qrow = pl.BlockSpec((tq, width), lambda b, hp, i: (b * nq + i, hp))
    keys = pl.BlockSpec((None, lp, width), lambda b, hp, i: (b, 0, hp))
    return pl.pallas_call(
        functools.partial(_stick_kernel, past=past, tq=tq, kb=kb),
        grid=(bsz, C_HEADS // STICK_HEADS_PER_STEP, nq),
        in_specs=[qrow, keys, keys, pl.BlockSpec((kb, kb), lambda b, hp, i: (0, 0))],
        out_specs=qrow,
        out_shape=jax.ShapeDtypeStruct((bsz * t_len, 512), F32),
        compiler_params=_params("parallel", "parallel", "arbitrary"),
        name="stick",
    )(cq, k_all, v_all, ust)


def _rwkv_pre_kernel(pd_ref, shift_ref, mu_ref, w0a0_ref, wa2_ref, gw2_ref, kk_ref_, ka_ref, hm_ref,
                     r_out, w_out, kr_out, v_out, kk_out, b_out, g_out, last_ref):
    t = pl.program_id(1)

    @pl.when(t == 0)
    def _():
        last_ref[...] = shift_ref[...]

    pd = pd_ref[...]
    tm = pd.shape[0]
    rolled = pltpu.roll(pd, 1, 0)
    first_row = lax.broadcasted_iota(jnp.int32, (tm, 1), 0) == 0
    prev = jnp.where(first_row, last_ref[...], rolled)
    last_ref[...] = pd[tm - 1:tm, :]
    pm = pd + (prev - pd) * mu_ref[...]
    r = pm[:, 0:512]
    k = pm[:, 512:1024]
    v = pm[:, 1024:1536]
    lwa = pm[:, 1536:1664]
    lg = pm[:, 1664:1792]
    lane = lax.broadcasted_iota(jnp.int32, lwa.shape, 1)
    lwa = jnp.where(lane < D_LORA_W, jnp.tanh(lwa), lwa)
    pre = w0a0_ref[...] + _dot_f32x3(lwa, wa2_ref[...])
    w_log = -_softplus(-pre[:, 0:512]) - 0.5
    decay = jnp.exp(-jnp.exp(w_log))
    a = _sigmoid(pre[:, 512:1024])
    g = _dot_f32x3(_sigmoid(lg), gw2_ref[...])
    kk = k * kk_ref_[...]
    kk = kk * lax.rsqrt(_dot_exact_rhs(kk * kk, hm_ref[...]) + 1e-12)
    r_out[...] = r
    w_out[...] = decay
    kr_out[...] = k * (1.0 + (a - 1.0) * ka_ref[...])
    v_out[...] = v
    kk_out[...] = kk
    b_out[...] = kk * a
    g_out[...] = g


def _rwkv_pre(pd, shift, mu, w0a0, wa2, gw2, k_k, k_a, bsz, t_len, tm):
    nt = t_len // tm
    row = lambda w: pl.BlockSpec((tm, w), lambda b, t: (b * nt + t, 0))
    const = lambda s: pl.BlockSpec(s, lambda b, t: (0,) * len(s))
    return pl.pallas_call(
        _rwkv_pre_kernel,
        grid=(bsz, nt),
        in_specs=[row(D_SHIFT_W), pl.BlockSpec((None, 1, D_SHIFT_W), lambda b, t: (b, 0, 0)),
                  const((1, D_SHIFT_W)), const((1, 2 * D_W)), const((LANES, 2 * D_W)), const((D_LORA_G, D_W)),
                  const((1, D_W)), const((1, D_W)), const((D_W, D_W))],
        out_specs=[row(D_W)] * 7,
        out_shape=[jax.ShapeDtypeStruct((bsz * t_len, D_W), F32)] * 7,
        scratch_shapes=[pltpu.VMEM((1, D_SHIFT_W), F32)],
        compiler_params=_params("parallel", "arbitrary"),
        name="rwkv_pre",
    )(pd, shift, mu, w0a0, wa2, gw2, k_k, k_a, _head_mask())


RWKV_VL = HEAD_DIM // 2


def _rwkv_scan_kernel(kk_ref, w_ref, b_ref, kr_ref, r_ref, v_ref, s0_ref, o_ref, s_out_ref, s_ref, *, tb):
    g = pl.program_id(0)

    @pl.when(g == 0)
    def _():
        s_ref[...] = s0_ref[...]

    def step(t, carry):
        kk = kk_ref[t]
        w = w_ref[t]
        b = b_ref[t]
        kr = kr_ref[t]
        r = r_ref[t]
        for vl in range(RWKV_VL):
            s = s_ref[vl]
            sa = jnp.sum(s * kk, axis=0, keepdims=True)
            s = s * w - sa * b + v_ref[t, vl:vl + 1, :] * kr
            s_ref[vl] = s
            o_ref[t, vl:vl + 1, :] = jnp.sum(s * r, axis=0, keepdims=True)
        return carry

    lax.fori_loop(0, tb, step, 0)

    @pl.when(g == pl.num_programs(0) - 1)
    def _():
        s_out_ref[...] = s_ref[...]


def _rwkv_scan(kk, w, b, kr, r, v, s0, t_len, tb):
    big = pl.BlockSpec((tb, HEAD_DIM, LANES), lambda g: (g, 0, 0))
    small = pl.BlockSpec((tb, RWKV_VL, LANES), lambda g: (g, 0, 0))
    state = pl.BlockSpec((RWKV_VL, HEAD_DIM, LANES), lambda g: (0, 0, 0))
    return pl.pallas_call(
        functools.partial(_rwkv_scan_kernel, tb=tb),
        grid=(t_len // tb,),
        in_specs=[big] * 5 + [small, state],
        out_specs=[small, state],
        out_shape=[jax.ShapeDtypeStruct((t_len, RWKV_VL, LANES), F32),
                   jax.ShapeDtypeStruct((RWKV_VL, HEAD_DIM, LANES), F32)],
        scratch_shapes=[pltpu.VMEM((RWKV_VL, HEAD_DIM, LANES), F32)],
        compiler_params=_params("arbitrary"),
        name="rwkv_scan",
    )(kk, w, b, kr, r, v, s0)


def _to_scan_keyed(x2, bsz, t_len):
    y = x2.reshape(bsz, t_len, D_HEADS, HEAD_DIM).transpose(1, 3, 0, 2).reshape(t_len, HEAD_DIM, 1, bsz * D_HEADS)
    return jnp.broadcast_to(y, (t_len, HEAD_DIM, 2, bsz * D_HEADS)).reshape(t_len, HEAD_DIM, LANES)


def _to_scan_valued(x2, bsz, t_len):
    y = x2.reshape(bsz, t_len, D_HEADS, 2, RWKV_VL).transpose(1, 4, 3, 0, 2)
    return y.reshape(t_len, RWKV_VL, 2 * bsz * D_HEADS)


def _from_scan_valued(y, bsz, t_len):
    return y.reshape(t_len, RWKV_VL, 2, bsz, D_HEADS).transpose(3, 0, 4, 2, 1).reshape(bsz * t_len, D_W)


def _state_to_scan(s, bsz):
    y = s.reshape(bsz, D_HEADS, 2, RWKV_VL, HEAD_DIM).transpose(3, 4, 2, 0, 1)
    return y.reshape(RWKV_VL, HEAD_DIM, 2 * bsz * D_HEADS)


def _state_from_scan(y, bsz):
    return y.reshape(RWKV_VL, HEAD_DIM, 2, bsz, D_HEADS).transpose(3, 4, 2, 0, 1).reshape(
        bsz, D_HEADS, HEAD_DIM, HEAD_DIM)


def _pad_keys(x, mult):
    pad = (-x.shape[1]) % mult
    return x if pad == 0 else jnp.pad(x, ((0, 0), (0, pad), (0, 0)))


def _tile(n, cap):
    return min(n, cap)


def _run_group(x, past, p):
    bsz, t_len, _ = x.shape
    assert bsz * D_HEADS * 2 == LANES, "rwkv scan packs (value half, batch, head) into the lane axis"
    n = bsz * t_len
    x2 = x.reshape(n, D_MODEL)
    past_len = 0 if past is None else past[0].shape[2]
    tm = _tile(t_len, 512)

    cos, sin = _rope_tables(past_len, t_len)
    aq, iq, bqk, kik, aviw, bv, bg, kik16, v16 = _proj_ab(x2, p['w_in_ab'], cos, sin, t_len, tm)
    ak = kik[:, 0:64].reshape(1, bsz, t_len, 64)
    ik = kik[:, 64:128].reshape(1, bsz, t_len, 64)
    av = aviw[:, 0:64].reshape(1, bsz, t_len, 64)
    kik_all = kik16.reshape(bsz, t_len, LANES)
    v_all = v16.reshape(bsz, t_len, LANES)
    ones_col = (jnp.arange(LANES) == 64).astype(F32)
    if past is None:
        s_b = jnp.zeros((bsz, B_HEADS, B_DK, B_DV), F32)
    else:
        pk, pv, pik, sb = past[0][0], past[1][0], past[2][0], past[3][0]
        kik_all = jnp.concatenate([jnp.concatenate([pk, pik], axis=-1).astype(BF16), kik_all], axis=1)
        pv_slab = jnp.concatenate([pv, jnp.broadcast_to(ones_col[64:], pv.shape)], axis=-1)
        v_all = jnp.concatenate([pv_slab.astype(BF16), v_all], axis=1)
        s_b = sb
    kt = 1024
    o_a = _dsa(aq, iq, aviw, _pad_keys(kik_all, kt), _pad_keys(v_all, kt), bsz, t_len, past_len, kt)
    o_b, s_b_new = _retention(bqk, bv, bg, s_b, p['b_gn'], bsz, t_len, _tile(t_len, 2 * CHUNK))
    x2 = _out_ln(o_a, o_b, x2, p['w_out_ab'], p['ln_g'][0, 0][None], p['ln_b'][0, 0][None], tm)
    tm_moe = _tile(n, 1024)
    x2 = _moe_ln(x2, p['router_w'], p['router_b'], p['w13'][0], p['w2'][0],
                 p['ln_g'][0, 1][None], p['ln_b'][0, 1][None], tm_moe)

    cq, ck, cv, pd, ck16, cv16 = _proj_cd(x2, p['w_in_cd'], tm)
    ck_all = ck16.reshape(bsz, t_len, 512)
    cv_all = cv16.reshape(bsz, t_len, 512)
    if past is None:
        s_d = jnp.zeros((bsz, D_HEADS, HEAD_DIM, HEAD_DIM), F32)
        shift = jnp.zeros((bsz, 1, D_SHIFT_W), F32)
    else:
        ck_all = jnp.concatenate([past[4][0].reshape(bsz, past_len, 512).astype(BF16), ck_all], axis=1)
        cv_all = jnp.concatenate([past[5][0].reshape(bsz, past_len, 512).astype(BF16), cv_all], axis=1)
        s_d, shift = past[6][0], past[7][0]
    kb = 256
    o_c = _stick(cq, _pad_keys(ck_all, kb), _pad_keys(cv_all, kb), bsz, t_len, past_len, kb)
    r, w, kr, v, kk, b, g = _rwkv_pre(pd, shift, p['d_mu'], p['d_w0a0'], p['d_wa2'], p['d_g2'],
                                      p['d_k_k'], p['d_k_a'], bsz, t_len, tm)
    keyed = [_to_scan_keyed(u, bsz, t_len) for u in (kk, w, b, kr, r)]
    o_scan, s_scan = _rwkv_scan(*keyed, _to_scan_valued(v, bsz, t_len), _state_to_scan(s_d, bsz),
                                t_len, _tile(t_len, 64))
    o_d = _from_scan_valued(o_scan, bsz, t_len)
    s_d_new = _state_from_scan(s_scan, bsz)
    x2 = _out_ln_d(o_c, o_d, r, kr, v, g, x2, p['w_out_cd'], p['d_lnx_g'], p['d_lnx_b'], p['d_r_k'],
                   p['ln_g'][1, 0][None], p['ln_b'][1, 0][None], tm)
    x2 = _moe_ln(x2, p['router_w'], p['router_b'], p['w13'][1], p['w2'][1],
                 p['ln_g'][1, 1][None], p['ln_b'][1, 1][None], tm_moe)

    states = (ak, av, ik, s_b_new[None],
              ck.reshape(1, bsz, t_len, C_HEADS, HEAD_DIM), cv.reshape(1, bsz, t_len, C_HEADS, HEAD_DIM),
              s_d_new[None], pd.reshape(bsz, t_len, D_SHIFT_W)[:, -1:][None])
    return x2.reshape(bsz, t_len, D_MODEL), states


def kernel(x_prompt, x_sample, cache_a_k, cache_a_v, cache_a_idx_k, state_b, cache_c_k, cache_c_v, state_d, state_d_shift, w_in_ab, w_out_ab, b_gn, w_in_cd, w_out_cd, d_mu, d_w0, d_w2, d_a0, d_a2, d_g2, d_k_k, d_k_a, d_r_k, d_lnx_g, d_lnx_b, ln_g, ln_b, router_w, router_b, moe_w1, moe_w3, moe_w2):
    zeros_w = jnp.zeros((D_LORA_W, D_W), F32)
    p = {
        'w_in_ab': _pack_w_in_ab(w_in_ab[0]),
        'w_out_ab': w_out_ab[0].astype(BF16),
        'b_gn': b_gn,
        'w_in_cd': w_in_cd[0].astype(BF16),
        'w_out_cd': w_out_cd[0].astype(BF16),
        'd_mu': d_mu,
        'd_w0a0': jnp.concatenate([d_w0, d_a0], axis=1),
        'd_wa2': jnp.concatenate([jnp.concatenate([d_w2[0], zeros_w], axis=1),
                                  jnp.concatenate([zeros_w, d_a2[0]], axis=1)], axis=0),
        'd_g2': d_g2[0],
        'd_k_k': d_k_k, 'd_k_a': d_k_a, 'd_r_k': d_r_k, 'd_lnx_g': d_lnx_g, 'd_lnx_b': d_lnx_b,
        'ln_g': ln_g, 'ln_b': ln_b,
        'router_w': router_w, 'router_b': router_b[None],
        'w13': jnp.concatenate([moe_w1, moe_w3], axis=-1).astype(BF16),
        'w2': moe_w2.astype(BF16),
    }
    y_p, sp = _run_group(x_prompt, None, p)
    past = (cache_a_k, cache_a_v, cache_a_idx_k, state_b, cache_c_k, cache_c_v, state_d, state_d_shift)
    y_s, ss = _run_group(x_sample, past, p)
    return (y_p, y_s, sp[0], sp[1], sp[2], ss[0], ss[1], ss[2], sp[3], ss[3], sp[4], sp[5], ss[4], ss[5],
            sp[6], ss[6], sp[7], ss[7])
```
